```python
import math, functools
import jax, jax.numpy as jnp
from jax import lax
import numpy as np

D_MODEL = 2048
BATCH = 4
SEQ = 4096
DEPTH = 1
DEC_BATCH = 32
DEC_SEQ = 1
PAST_LEN = 16384
PAGE_SIZE = 128

SSM_WIDTH = D_MODEL // 2
ATT_WIDTH = D_MODEL - SSM_WIDTH
SSM_CH = 16
SSM_GROUPS = SSM_WIDTH // SSM_CH
SSM_STATE = 64
DT_MIN = 1e-3
DT_MAX = 1e-1
HEAD_DIM = 64
N_HEADS = ATT_WIDTH // HEAD_DIM
KV_GROUPS = 4
HEADS_PER_GROUP = N_HEADS // KV_GROUPS
KV_WIDTH = KV_GROUPS * HEAD_DIM
CMP_STRIDE = 16
CMP_BLOCK = 32
SLC_BLOCK = 64
N_SEL = 16
WINDOW = 512
Q_BLOCK = 128
ATTN_SCALE = HEAD_DIM ** -0.5
NEG_INF = -1e30
FORCE_BONUS = 1e4
D_FF = 256 * ((8 * D_MODEL // 3 + 255) // 256)
CONV_W = 3
EPS = 1e-6
SPLIT_SIZES = (SSM_WIDTH, SSM_WIDTH, ATT_WIDTH) + (KV_WIDTH,) * 6 + (3 * N_HEADS,)
IN_WIDTH = sum(SPLIT_SIZES)

kernel_name = 'hymba_s5_nsa_convffn_step'


def rms_norm(x, g):
    xf = x.astype(jnp.float32)
    y = xf * lax.rsqrt(jnp.mean(xf * xf, axis=-1, keepdims=True) + EPS)
    return (y * g.astype(jnp.float32)).astype(x.dtype)


def ada_modulation(c, w_ada, b_ada):
    m = jax.nn.silu(c) @ w_ada + b_ada
    return jnp.split(m[:, None, :], 6, axis=-1)


def modulate(h, shift, scale):
    return h * (1.0 + scale) + shift


def masked_softmax(s, mask):
    s = jnp.where(mask, s.astype(jnp.float32), NEG_INF)
    p = jnp.exp(s - jnp.max(s, axis=-1, keepdims=True)) * mask
    return p / jnp.maximum(jnp.sum(p, axis=-1, keepdims=True), 1e-30)


def s5_discretize(lam_re, lam_im, log_dt, b_re, b_im):
    f32 = jnp.float32
    lam_re, lam_im = lam_re.astype(f32), lam_im.astype(f32)
    dt = jnp.exp(log_dt.astype(f32))[:, None]
    mag = jnp.exp(lam_re * dt)
    ab_re, ab_im = mag * jnp.cos(lam_im * dt), mag * jnp.sin(lam_im * dt)
    den = lam_re * lam_re + lam_im * lam_im
    f_re = ((ab_re - 1.0) * lam_re + ab_im * lam_im) / den
    f_im = (ab_im * lam_re - (ab_re - 1.0) * lam_im) / den
    b_re, b_im = b_re.astype(f32), b_im.astype(f32)
    bb_re = f_re[..., None] * b_re - f_im[..., None] * b_im
    bb_im = f_re[..., None] * b_im + f_im[..., None] * b_re
    return ab_re, ab_im, bb_re, bb_im


def _s5_combine(e1, e2):
    a1r, a1i, b1r, b1i = e1
    a2r, a2i, b2r, b2i = e2
    return (a2r * a1r - a2i * a1i, a2r * a1i + a2i * a1r,
            a2r * b1r - a2i * b1i + b2r, a2r * b1i + a2i * b1r + b2i)


def s5_mix(u, ab_re, ab_im, bb_re, bb_im, c_re, c_im, d_skip, h0_re, h0_im):
    f32 = jnp.float32
    bu_re = jnp.einsum('btgc,gpc->btgp', u, bb_re)
    bu_im = jnp.einsum('btgc,gpc->btgp', u, bb_im)
    h0_re, h0_im = h0_re.astype(f32), h0_im.astype(f32)
    bu_re = bu_re.at[:, 0].add(ab_re * h0_re - ab_im * h0_im)
    bu_im = bu_im.at[:, 0].add(ab_re * h0_im + ab_im * h0_re)
    a_re = jnp.broadcast_to(ab_re, bu_re.shape)
    a_im = jnp.broadcast_to(ab_im, bu_im.shape)
    _, _, h_re, h_im = lax.associative_scan(_s5_combine, (a_re, a_im, bu_re, bu_im), axis=1)
    y = (jnp.einsum('btgp,gcp->btgc', h_re, c_re.astype(f32))
         - jnp.einsum('btgp,gcp->btgc', h_im, c_im.astype(f32))
         + d_skip.astype(f32) * u)
    return y, h_re[:, -1], h_im[:, -1]


def chunk_proj(chunks, w_cmp):
    lo = jnp.einsum('bnjsgd,sjde->bnsge', chunks, w_cmp[:, :CMP_STRIDE])
    hi = jnp.einsum('bnjsgd,sjde->bnsge', chunks, w_cmp[:, CMP_STRIDE:])
    return lo, hi


def compress_from_chunks(lo, hi, w_cmp, pe_cmp):
    bias = jnp.einsum('sjd,sjde->se', pe_cmp, w_cmp)
    return lo[:, :-1] + hi[:, 1:] + bias[None, None, :, None, :]


def cmp_branch(q, qpos, kvc):
    nc = kvc.shape[1]
    cend = jnp.arange(nc) * CMP_STRIDE + CMP_BLOCK - 1
    mask = (cend[None, :] <= qpos[:, None])[None, :, None, None, :]
    s = jnp.einsum('btgrd,bngd->btgrn', q, kvc[:, :, 0]) * ATTN_SCALE
    p = masked_softmax(s, mask)
    v = kvc[:, :, 1]
    return jnp.einsum('btgrn,bngd->btgrd', p.astype(v.dtype), v), p


def select_blocks(p, qpos, n_blk, n_sel):
    B, T, G = p.shape[:3]
    r = SLC_BLOCK // CMP_STRIDE
    imp = jnp.sum(p, axis=3)
    imp = jnp.pad(imp, ((0, 0), (0, 0), (0, 0), (0, n_blk * r - imp.shape[-1])))
    imp = imp.reshape(B, T, G, n_blk, r)
    tail = jnp.pad(imp[..., :-1, r - 1], ((0, 0), (0, 0), (0, 0), (1, 0)))
    imp = jnp.sum(imp, axis=-1) + tail
    blk = jnp.arange(n_blk)[None, :]
    cur = (qpos // SLC_BLOCK)[:, None]
    valid = (blk <= cur)[None, :, None, :]
    forced = ((blk == 0) | (blk == cur) | (blk == cur - 1))[None, :, None, :]
    score = jnp.where(valid, imp + jnp.where(forced, FORCE_BONUS, 0.0), NEG_INF)
    _, idx = lax.top_k(score, n_sel)
    return idx


def sel_branch(q, qpos, sel, idx):
    B, T, G, k = idx.shape
    kv = sel.reshape(B, T, G, k * SLC_BLOCK, 2, HEAD_DIM)
    kpos = (idx[..., None] * SLC_BLOCK + jnp.arange(SLC_BLOCK)).reshape(B, T, G, k * SLC_BLOCK)
    mask = (kpos <= qpos[None, :, None, None])[:, :, :, None, :]
    s = jnp.einsum('btgrd,btgkd->btgrk', q, kv[..., 0, :]) * ATTN_SCALE
    p = masked_softmax(s, mask)
    v = kv[..., 1, :]
    return jnp.einsum('btgrk,btgkd->btgrd', p.astype(v.dtype), v)


def win_branch(q, qpos, kvw, kpos):
    mask = ((kpos[None, :] <= qpos[:, None]) & (kpos[None, :] > qpos[:, None] - WINDOW)
            & (kpos[None, :] >= 0))[None, :, None, None, :]
    s = jnp.einsum('btgrd,bkgd->btgrk', q, kvw[:, :, 0]) * ATTN_SCALE
    p = masked_softmax(s, mask)
    v = kvw[:, :, 1]
    return jnp.einsum('btgrk,bkgd->btgrd', p.astype(v.dtype), v)


def gate_merge(gates, o_c, o_s, o_w):
    return gates[..., 0:1] * o_c + gates[..., 1:2] * o_s + gates[..., 2:3] * o_w


def nsa_prompt(q, kv_c, kv_s, kv_w, gates, w_cmp, pe_cmp):
    B, S = q.shape[:2]
    lo, hi = chunk_proj(kv_c.reshape(B, S // CMP_STRIDE, CMP_STRIDE, 2, KV_GROUPS, HEAD_DIM), w_cmp)
    kvc = compress_from_chunks(lo, hi, w_cmp, pe_cmp)
    n_blk = S // SLC_BLOCK
    n_sel = min(N_SEL, n_blk)
    slc_blocks = kv_s.reshape(B, n_blk, SLC_BLOCK, 2, KV_GROUPS, HEAD_DIM)
    win_pad = jnp.pad(kv_w, ((0, 0), (WINDOW, 0), (0, 0), (0, 0), (0, 0)))
    bi = jnp.arange(B)[:, None, None, None]
    gi = jnp.arange(KV_GROUPS)[None, None, :, None]

    def one_block(q0):
        qb = lax.dynamic_slice_in_dim(q, q0, Q_BLOCK, axis=1)
        gb = lax.dynamic_slice_in_dim(gates, q0, Q_BLOCK, axis=1)
        qpos = q0 + jnp.arange(Q_BLOCK)
        o_c, p = cmp_branch(qb, qpos, kvc)
        idx = select_blocks(p, qpos, n_blk, n_sel)
        o_s = sel_branch(qb, qpos, slc_blocks[bi, idx, :, :, gi], idx)
        kw = lax.dynamic_slice_in_dim(win_pad, q0, WINDOW + Q_BLOCK, axis=1)
        o_w = win_branch(qb, qpos, kw, q0 - WINDOW + jnp.arange(WINDOW + Q_BLOCK))
        return gate_merge(gb, o_c, o_s, o_w)

    out = lax.map(one_block, jnp.arange(0, S, Q_BLOCK))
    out = jnp.moveaxis(out, 0, 1).reshape(B, S, ATT_WIDTH)
    return out, kv_w[:, S - min(WINDOW, S):]


def nsa_sample(q, kv_c, kv_s, kv_w, gates, w_cmp, pe_cmp, cache_cmp_kv, cache_slc_kv, cache_win_kv, page_table):
    DB, T = q.shape[:2]
    n_pages = page_table.shape[1]
    past = n_pages * PAGE_SIZE
    new_pad = -(-T // SLC_BLOCK) * SLC_BLOCK
    pad = ((0, 0), (0, new_pad - T), (0, 0), (0, 0), (0, 0))
    qpos = past + jnp.arange(T)
    past_chunks = cache_cmp_kv[page_table].reshape(DB, past // CMP_STRIDE, CMP_STRIDE, 2, KV_GROUPS, HEAD_DIM)
    new_chunks = jnp.pad(kv_c, pad).reshape(DB, new_pad // CMP_STRIDE, CMP_STRIDE, 2, KV_GROUPS, HEAD_DIM)
    lo_p, hi_p = chunk_proj(past_chunks, w_cmp)
    lo_n, hi_n = chunk_proj(new_chunks, w_cmp)
    kvc = compress_from_chunks(jnp.concatenate([lo_p, lo_n], axis=1),
                               jnp.concatenate([hi_p, hi_n], axis=1), w_cmp, pe_cmp)
    o_c, p = cmp_branch(q, qpos, kvc)
    n_blk = (past + new_pad) // SLC_BLOCK
    idx = select_blocks(p, qpos, n_blk, min(N_SEL, n_blk))
    n_past_blk = past // SLC_BLOCK
    per_page = PAGE_SIZE // SLC_BLOCK
    bi = jnp.arange(DB)[:, None, None, None]
    gi = jnp.arange(KV_GROUPS)[None, None, :, None]
    pool_blocks = cache_slc_kv.reshape(-1, SLC_BLOCK, 2, KV_GROUPS, HEAD_DIM)
    jp = jnp.minimum(idx, n_past_blk - 1)
    phys = page_table[bi, jp // per_page] * per_page + jp % per_page
    from_pool = pool_blocks[phys, :, :, gi]
    new_blocks = jnp.pad(kv_s, pad).reshape(DB, new_pad // SLC_BLOCK, SLC_BLOCK, 2, KV_GROUPS, HEAD_DIM)
    jn = jnp.clip(idx - n_past_blk, 0, new_pad // SLC_BLOCK - 1)
    from_new = new_blocks[bi, jn, :, :, gi]
    sel = jnp.where((idx < n_past_blk)[..., None, None, None], from_pool, from_new.astype(from_pool.dtype))
    o_s = sel_branch(q, qpos, sel, idx)
    wbuf = cache_win_kv.shape[1]
    kw = jnp.concatenate([cache_win_kv, kv_w.astype(cache_win_kv.dtype)], axis=1)
    o_w = win_branch(q, qpos, kw, past - wbuf + jnp.arange(wbuf + T))
    out = gate_merge(gates, o_c, o_s, o_w).reshape(DB, T, ATT_WIDTH)
    return out, kw[:, T:]


def conv_ffn(h, conv_prev, w_up, conv_w, conv_b, w_down):
    T = h.shape[1]
    a = h @ w_up
    ext = jnp.concatenate([conv_prev.astype(a.dtype), a], axis=1)
    y = conv_b + conv_w[CONV_W - 1] * ext[:, CONV_W - 1:CONV_W - 1 + T]
    for k in range(CONV_W - 1):
        y = y + conv_w[k] * ext[:, k:k + T]
    val, gate = jnp.split(y, 2, axis=-1)
    return (jax.nn.silu(gate) * val) @ w_down, ext[:, T:]


def run_layer(x, c, lw, nsa_fn, h0_re, h0_im, conv_prev):
    B, T = x.shape[:2]
    sh1, sc1, gt1, sh2, sc2, gt2 = ada_modulation(c, lw['w_ada'], lw['b_ada'])
    h = modulate(rms_norm(x, lw['g_norm1']), sh1, sc1)
    z = h @ lw['w_in']
    (u, g_ssm, q, k_c, v_c, k_s, v_s, k_w, v_w, g_att) = jnp.split(
        z, np.cumsum(SPLIT_SIZES)[:-1].tolist(), axis=-1)
    ab_re, ab_im, bb_re, bb_im = s5_discretize(lw['lam_re'], lw['lam_im'], lw['log_dt'], lw['b_re'], lw['b_im'])
    uf = u.astype(jnp.float32).reshape(B, T, SSM_GROUPS, SSM_CH)
    y_ssm, h_re, h_im = s5_mix(uf, ab_re, ab_im, bb_re, bb_im, lw['c_re'], lw['c_im'],
                               lw['d_skip'].reshape(SSM_GROUPS, SSM_CH), h0_re, h0_im)
    y_ssm = jax.nn.gelu(y_ssm.reshape(B, T, SSM_WIDTH)).astype(x.dtype) * jax.nn.sigmoid(g_ssm)
    def kv(k, v):
        return jnp.stack([k.reshape(B, T, KV_GROUPS, HEAD_DIM), v.reshape(B, T, KV_GROUPS, HEAD_DIM)], axis=2)
    kv_c, kv_s, kv_w = kv(k_c, v_c), kv(k_s, v_s), kv(k_w, v_w)
    qh = q.reshape(B, T, KV_GROUPS, HEADS_PER_GROUP, HEAD_DIM)
    gates = jax.nn.sigmoid(g_att.astype(jnp.float32)).reshape(B, T, KV_GROUPS, HEADS_PER_GROUP, 3)
    y_att, win_state = nsa_fn(qh, kv_c, kv_s, kv_w, gates, lw['w_cmp'], lw['pe_cmp'])
    mixed = jnp.concatenate([rms_norm(y_ssm, lw['g_out_ssm']),
                             rms_norm(y_att, lw['g_out_att']).astype(y_ssm.dtype)], axis=-1)
    x = x + gt1 * (mixed @ lw['w_out'])
    h2 = modulate(rms_norm(x, lw['g_norm2']), sh2, sc2)
    f, conv_state = conv_ffn(h2, conv_prev, lw['w_up'], lw['conv_w'], lw['conv_b'], lw['w_down'])
    x = x + gt2 * f
    return x, (kv_c, kv_s, win_state, h_re, h_im, conv_state)


def setup_inputs(seed: int = 0) -> dict:
    key = jax.random.key(seed)
    ks = iter(jax.random.split(key, 64))
    f32 = jnp.float32

    def nrm(shape, s=1.0):
        return s * jax.random.normal(next(ks), shape, f32)

    n_pages = PAST_LEN // PAGE_SIZE
    n_pool = DEC_BATCH * n_pages + (DEC_BATCH * n_pages) // 4
    win_buf = min(WINDOW, PAST_LEN)
    f2 = 2 * D_FF
    L = DEPTH
    n_idx = jnp.arange(SSM_STATE, dtype=f32)
    page_table = jax.random.permutation(next(ks), n_pool)[:DEC_BATCH * n_pages]
    page_table = page_table.reshape(DEC_BATCH, n_pages).astype(jnp.int32)
    return {
        'x_prompt': nrm((BATCH, SEQ, D_MODEL)),
        'x_sample': nrm((DEC_BATCH, DEC_SEQ, D_MODEL)),
        'cache_cmp_kv': nrm((L, n_pool, PAGE_SIZE, 2, KV_GROUPS, HEAD_DIM)),
        'cache_slc_kv': nrm((L, n_pool, PAGE_SIZE, 2, KV_GROUPS, HEAD_DIM)),
        'cache_win_kv': nrm((L, DEC_BATCH, win_buf, 2, KV_GROUPS, HEAD_DIM)),
        'state_ssm_re': nrm((L, DEC_BATCH, SSM_GROUPS, SSM_STATE), 0.1),
        'state_ssm_im': nrm((L, DEC_BATCH, SSM_GROUPS, SSM_STATE), 0.1),
        'state_conv': nrm((L, DEC_BATCH, CONV_W - 1, f2)),
        'page_table': page_table,
        'c_prompt': nrm((BATCH, D_MODEL)),
        'c_sample': nrm((DEC_BATCH, D_MODEL)),
        'w_ada': nrm((L, D_MODEL, 6 * D_MODEL), 0.5 * D_MODEL ** -0.5),
        'b_ada': nrm((L, 6 * D_MODEL), 0.01),
        'g_norm1': 1.0 + nrm((L, D_MODEL), 0.02),
        'w_in': nrm((L, D_MODEL, IN_WIDTH), D_MODEL ** -0.5),
        'ssm_lam_re': -0.5 + nrm((L, SSM_GROUPS, SSM_STATE), 0.01),
        'ssm_lam_im': math.pi * n_idx + nrm((L, SSM_GROUPS, SSM_STATE), 0.01),
        'ssm_log_dt': math.log(DT_MIN) + (math.log(DT_MAX) - math.log(DT_MIN))
                      * jax.random.uniform(next(ks), (L, SSM_GROUPS), f32),
        'ssm_b_re': nrm((L, SSM_GROUPS, SSM_STATE, SSM_CH), (2 * SSM_CH) ** -0.5),
        'ssm_b_im': nrm((L, SSM_GROUPS, SSM_STATE, SSM_CH), (2 * SSM_CH) ** -0.5),
        'ssm_c_re': nrm((L, SSM_GROUPS, SSM_CH, SSM_STATE), (2 * SSM_STATE) ** -0.5),
        'ssm_c_im': nrm((L, SSM_GROUPS, SSM_CH, SSM_STATE), (2 * SSM_STATE) ** -0.5),
        'ssm_d': nrm((L, SSM_WIDTH)),
        'w_cmp': nrm((L, 2, CMP_BLOCK, HEAD_DIM, HEAD_DIM), (CMP_BLOCK * HEAD_DIM) ** -0.5),
        'pe_cmp': nrm((L, 2, CMP_BLOCK, HEAD_DIM), 0.1),
        'g_out_ssm': 1.0 + nrm((L, SSM_WIDTH), 0.02),
        'g_out_att': 1.0 + nrm((L, ATT_WIDTH), 0.02),
        'w_out': nrm((L, D_MODEL, D_MODEL), D_MODEL ** -0.5),
        'g_norm2': 1.0 + nrm((L, D_MODEL), 0.02),
        'w_up': nrm((L, D_MODEL, f2), D_MODEL ** -0.5),
        'conv_w': nrm((L, CONV_W, f2), CONV_W ** -0.5),
        'conv_b': nrm((L, f2), 0.01),
        'w_down': nrm((L, D_FF, D_MODEL), D_FF ** -0.5),
        'g_final': 1.0 + nrm((D_MODEL,), 0.02),
    }


def reference(x_prompt, x_sample, cache_cmp_kv, cache_slc_kv, cache_win_kv, state_ssm_re, state_ssm_im,
              state_conv, page_table, c_prompt, c_sample, w_ada, b_ada, g_norm1, w_in, ssm_lam_re,
              ssm_lam_im, ssm_log_dt, ssm_b_re, ssm_b_im, ssm_c_re, ssm_c_im, ssm_d, w_cmp, pe_cmp,
              g_out_ssm, g_out_att, w_out, g_norm2, w_up, conv_w, conv_b, w_down, g_final):
    xp, xs = x_prompt, x_sample
    B = x_prompt.shape[0]
    new_p, new_s = [], []
    for l in range(DEPTH):
        lw = {'w_ada': w_ada[l], 'b_ada': b_ada[l], 'g_norm1': g_norm1[l], 'w_in': w_in[l],
              'lam_re': ssm_lam_re[l], 'lam_im': ssm_lam_im[l], 'log_dt': ssm_log_dt[l],
              'b_re': ssm_b_re[l], 'b_im': ssm_b_im[l], 'c_re': ssm_c_re[l], 'c_im': ssm_c_im[l],
              'd_skip': ssm_d[l], 'w_cmp': w_cmp[l], 'pe_cmp': pe_cmp[l], 'g_out_ssm': g_out_ssm[l],
              'g_out_att': g_out_att[l], 'w_out': w_out[l], 'g_norm2': g_norm2[l], 'w_up': w_up[l],
              'conv_w': conv_w[l], 'conv_b': conv_b[l], 'w_down': w_down[l]}
        h0 = jnp.zeros((B, SSM_GROUPS, SSM_STATE), jnp.float32)
        conv0 = jnp.zeros((B, CONV_W - 1, 2 * D_FF), xp.dtype)
        xp, sp = run_layer(xp, c_prompt, lw, nsa_prompt, h0, h0, conv0)
        sample_nsa = functools.partial(nsa_sample, cache_cmp_kv=cache_cmp_kv[l], cache_slc_kv=cache_slc_kv[l],
                                       cache_win_kv=cache_win_kv[l], page_table=page_table)
        xs, ss = run_layer(xs, c_sample, lw, sample_nsa, state_ssm_re[l], state_ssm_im[l], state_conv[l])
        new_p.append(sp)
        new_s.append(ss)
    y_prompt = rms_norm(xp, g_final)
    y_sample = rms_norm(xs, g_final)

    def stk(lst, i):
        return jnp.stack([s[i] for s in lst])

    return (y_prompt, y_sample,
            stk(new_p, 0), stk(new_s, 0),
            stk(new_p, 1), stk(new_s, 1),
            stk(new_p, 2), stk(new_s, 2),
            stk(new_p, 3), stk(new_p, 4), stk(new_s, 3), stk(new_s, 4),
            stk(new_p, 5), stk(new_s, 5))
```

```python
import functools
import math

import jax
import jax.numpy as jnp
import numpy as np
from jax import lax
from jax.experimental import pallas as pl
from jax.experimental.pallas import tpu as pltpu

F32 = jnp.float32
BF16 = jnp.bfloat16

D_MODEL = 2048
SSM_WIDTH = D_MODEL // 2
ATT_WIDTH = D_MODEL - SSM_WIDTH
SSM_CH = 16
SSM_GROUPS = SSM_WIDTH // SSM_CH
SSM_STATE = 64
HEAD_DIM = 64
N_HEADS = ATT_WIDTH // HEAD_DIM
KV_GROUPS = 4
HEADS_PER_GROUP = N_HEADS // KV_GROUPS
KV_WIDTH = KV_GROUPS * HEAD_DIM
CMP_STRIDE = 16
CMP_BLOCK = 32
SLC_BLOCK = 64
N_SEL = 16
WINDOW = 512
PAGE_SIZE = 128
ATTN_SCALE = HEAD_DIM ** -0.5
NEG_INF = -1e30
FORCE_BONUS = 1e4
D_FF = 256 * ((8 * D_MODEL // 3 + 255) // 256)
CONV_W = 3
EPS = 1e-6
IN_WIDTH = 3 * SSM_WIDTH + 6 * KV_WIDTH + 3 * N_HEADS

LANES = 128
SUBLANES = 8
VMEM_LIMIT = 56 * 1024 * 1024

IN_TN = 512
IN_NA = 3 * SSM_WIDTH // IN_TN
IN_PAD = (IN_NA + 4) * IN_TN
GATE_PAD = LANES
SSM_LCH = 8 * SSM_STATE
SSM_NCH = SSM_GROUPS // 8
Q_TILE = 128
SEL_KT = 256
WIN_KT = 128
FFN_TF = 512
FFN_NF = D_FF // FFN_TF


def _cparams(sem):
    return pltpu.CompilerParams(dimension_semantics=sem, vmem_limit_bytes=VMEM_LIMIT)


def _rms(x, g):
    return x * lax.rsqrt(jnp.mean(x * x, axis=-1, keepdims=True) + EPS) * g


def _ada_kernel(c_ref, w_ref, b_ref, o_ref):
    c = c_ref[...]
    a = (c * jax.nn.sigmoid(c)).astype(BF16)
    o_ref[...] = jnp.dot(a, w_ref[...].astype(BF16), preferred_element_type=F32) + b_ref[...]


def _ada(c_all, w_ada, b_ada):
    r, d = c_all.shape
    n = w_ada.shape[1]
    tn = 1024
    return pl.pallas_call(
        _ada_kernel,
        grid=(n // tn,),
        in_specs=[pl.BlockSpec((r, d), lambda j: (0, 0)),
                  pl.BlockSpec((d, tn), lambda j: (0, j)),
                  pl.BlockSpec((1, tn), lambda j: (0, j))],
        out_specs=pl.BlockSpec((r, tn), lambda j: (0, j)),
        out_shape=jax.ShapeDtypeStruct((r, n), F32),
        compiler_params=_cparams(("arbitrary",)),
        name="ada",
    )(c_all, w_ada, b_ada.reshape(1, n))


def _inproj_kernel(x_ref, sc_ref, sh_ref, g_ref, w_ref, za_ref, kc_ref, ks_ref, kw_ref, gt_ref, h_ref):
    j = pl.program_id(1)

    @pl.when(j == 0)
    def _():
        h = _rms(x_ref[...], g_ref[...]) * (1.0 + sc_ref[...]) + sh_ref[...]
        h_ref[...] = h.astype(BF16)

    z = jnp.dot(h_ref[...], w_ref[...], preferred_element_type=F32)

    @pl.when(j < IN_NA)
    def _():
        za_ref[...] = z

    @pl.when(j == IN_NA)
    def _():
        kc_ref[...] = z

    @pl.when(j == IN_NA + 1)
    def _():
        ks_ref[...] = z

    @pl.when(j == IN_NA + 2)
    def _():
        kw_ref[...] = z

    @pl.when(j == IN_NA + 3)
    def _():
        gt_ref[...] = z[:, :GATE_PAD]


def _inproj(x, sc, sh, g, w_pad, tm, rows_per_mod):
    n, d = x.shape
    r = sc.shape[1]
    tpm = rows_per_mod // tm
    nj = IN_PAD // IN_TN
    mod_spec = pl.BlockSpec((None, r, d), lambda i, j: (i // tpm, 0, 0))
    kv_spec = pl.BlockSpec((tm, 2 * KV_WIDTH), lambda i, j: (i, 0))
    return pl.pallas_call(
        _inproj_kernel,
        grid=(n // tm, nj),
        in_specs=[pl.BlockSpec((tm, d), lambda i, j: (i, 0)), mod_spec, mod_spec,
                  pl.BlockSpec((1, d), lambda i, j: (0, 0)),
                  pl.BlockSpec((d, IN_TN), lambda i, j: (0, j))],
        out_specs=[pl.BlockSpec((tm, IN_TN), lambda i, j: (i, jnp.minimum(j, IN_NA - 1))),
                   kv_spec, kv_spec, kv_spec,
                   pl.BlockSpec((tm, GATE_PAD), lambda i, j: (i, 0))],
        out_shape=[jax.ShapeDtypeStruct((n, 3 * SSM_WIDTH), F32),
                   jax.ShapeDtypeStruct((n, 2 * KV_WIDTH), F32),
                   jax.ShapeDtypeStruct((n, 2 * KV_WIDTH), F32),
                   jax.ShapeDtypeStruct((n, 2 * KV_WIDTH), F32),
                   jax.ShapeDtypeStruct((n, GATE_PAD), F32)],
        scratch_shapes=[pltpu.VMEM((tm, d), BF16)],
        compiler_params=_cparams(("arbitrary", "arbitrary")),
        name="inproj",
    )(x, sc, sh, g.reshape(1, d), w_pad)


def _s5_prep_kernel(lre_ref, lim_ref, ldt_ref, lrex_ref, limx_ref, bre_ref, bim_ref,
                    pwr_ref, pwi_ref, bbr_ref, bbi_ref):
    dt = jnp.exp(ldt_ref[...])

    def disc(lre, lim):
        mag = jnp.exp(lre * dt)
        ab_re = mag * jnp.cos(lim * dt)
        ab_im = mag * jnp.sin(lim * dt)
        den = lre * lre + lim * lim
        f_re = ((ab_re - 1.0) * lre + ab_im * lim) / den
        f_im = (ab_im * lre - (ab_re - 1.0) * lim) / den
        return ab_re, ab_im, f_re, f_im

    ab_re, ab_im, _, _ = disc(lre_ref[...], lim_ref[...])
    pr, pi = ab_re, ab_im
    pwr_ref[0] = pr
    pwi_ref[0] = pi
    for k in range(1, SUBLANES):
        pr, pi = pr * ab_re - pi * ab_im, pr * ab_im + pi * ab_re
        pwr_ref[k] = pr
        pwi_ref[k] = pi
    _, _, f_re, f_im = disc(lrex_ref[...], limx_ref[...])
    b_re, b_im = bre_ref[...], bim_ref[...]
    bbr_ref[...] = f_re * b_re - f_im * b_im
    bbi_ref[...] = f_re * b_im + f_im * b_re


def _s5_prep(lam_re, lam_im, log_dt, b_re, b_im):
    g, p = lam_re.shape
    ch = b_re.shape[-1]
    lrex = jnp.repeat(lam_re, ch, axis=1)
    limx = jnp.repeat(lam_im, ch, axis=1)
    full = lambda shape: pl.BlockSpec(shape, lambda: (0,) * len(shape))
    return pl.pallas_call(
        _s5_prep_kernel,
        in_specs=[full((g, p)), full((g, p)), full((g, 1)), full((g, p * ch)), full((g, p * ch)),
                  full((g, p * ch)), full((g, p * ch))],
        out_specs=[full((SUBLANES, g, p)), full((SUBLANES, g, p)), full((g, p * ch)), full((g, p * ch))],
        out_shape=[jax.ShapeDtypeStruct((SUBLANES, g, p), F32), jax.ShapeDtypeStruct((SUBLANES, g, p), F32),
                   jax.ShapeDtypeStruct((g, p * ch), F32), jax.ShapeDtypeStruct((g, p * ch), F32)],
        name="s5_prep",
    )(lam_re, lam_im, log_dt.reshape(g, 1), lrex, limx, b_re.reshape(g, p * ch), b_im.reshape(g, p * ch))


def _s5_weights(pwr, pwi, bbr, bbi, c_re, c_im):
    g, p, ch = SSM_GROUPS, SSM_STATE, SSM_CH
    eye = jnp.eye(8, dtype=F32)

    def w_in(bb):
        bb = bb.reshape(SSM_NCH, 8, p, ch)
        return jnp.einsum('jgpc,gh->jgchp', bb, eye).reshape(SSM_NCH, 8 * ch, 8 * p)

    def w_out(c):
        c = c.reshape(SSM_NCH, 8, ch, p)
        return jnp.einsum('jgcp,gh->jgphc', c, eye).reshape(SSM_NCH, 8 * p, 8 * ch)

    w1 = jnp.concatenate([w_in(bbr), w_in(bbi)], axis=-1).astype(BF16)
    w2 = jnp.concatenate([w_out(c_re), w_out(-c_im)], axis=1).astype(BF16)
    pw_r = pwr.reshape(SUBLANES, g * p)
    pw_i = pwi.reshape(SUBLANES, g * p)
    tau = jnp.arange(SUBLANES)[:, None]
    lvl_r = jnp.stack([jnp.where(tau >= d, pw_r[d - 1][None, :], 0.0) for d in (1, 2, 4)])
    lvl_i = jnp.stack([jnp.where(tau >= d, pw_i[d - 1][None, :], 0.0) for d in (1, 2, 4)])
    return w1, w2, lvl_r, lvl_i, pw_r, pw_i


def _s5_post(y, u, g_glu, d_skip, g_out):
    y = y + d_skip * u
    y = jax.nn.gelu(y) * jax.nn.sigmoid(g_glu)
    return _rms(y, g_out).astype(BF16)


def _s5_scan_kernel(u_ref, gg_ref, w1_ref, w2_ref, lr_ref, li_ref, pr_ref, pi_ref, d_ref, go_ref,
                    o_ref, sr_ref, si_ref, br_ref, bi_ref, y_ref, hr_ref, hi_ref, *, tt):
    t = pl.program_id(1)
    nrt = tt // SUBLANES

    @pl.when(t == 0)
    def _():
        hr_ref[...] = jnp.zeros_like(hr_ref)
        hi_ref[...] = jnp.zeros_like(hi_ref)

    def chunk(j, carry):
        lo = pl.multiple_of(j * SSM_LCH, SSM_LCH)
        uo = pl.multiple_of(j * LANES, LANES)
        ub = u_ref[:, pl.ds(uo, LANES)].astype(BF16)
        bu = jnp.dot(ub, w1_ref[j], preferred_element_type=F32)
        xr = bu[:, :SSM_LCH].reshape(nrt, SUBLANES, SSM_LCH)
        xi = bu[:, SSM_LCH:].reshape(nrt, SUBLANES, SSM_LCH)
        for lvl, d in enumerate((1, 2, 4)):
            ar = lr_ref[lvl, :, pl.ds(lo, SSM_LCH)]
            ai = li_ref[lvl, :, pl.ds(lo, SSM_LCH)]
            zr = pltpu.roll(xr, d, 1)
            zi = pltpu.roll(xi, d, 1)
            xr, xi = xr + ar * zr - ai * zi, xi + ar * zi + ai * zr
        br_ref[...] = xr.reshape(tt, SSM_LCH)
        bi_ref[...] = xi.reshape(tt, SSM_LCH)
        pr = pr_ref[:, pl.ds(lo, SSM_LCH)]
        pi = pi_ref[:, pl.ds(lo, SSM_LCH)]

        def tile(k, h):
            hr, hi = h
            r0 = pl.multiple_of(k * SUBLANES, SUBLANES)
            vr = br_ref[pl.ds(r0, SUBLANES), :] + pr * hr - pi * hi
            vi = bi_ref[pl.ds(r0, SUBLANES), :] + pr * hi + pi * hr
            br_ref[pl.ds(r0, SUBLANES), :] = vr
            bi_ref[pl.ds(r0, SUBLANES), :] = vi
            return vr[SUBLANES - 1:, :], vi[SUBLANES - 1:, :]

        hr, hi = lax.fori_loop(0, nrt, tile, (hr_ref[:, pl.ds(lo, SSM_LCH)], hi_ref[:, pl.ds(lo, SSM_LCH)]))
        hr_ref[:, pl.ds(lo, SSM_LCH)] = hr
        hi_ref[:, pl.ds(lo, SSM_LCH)] = hi
        hcat = jnp.concatenate([br_ref[...], bi_ref[...]], axis=1).astype(BF16)
        y_ref[:, pl.ds(uo, LANES)] = jnp.dot(hcat, w2_ref[j], preferred_element_type=F32)
        return carry

    lax.fori_loop(0, SSM_NCH, chunk, 0)
    o_ref[...] = _s5_post(y_ref[...], u_ref[...], gg_ref[...], d_ref[...], go_ref[...])
    sr_ref[...] = hr_ref[...]
    si_ref[...] = hi_ref[...]


def _s5_prompt(za, b, s, w1, w2, lvl_r, lvl_i, pw_r, pw_i, d_skip, g_out, tt):
    nt = s // tt
    gp = SSM_GROUPS * SSM_STATE
    const2 = lambda shape: pl.BlockSpec(shape, lambda bi, ti: (0,) * len(shape))
    st_spec = pl.BlockSpec((None, 1, gp), lambda bi, ti: (bi, 0, 0))
    return pl.pallas_call(
        functools.partial(_s5_scan_kernel, tt=tt),
        grid=(b, nt),
        in_specs=[pl.BlockSpec((tt, SSM_WIDTH), lambda bi, ti: (bi * nt + ti, 0)),
                  pl.BlockSpec((tt, SSM_WIDTH), lambda bi, ti: (bi * nt + ti, 1)),
                  const2(w1.shape), const2(w2.shape), const2(lvl_r.shape), const2(lvl_i.shape),
                  const2(pw_r.shape), const2(pw_i.shape), const2((1, SSM_WIDTH)), const2((1, SSM_WIDTH))],
        out_specs=[pl.BlockSpec((tt, SSM_WIDTH), lambda bi, ti: (bi * nt + ti, 0)), st_spec, st_spec],
        out_shape=[jax.ShapeDtypeStruct((b * s, SSM_WIDTH), BF16),
                   jax.ShapeDtypeStruct((b, 1, gp), F32), jax.ShapeDtypeStruct((b, 1, gp), F32)],
        scratch_shapes=[pltpu.VMEM((tt, SSM_LCH), F32), pltpu.VMEM((tt, SSM_LCH), F32),
                        pltpu.VMEM((tt, SSM_WIDTH), F32),
                        pltpu.VMEM((1, gp), F32), pltpu.VMEM((1, gp), F32)],
        compiler_params=_cparams(("arbitrary", "arbitrary")),
        name="s5_prompt",
    )(za, za, w1, w2, lvl_r, lvl_i, pw_r, pw_i, d_skip.reshape(1, -1), g_out.reshape(1, -1))


def _s5_step_kernel(u_ref, gg_ref, h0r_ref, h0i_ref, w1_ref, w2_ref, pr_ref, pi_ref, d_ref, go_ref,
                    o_ref, sr_ref, si_ref, y_ref):
    for j in range(SSM_NCH):
        lo, uo = j * SSM_LCH, j * LANES
        bu = jnp.dot(u_ref[:, uo:uo + LANES].astype(BF16), w1_ref[j], preferred_element_type=F32)
        ar = pr_ref[0:1, lo:lo + SSM_LCH]
        ai = pi_ref[0:1, lo:lo + SSM_LCH]
        h0r = h0r_ref[:, lo:lo + SSM_LCH]
        h0i = h0i_ref[:, lo:lo + SSM_LCH]
        hr = bu[:, :SSM_LCH] + (ar * h0r - ai * h0i)
        hi = bu[:, SSM_LCH:] + (ar * h0i + ai * h0r)
        sr_ref[:, lo:lo + SSM_LCH] = hr
        si_ref[:, lo:lo + SSM_LCH] = hi
        hcat = jnp.concatenate([hr, hi], axis=1).astype(BF16)
        y_ref[:, uo:uo + LANES] = jnp.dot(hcat, w2_ref[j], preferred_element_type=F32)
    o_ref[...] = _s5_post(y_ref[...], u_ref[...], gg_ref[...], d_ref[...], go_ref[...])


def _s5_sample(za, h0r, h0i, w1, w2, pw_r, pw_i, d_skip, g_out):
    n = za.shape[0]
    gp = SSM_GROUPS * SSM_STATE
    full = lambda shape: pl.BlockSpec(shape, lambda i: (0,) * len(shape))
    return pl.pallas_call(
        _s5_step_kernel,
        grid=(1,),
        in_specs=[pl.BlockSpec((n, SSM_WIDTH), lambda i: (0, 0)), pl.BlockSpec((n, SSM_WIDTH), lambda i: (0, 1)),
                  full((n, gp)), full((n, gp)), full(w1.shape), full(w2.shape),
                  full(pw_r.shape), full(pw_i.shape), full((1, SSM_WIDTH)), full((1, SSM_WIDTH))],
        out_specs=[full((n, SSM_WIDTH)), full((n, gp)), full((n, gp))],
        out_shape=[jax.ShapeDtypeStruct((n, SSM_WIDTH), BF16),
                   jax.ShapeDtypeStruct((n, gp), F32), jax.ShapeDtypeStruct((n, gp), F32)],
        scratch_shapes=[pltpu.VMEM((n, SSM_WIDTH), F32)],
        compiler_params=_cparams(("arbitrary",)),
        name="s5_sample",
    )(za, za, h0r, h0i, w1, w2, pw_r, pw_i, d_skip.reshape(1, -1), g_out.reshape(1, -1))


def _cmp_weights(w_cmp, pe_cmp):
    eye = jnp.eye(KV_GROUPS, dtype=F32)

    def bd(w):
        return jnp.einsum('sjde,gh->sjgdhe', w, eye).reshape(2, CMP_STRIDE, KV_WIDTH, KV_WIDTH)

    wbd = jnp.concatenate([bd(w_cmp[:, :CMP_STRIDE]), bd(w_cmp[:, CMP_STRIDE:])], axis=-1).astype(BF16)
    pe8 = jnp.zeros((2, SUBLANES, CMP_BLOCK * HEAD_DIM), F32).at[:, 0].set(pe_cmp.reshape(2, -1)).astype(BF16)
    wb = jnp.tile(w_cmp.reshape(2, CMP_BLOCK * HEAD_DIM, HEAD_DIM), (1, 1, KV_GROUPS)).astype(BF16)
    return wbd, pe8, wb


def _split3(x):
    a = x.astype(BF16)
    r = x - a.astype(F32)
    b = r.astype(BF16)
    c = (r - b.astype(F32)).astype(BF16)
    return a, b, c


def _dot_exact_left(e, x):
    a, b, c = _split3(x)
    return (jnp.dot(e, a, preferred_element_type=F32) + jnp.dot(e, b, preferred_element_type=F32)
            + jnp.dot(e, c, preferred_element_type=F32))


def _dot_exact_right(x, e):
    a, b, c = _split3(x)
    return (jnp.dot(a, e, preferred_element_type=F32) + jnp.dot(b, e, preferred_element_type=F32)
            + jnp.dot(c, e, preferred_element_type=F32))


def _rank_select(sco_ref, score, blk, n_cand, n_sel):
    def body(i, rank):
        si = sco_ref[pl.ds(i, 1), :]
        beats = jnp.where(si > score, 1, 0) + jnp.where(si == score, jnp.where(blk > i, 1, 0), 0)
        return rank + beats

    rank = lax.fori_loop(0, n_cand, body, jnp.zeros(score.shape, jnp.int32))
    return rank


def _flash_update(carry, s, maskf, v):
    m, l, acc = carry
    s = jnp.where(maskf > 0.5, s, NEG_INF)
    m_new = jnp.maximum(m, jnp.max(s, axis=0, keepdims=True))
    alpha = jnp.exp(m - m_new)
    p = jnp.exp(s - m_new) * maskf
    l = alpha * l + jnp.sum(p, axis=0, keepdims=True)
    acc = alpha * acc + jnp.dot(v, p.astype(BF16), preferred_element_type=F32)
    return m_new, l, acc


def _cmpproj_kernel(x_ref, wbd_ref, pe_ref, wb_ref, kc_ref, vct_ref):
    nch = x_ref.shape[0]
    outs = []
    for s in range(2):
        acc = jnp.zeros((nch, 2 * KV_WIDTH), F32)
        for j in range(CMP_STRIDE):
            c0 = j * 2 * KV_WIDTH + s * KV_WIDTH
            acc = acc + jnp.dot(x_ref[:, c0:c0 + KV_WIDTH].astype(BF16), wbd_ref[s, j],
                                preferred_element_type=F32)
        bias = jnp.dot(pe_ref[s], wb_ref[s], preferred_element_type=F32)[0:1]
        outs.append(acc[:, :KV_WIDTH] + pltpu.roll(acc[:, KV_WIDTH:], nch - 1, 0) + bias)
    kc_ref[...] = outs[0].astype(BF16)
    vct_ref[...] = jnp.transpose(outs[1]).reshape(KV_GROUPS, HEAD_DIM, nch).astype(BF16)


def _cmpproj(kvc, b, s, wbd, pe8, wb):
    nch = s // CMP_STRIDE
    x = kvc.reshape(b, nch, CMP_STRIDE * 2 * KV_WIDTH)
    const = lambda shape: pl.BlockSpec(shape, lambda bi: (0,) * len(shape))
    return pl.pallas_call(
        _cmpproj_kernel,
        grid=(b,),
        in_specs=[pl.BlockSpec((None, nch, CMP_STRIDE * 2 * KV_WIDTH), lambda bi: (bi, 0, 0)),
                  const(wbd.shape), const(pe8.shape), const(wb.shape)],
        out_specs=[pl.BlockSpec((None, nch, KV_WIDTH), lambda bi: (bi, 0, 0)),
                   pl.BlockSpec((None, KV_GROUPS, HEAD_DIM, nch), lambda bi: (bi, 0, 0, 0))],
        out_shape=[jax.ShapeDtypeStruct((b, nch, KV_WIDTH), BF16),
                   jax.ShapeDtypeStruct((b, KV_GROUPS, HEAD_DIM, nch), BF16)],
        compiler_params=_cparams(("arbitrary",)),
        name="cmpproj",
    )(x, wbd, pe8, wb)


def _kvprep_kernel(ks_ref, kw_ref, kso_ref, vsto_ref, kwo_ref, vwto_ref):
    tk = ks_ref.shape[0]
    for src, ko, vto in ((ks_ref, kso_ref, vsto_ref), (kw_ref, kwo_ref, vwto_ref)):
        x = src[...]
        ko[...] = x[:, :KV_WIDTH].astype(BF16)
        vto[...] = jnp.transpose(x[:, KV_WIDTH:]).reshape(KV_GROUPS, HEAD_DIM, tk).astype(BF16)


def _kvprep(kvs, kvw, b, s, tk):
    nt = s // tk
    in_spec = pl.BlockSpec((tk, 2 * KV_WIDTH), lambda bi, ti: (bi * nt + ti, 0))
    k_spec = pl.BlockSpec((None, tk, KV_WIDTH), lambda bi, ti: (bi, ti, 0))
    vt_spec = pl.BlockSpec((None, KV_GROUPS, HEAD_DIM, tk), lambda bi, ti: (bi, 0, 0, ti))
    k_shape = jax.ShapeDtypeStruct((b, s, KV_WIDTH), BF16)
    vt_shape = jax.ShapeDtypeStruct((b, KV_GROUPS, HEAD_DIM, s), BF16)
    return pl.pallas_call(
        _kvprep_kernel,
        grid=(b, nt),
        in_specs=[in_spec, in_spec],
        out_specs=[k_spec, vt_spec, k_spec, vt_spec],
        out_shape=[k_shape, vt_shape, k_shape, vt_shape],
        compiler_params=_cparams(("arbitrary", "arbitrary")),
        name="kvprep",
    )(kvs, kvw)


def _nsa_prompt_kernel(q_ref, gt_ref, ks_ref, vst_ref, kw_ref, vwt_ref, kc_ref, vct_ref, esum_ref, go_ref,
                       o_ref, yt_ref, sel_ref, sco_ref, *, n_blk, n_sel):
    qi = pl.program_id(1)
    q0 = qi * Q_TILE
    nq = HEADS_PER_GROUP * Q_TILE
    ncb = kc_ref.shape[0]
    bpt = SEL_KT // SLC_BLOCK
    tpos = q0 + lax.broadcasted_iota(jnp.int32, (1, Q_TILE), 1)
    tpos4 = jnp.concatenate([tpos] * HEADS_PER_GROUP, axis=1)
    gates = jax.nn.sigmoid(jnp.transpose(gt_ref[...]))
    cend = lax.broadcasted_iota(jnp.int32, (ncb, 1), 0) * CMP_STRIDE + (CMP_BLOCK - 1)
    blk = lax.broadcasted_iota(jnp.int32, (n_blk, 1), 0)
    cur = tpos // SLC_BLOCK
    valid = blk <= cur
    forced = (blk == 0) | (blk == cur) | (blk == cur - 1)
    init = (jnp.full((1, nq), NEG_INF, F32), jnp.zeros((1, nq), F32), jnp.zeros((HEAD_DIM, nq), F32))

    for g in range(KV_GROUPS):
        lane0 = (g // 2) * LANES
        sub = slice((g % 2) * HEAD_DIM, (g % 2 + 1) * HEAD_DIM)
        qt = jnp.transpose(q_ref[:, g * 256:(g + 1) * 256] * ATTN_SCALE)
        qst = jnp.concatenate([qt[r * HEAD_DIM:(r + 1) * HEAD_DIM] for r in range(HEADS_PER_GROUP)],
                              axis=1).astype(BF16)

        sc = jnp.dot(kc_ref[:, lane0:lane0 + LANES][:, sub], qst, preferred_element_type=F32)
        mcf = jnp.where(cend <= tpos4, 1.0, 0.0)
        sc = jnp.where(mcf > 0.5, sc, NEG_INF)
        p = jnp.exp(sc - jnp.max(sc, axis=0, keepdims=True)) * mcf
        p = p / jnp.maximum(jnp.sum(p, axis=0, keepdims=True), 1e-30)
        o_c = jnp.dot(vct_ref[g], p.astype(BF16), preferred_element_type=F32)
        imp = p[:, 0:Q_TILE]
        for r in range(1, HEADS_PER_GROUP):
            imp = imp + p[:, r * Q_TILE:(r + 1) * Q_TILE]
        impb = _dot_exact_left(esum_ref[...], imp)
        score = jnp.where(valid, impb + jnp.where(forced, FORCE_BONUS, 0.0), NEG_INF)
        sco_ref[...] = score
        rank = _rank_select(sco_ref, score, blk, n_blk, n_sel)
        sel_ref[...] = jnp.where(rank < n_sel, 1.0, 0.0)

        def sel_body(kt, carry):
            k0 = pl.multiple_of(kt * SEL_KT, SEL_KT)
            kk = ks_ref[pl.ds(k0, SEL_KT), lane0:lane0 + LANES][:, sub]
            s = jnp.dot(kk, qst, preferred_element_type=F32)
            kpos = k0 + lax.broadcasted_iota(jnp.int32, (SEL_KT, 1), 0)
            rows = sel_ref[pl.ds(kt * bpt, bpt), :]
            sx = jnp.concatenate([jnp.broadcast_to(rows[i:i + 1], (SLC_BLOCK, Q_TILE)) for i in range(bpt)],
                                 axis=0)
            m1 = jnp.where(kpos <= tpos, sx, 0.0)
            mk = jnp.concatenate([m1] * HEADS_PER_GROUP, axis=1)
            return _flash_update(carry, s, mk, vst_ref[g, :, pl.ds(k0, SEL_KT)])

        _, l_s, a_s = lax.fori_loop(0, (q0 + Q_TILE + SEL_KT - 1) // SEL_KT, sel_body, init)
        o_s = a_s / jnp.maximum(l_s, 1e-30)

        def win_body(kt, carry):
            k0 = pl.multiple_of(kt * WIN_KT, WIN_KT)
            kk = kw_ref[pl.ds(k0, WIN_KT), lane0:lane0 + LANES][:, sub]
            s = jnp.dot(kk, qst, preferred_element_type=F32)
            kpos = k0 + lax.broadcasted_iota(jnp.int32, (WIN_KT, 1), 0)
            m1 = jnp.where(kpos <= tpos, jnp.where(kpos > tpos - WINDOW, 1.0, 0.0), 0.0)
            mk = jnp.concatenate([m1] * HEADS_PER_GROUP, axis=1)
            return _flash_update(carry, s, mk, vwt_ref[g, :, pl.ds(k0, WIN_KT)])

        _, l_w, a_w = lax.fori_loop(jnp.maximum(qi - WINDOW // WIN_KT, 0), qi + 1, win_body, init)
        o_w = a_w / jnp.maximum(l_w, 1e-30)

        for r in range(HEADS_PER_GROUP):
            cs = slice(r * Q_TILE, (r + 1) * Q_TILE)
            gi = (g * HEADS_PER_GROUP + r) * 3
            y = (gates[gi:gi + 1] * o_c[:, cs] + gates[gi + 1:gi + 2] * o_s[:, cs]
                 + gates[gi + 2:gi + 3] * o_w[:, cs])
            row0 = (g * HEADS_PER_GROUP + r) * HEAD_DIM
            yt_ref[row0:row0 + HEAD_DIM, :] = y

    o_ref[...] = _rms(jnp.transpose(yt_ref[...]), go_ref[...]).astype(BF16)


def _esum_matrix(n_blk, n_rows, row_of_block0):
    r = SLC_BLOCK // CMP_STRIDE
    j = np.arange(n_blk)[:, None]
    i = np.arange(n_rows)[None, :] - row_of_block0
    e = (i >= r * j - 1) & (i <= r * j + r - 1) & (i >= 0)
    return jnp.asarray(e, BF16)


def _nsa_prompt(za, gates, ks, vst, kw, vwt, kc, vct, g_out, b, s):
    nqt = s // Q_TILE
    n_blk = s // SLC_BLOCK
    n_sel = min(N_SEL, n_blk)
    ncb = s // CMP_STRIDE
    esum = _esum_matrix(n_blk, ncb, 0) * jnp.asarray(np.arange(ncb)[None, :] < ncb - 1, BF16)
    qcol = 2 * SSM_WIDTH // ATT_WIDTH
    per_b3 = lambda shape: pl.BlockSpec((None,) + shape, lambda bi, qi: (bi, 0, 0))
    per_b4 = lambda shape: pl.BlockSpec((None,) + shape, lambda bi, qi: (bi, 0, 0, 0))
    return pl.pallas_call(
        functools.partial(_nsa_prompt_kernel, n_blk=n_blk, n_sel=n_sel),
        grid=(b, nqt),
        in_specs=[pl.BlockSpec((Q_TILE, ATT_WIDTH), lambda bi, qi: (bi * nqt + qi, qcol)),
                  pl.BlockSpec((Q_TILE, GATE_PAD), lambda bi, qi: (bi * nqt + qi, 0)),
                  per_b3((s, KV_WIDTH)), per_b4((KV_GROUPS, HEAD_DIM, s)),
                  per_b3((s, KV_WIDTH)), per_b4((KV_GROUPS, HEAD_DIM, s)),
                  per_b3((ncb, KV_WIDTH)), per_b4((KV_GROUPS, HEAD_DIM, ncb)),
                  pl.BlockSpec((n_blk, ncb), lambda bi, qi: (0, 0)),
                  pl.BlockSpec((1, ATT_WIDTH), lambda bi, qi: (0, 0))],
        out_specs=pl.BlockSpec((Q_TILE, ATT_WIDTH), lambda bi, qi: (bi * nqt + qi, 0)),
        out_shape=jax.ShapeDtypeStruct((b * s, ATT_WIDTH), BF16),
        scratch_shapes=[pltpu.VMEM((ATT_WIDTH, Q_TILE), F32), pltpu.VMEM((n_blk, Q_TILE), F32),
                        pltpu.VMEM((n_blk, Q_TILE), F32)],
        compiler_params=_cparams(("arbitrary", "arbitrary")),
        name="nsa_prompt",
    )(za, gates, ks, vst, kw, vwt, kc, vct, esum, g_out.reshape(1, -1))


def _cmp_pages_kernel(pt_ref, *refs, pg):
    x_refs = refs[:pg]
    wbd_ref, pe_ref, wb_ref, kc_ref, vc_ref, carry_ref = refs[pg:]
    rows = pg * (PAGE_SIZE // CMP_STRIDE)
    h = pl.program_id(1)

    @pl.when(h == 0)
    def _():
        carry_ref[...] = jnp.zeros_like(carry_ref)

    row = lax.broadcasted_iota(jnp.int32, (rows, 1), 0)
    for s, out_ref in ((0, kc_ref), (1, vc_ref)):
        acc = jnp.zeros((rows, 2 * KV_WIDTH), F32)
        for j in range(CMP_STRIDE):
            c0 = j * 2 * KV_WIDTH + s * KV_WIDTH
            xs = jnp.concatenate([x[:, c0:c0 + KV_WIDTH] for x in x_refs], axis=0).astype(BF16)
            acc = acc + jnp.dot(xs, wbd_ref[s, j], preferred_element_type=F32)
        bias = jnp.dot(pe_ref[s], wb_ref[s], preferred_element_type=F32)[0:1]
        lo = acc[:, :KV_WIDTH]
        prev = jnp.where(row == 0, carry_ref[s, SUBLANES - 1:SUBLANES, :], pltpu.roll(lo, 1, 0))
        out_ref[...] = (prev + acc[:, KV_WIDTH:] + bias).astype(BF16)
        carry_ref[s] = lo[rows - SUBLANES:, :]


def _cmp_pages(cache_cmp, page_table, wbd, pe8, wb):
    n_pool = cache_cmp.shape[0]
    db, n_pages = page_table.shape
    cpp = PAGE_SIZE // CMP_STRIDE
    width = CMP_STRIDE * 2 * KV_WIDTH
    pg = min(64, n_pages)
    nh = n_pages // pg
    x = cache_cmp.reshape(n_pool, cpp, width)
    page_specs = [pl.BlockSpec((None, cpp, width), functools.partial(
        lambda bi, hi, pt, k: (pt[bi, hi * pg + k], 0, 0), k=k)) for k in range(pg)]
    const = lambda shape: pl.BlockSpec(shape, lambda bi, hi, pt: (0,) * len(shape))
    out_spec = pl.BlockSpec((None, pg * cpp, KV_WIDTH), lambda bi, hi, pt: (bi, hi, 0))
    out_shape = jax.ShapeDtypeStruct((db, n_pages * cpp, KV_WIDTH), BF16)
    return pl.pallas_call(
        functools.partial(_cmp_pages_kernel, pg=pg),
        grid_spec=pltpu.PrefetchScalarGridSpec(
            num_scalar_prefetch=1,
            grid=(db, nh),
            in_specs=page_specs + [const(wbd.shape), const(pe8.shape), const(wb.shape)],
            out_specs=[out_spec, out_spec],
            scratch_shapes=[pltpu.VMEM((2, SUBLANES, KV_WIDTH), F32)]),
        out_shape=[out_shape, out_shape],
        compiler_params=_cparams(("arbitrary", "arbitrary")),
        name="cmp_pages",
    )(page_table, *([x] * pg), wbd, pe8, wb)


def _query_blockdiag(q_ref):
    qt = jnp.transpose(q_ref[...] * ATTN_SCALE)
    tiled = jnp.concatenate([qt] * KV_GROUPS, axis=0)
    rowg = lax.broadcasted_iota(jnp.int32, (KV_WIDTH, 1), 0) // HEAD_DIM
    colg = lax.broadcasted_iota(jnp.int32, (1, LANES), 1) // HEADS_PER_GROUP
    return jnp.where(rowg == colg, tiled, 0.0).astype(BF16)


def _diag_heads(o):
    rowg = lax.broadcasted_iota(jnp.int32, (N_HEADS, 1), 0) // HEADS_PER_GROUP
    out = jnp.zeros((N_HEADS, HEAD_DIM), F32)
    for g in range(KV_GROUPS):
        out = out + jnp.where(rowg == g, o[:N_HEADS, g * HEAD_DIM:(g + 1) * HEAD_DIM], 0.0)
    return out


def _smp_cmp_kernel(q_ref, kc_ref, vc_ref, esum_ref, gsum_ref, oc_ref, imp_ref, *, qpos):
    qbd = _query_blockdiag(q_ref)
    nr = kc_ref.shape[0]
    s = jnp.dot(kc_ref[...], qbd, preferred_element_type=F32)
    row = lax.broadcasted_iota(jnp.int32, (nr, 1), 0)
    mf = jnp.where(row >= 1, jnp.where((row - 1) * CMP_STRIDE + CMP_BLOCK - 1 <= qpos, 1.0, 0.0), 0.0)
    s = jnp.where(mf > 0.5, s, NEG_INF)
    p = jnp.exp(s - jnp.max(s, axis=0, keepdims=True)) * mf
    p = p / jnp.maximum(jnp.sum(p, axis=0, keepdims=True), 1e-30)
    o = jnp.dot(jnp.transpose(p).astype(BF16), vc_ref[...], preferred_element_type=F32)
    oc_ref[...] = _diag_heads(o)
    impg = _dot_exact_right(p, gsum_ref[...])
    imp_ref[...] = _dot_exact_left(esum_ref[...], impg)


def _smp_cmp(q_pad, kc, vc, n_blk_pad, qpos):
    db, nr, _ = kc.shape
    n_blk = (qpos + SLC_BLOCK) // SLC_BLOCK
    esum = _esum_matrix(n_blk_pad, nr, 1) * jnp.asarray(np.arange(n_blk_pad)[:, None] < n_blk, BF16)
    gs = (np.arange(LANES)[:, None] // HEADS_PER_GROUP == np.arange(LANES)[None, :]) & (
        np.arange(LANES)[:, None] < N_HEADS)
    gsum = jnp.asarray(gs, BF16)
    per_b = lambda shape: pl.BlockSpec((None,) + shape, lambda bi: (bi, 0, 0))
    const = lambda shape: pl.BlockSpec(shape, lambda bi: (0,) * len(shape))
    return pl.pallas_call(
        functools.partial(_smp_cmp_kernel, qpos=qpos),
        grid=(db,),
        in_specs=[per_b((LANES, HEAD_DIM)), per_b((nr, KV_WIDTH)), per_b((nr, KV_WIDTH)),
                  const((n_blk_pad, nr)), const((LANES, LANES))],
        out_specs=[per_b((N_HEADS, HEAD_DIM)), per_b((n_blk_pad, LANES))],
        out_shape=[jax.ShapeDtypeStruct((db, N_HEADS, HEAD_DIM), F32),
                   jax.ShapeDtypeStruct((db, n_blk_pad, LANES), F32)],
        compiler_params=_cparams(("arbitrary",)),
        name="smp_cmp",
    )(q_pad, kc, vc, esum, gsum)


def _smp_topk_kernel(imp_ref, idx_ref, sco_ref, *, n_blk, n_sel):
    nbp = imp_ref.shape[0]
    blk = lax.broadcasted_iota(jnp.int32, (nbp, 1), 0)
    cur = n_blk - 1
    valid = blk <= cur
    forced = (blk == 0) | (blk == cur) | (blk == cur - 1)
    score = jnp.where(valid, imp_ref[...] + jnp.where(forced, FORCE_BONUS, 0.0), NEG_INF)
    sco_ref[...] = score
    rank = _rank_select(sco_ref, score, blk, nbp, n_sel)
    for k in range(n_sel):
        idx_ref[k:k + 1, :] = jnp.sum(jnp.where(rank == k, blk, 0), axis=0, keepdims=True)


def _smp_topk(score_t, n_blk, n_sel):
    nbp, w = score_t.shape
    full = lambda shape: pl.BlockSpec(shape, lambda: (0,) * len(shape))
    return pl.pallas_call(
        functools.partial(_smp_topk_kernel, n_blk=n_blk, n_sel=n_sel),
        in_specs=[full((nbp, w))],
        out_specs=full((n_sel, w)),
        out_shape=jax.ShapeDtypeStruct((n_sel, w), jnp.int32),
        scratch_shapes=[pltpu.VMEM((nbp, w), F32)],
        name="smp_topk",
    )(score_t)


def _smp_attn_kernel(phys_ref, isnew_ref, *refs, n_slots, n_sel, past, wbuf):
    blk_refs = refs[:n_slots]
    (q_ref, ksn_ref, win_ref, kwn_ref, oc_ref, gt_ref, go_ref, o_ref, wout_ref) = refs[n_slots:]
    b = pl.program_id(0)
    qbd = _query_blockdiag(q_ref)
    colg = lax.broadcasted_iota(jnp.int32, (1, LANES), 1) // HEADS_PER_GROUP
    row8 = lax.broadcasted_iota(jnp.int32, (SUBLANES, 1), 0)

    s_parts, m_parts, v_parts = [], [], []
    newcol = jnp.zeros((1, LANES), F32)
    for slot in range(n_slots):
        g = slot // n_sel
        fresh = isnew_ref[b, slot]
        x = blk_refs[slot][...]
        s_parts.append(jnp.dot(x[:, :KV_WIDTH].astype(BF16), qbd, preferred_element_type=F32))
        v_parts.append(x[:, KV_WIDTH:].astype(BF16))
        own = jnp.where(colg == g, 1.0, 0.0)
        keep = jnp.where(fresh == 0, 1.0, 0.0)
        m_parts.append(jnp.broadcast_to(own * keep, (SLC_BLOCK, LANES)))
        newcol = jnp.maximum(newcol, own * (1.0 - keep))
    xn = jnp.broadcast_to(ksn_ref[...], (SUBLANES, 2 * KV_WIDTH))
    s_parts.append(jnp.dot(xn[:, :KV_WIDTH].astype(BF16), qbd, preferred_element_type=F32))
    v_parts.append(xn[:, KV_WIDTH:].astype(BF16))
    m_parts.append(jnp.where(row8 == 0, newcol, 0.0))
    s = jnp.concatenate(s_parts, axis=0)
    mf = jnp.concatenate(m_parts, axis=0)
    v = jnp.concatenate(v_parts, axis=0)
    s = jnp.where(mf > 0.5, s, NEG_INF)
    p = jnp.exp(s - jnp.max(s, axis=0, keepdims=True)) * mf
    p = p / jnp.maximum(jnp.sum(p, axis=0, keepdims=True), 1e-30)
    o_s = _diag_heads(jnp.dot(jnp.transpose(p).astype(BF16), v, preferred_element_type=F32))

    win = win_ref[...]
    xw = jnp.broadcast_to(kwn_ref[...], (SUBLANES, 2 * KV_WIDTH))
    roww = lax.broadcasted_iota(jnp.int32, (wbuf, 1), 0)
    kposw = past - wbuf + roww
    mw = jnp.where((kposw > past - WINDOW) & (kposw >= 0), 1.0, 0.0)
    sw = jnp.concatenate([jnp.dot(win[:, :KV_WIDTH].astype(BF16), qbd, preferred_element_type=F32),
                          jnp.dot(xw[:, :KV_WIDTH].astype(BF16), qbd, preferred_element_type=F32)], axis=0)
    mwf = jnp.concatenate([jnp.broadcast_to(mw, (wbuf, LANES)),
                           jnp.broadcast_to(jnp.where(row8 == 0, 1.0, 0.0), (SUBLANES, LANES))], axis=0)
    vw = jnp.concatenate([win[:, KV_WIDTH:].astype(BF16), xw[:, KV_WIDTH:].astype(BF16)], axis=0)
    sw = jnp.where(mwf > 0.5, sw, NEG_INF)
    pw = jnp.exp(sw - jnp.max(sw, axis=0, keepdims=True)) * mwf
    pw = pw / jnp.maximum(jnp.sum(pw, axis=0, keepdims=True), 1e-30)
    o_w = _diag_heads(jnp.dot(jnp.transpose(pw).astype(BF16), vw, preferred_element_type=F32))
    wout_ref[...] = jnp.where(roww == wbuf - 1, kwn_ref[...], pltpu.roll(win, wbuf - 1, 0))

    gates = jax.nn.sigmoid(gt_ref[...])
    y = gates[:, 0:1] * oc_ref[...] + gates[:, 1:2] * o_s + gates[:, 2:3] * o_w
    ms = jnp.sum(jnp.sum(y * y, axis=1, keepdims=True), axis=0, keepdims=True) / ATT_WIDTH
    o_ref[...] = (y * lax.rsqrt(ms + EPS) * go_ref[...]).astype(BF16)


def _smp_attn(phys, isnew, cache_slc, q_pad, kvs_new, cache_win, kvw_new, o_c, gates, g_out, past):
    db, n_slots = phys.shape
    n_sel = n_slots // KV_GROUPS
    wbuf = cache_win.shape[1]
    width = 2 * KV_WIDTH
    blocks = cache_slc.reshape(-1, SLC_BLOCK, width)
    blk_specs = [pl.BlockSpec((None, SLC_BLOCK, width), functools.partial(
        lambda bi, ph, nw, k: (ph[bi, k], 0, 0), k=k)) for k in range(n_slots)]
    per_b = lambda shape: pl.BlockSpec((None,) + shape, lambda bi, ph, nw: (bi, 0, 0))
    return pl.pallas_call(
        functools.partial(_smp_attn_kernel, n_slots=n_slots, n_sel=n_sel, past=past, wbuf=wbuf),
        grid_spec=pltpu.PrefetchScalarGridSpec(
            num_scalar_prefetch=2,
            grid=(db,),
            in_specs=blk_specs + [per_b((LANES, HEAD_DIM)), per_b((1, width)), per_b((wbuf, width)),
                                  per_b((1, width)), per_b((N_HEADS, HEAD_DIM)), per_b((N_HEADS, 3)),
                                  pl.BlockSpec((N_HEADS, HEAD_DIM), lambda bi, ph, nw: (0, 0))],
            out_specs=[per_b((N_HEADS, HEAD_DIM)), per_b((wbuf, width))]),
        out_shape=[jax.ShapeDtypeStruct((db, N_HEADS, HEAD_DIM), BF16),
                   jax.ShapeDtypeStruct((db, wbuf, width), F32)],
        compiler_params=_cparams(("arbitrary",)),
        name="smp_attn",
    )(phys, isnew, *([blocks] * n_slots), q_pad, kvs_new.reshape(db, 1, width), cache_win.reshape(db, wbuf, width),
      kvw_new.reshape(db, 1, width), o_c, gates, g_out.reshape(N_HEADS, HEAD_DIM))


def _nsa_sample(za, gates_raw, kvs_new, kvw_new, cache_cmp, cache_slc, cache_win, page_table, wbd, pe8, wb,
                g_out):
    db, n_pages = page_table.shape
    past = n_pages * PAGE_SIZE
    n_blk = (past + SLC_BLOCK) // SLC_BLOCK
    n_sel = min(N_SEL, n_blk)
    n_blk_pad = -(-n_blk // SUBLANES) * SUBLANES
    q = za[:, 2 * SSM_WIDTH:].reshape(db, N_HEADS, HEAD_DIM)
    q_pad = jnp.pad(q, ((0, 0), (0, LANES - N_HEADS), (0, 0)))
    kc, vc = _cmp_pages(cache_cmp, page_table, wbd, pe8, wb)
    o_c, imp = _smp_cmp(q_pad, kc, vc, n_blk_pad, past)
    score_t = jnp.transpose(imp[:, :, :KV_GROUPS], (1, 0, 2)).reshape(n_blk_pad, db * KV_GROUPS)
    lane_pad = -(-db * KV_GROUPS // LANES) * LANES
    score_t = jnp.pad(score_t, ((0, 0), (0, lane_pad - db * KV_GROUPS)))
    idx = _smp_topk(score_t, n_blk, n_sel)[:, :db * KV_GROUPS]
    idx = jnp.transpose(idx.reshape(n_sel, db, KV_GROUPS), (1, 2, 0))
    n_past_blk = past // SLC_BLOCK
    per_page = PAGE_SIZE // SLC_BLOCK
    jp = jnp.minimum(idx, n_past_blk - 1)
    page = jnp.take_along_axis(page_table, (jp // per_page).reshape(db, -1), axis=1).reshape(idx.shape)
    phys = (page * per_page + jp % per_page).reshape(db, KV_GROUPS * n_sel).astype(jnp.int32)
    isnew = (idx >= n_past_blk).reshape(db, KV_GROUPS * n_sel).astype(jnp.int32)
    gates = gates_raw[:, :3 * N_HEADS].reshape(db, N_HEADS, 3)
    y, win_new = _smp_attn(phys, isnew, cache_slc, q_pad, kvs_new, cache_win, kvw_new, o_c, gates, g_out, past)
    return y.reshape(db, ATT_WIDTH), win_new


def _outproj_kernel(x_ref, ms_ref, ma_ref, w_ref, gt_ref, sc_ref, sh_ref, g_ref, x1_ref, h2_ref):
    mixed = jnp.concatenate([ms_ref[...], ma_ref[...]], axis=1)
    x1 = x_ref[...] + gt_ref[...] * jnp.dot(mixed, w_ref[...], preferred_element_type=F32)
    x1_ref[...] = x1
    h2_ref[...] = (_rms(x1, g_ref[...]) * (1.0 + sc_ref[...]) + sh_ref[...]).astype(BF16)


def _outproj(x, m_ssm, m_att, w_out, gt1, sc2, sh2, g2, tm, rows_per_mod):
    n, d = x.shape
    r = gt1.shape[1]
    tpm = rows_per_mod // tm
    mod_spec = pl.BlockSpec((None, r, d), lambda i: (i // tpm, 0, 0))
    row = lambda w: pl.BlockSpec((tm, w), lambda i: (i, 0))
    return pl.pallas_call(
        _outproj_kernel,
        grid=(n // tm,),
        in_specs=[row(d), row(SSM_WIDTH), row(ATT_WIDTH), pl.BlockSpec((d, d), lambda i: (0, 0)),
                  mod_spec, mod_spec, mod_spec, pl.BlockSpec((1, d), lambda i: (0, 0))],
        out_specs=[row(d), row(d)],
        out_shape=[jax.ShapeDtypeStruct((n, d), F32), jax.ShapeDtypeStruct((n, d), BF16)],
        compiler_params=_cparams(("arbitrary",)),
        name="outproj",
    )(x, m_ssm, m_att, w_out, gt1, sc2, sh2, g2.reshape(1, d))


def _ffn_act(a_v, a_g, a1_v, a1_g, a2_v, a2_g, cwv_ref, cwg_ref, cbv_ref, cbg_ref):
    val = cbv_ref[...] + cwv_ref[2:3, :] * a_v + cwv_ref[0:1, :] * a2_v + cwv_ref[1:2, :] * a1_v
    gate = cbg_ref[...] + cwg_ref[2:3, :] * a_g + cwg_ref[0:1, :] * a2_g + cwg_ref[1:2, :] * a1_g
    return (gate * jax.nn.sigmoid(gate) * val).astype(BF16)


def _ffn_finish(j, contrib, x1_ref, gt_ref, gf_ref, y_ref):
    @pl.when(j == 0)
    def _():
        y_ref[...] = contrib

    @pl.when(j > 0)
    def _():
        y_ref[...] += contrib

    @pl.when(j == FFN_NF - 1)
    def _():
        y_ref[...] = _rms(x1_ref[...] + gt_ref[...] * y_ref[...], gf_ref[...])


def _ffn_seq_kernel(h_ref, x1_ref, gt_ref, wv_ref, wg_ref, cwv_ref, cwg_ref, cbv_ref, cbg_ref, wd_ref, gf_ref,
                    y_ref, tv_ref, tg_ref, sv_ref, sg_ref, hv_ref, hg_ref, *, tm, tpb):
    i = pl.program_id(0)
    j = pl.program_id(1)

    @pl.when(i % tpb == 0)
    def _():
        hv_ref[j] = jnp.zeros((SUBLANES, FFN_TF), F32)
        hg_ref[j] = jnp.zeros((SUBLANES, FFN_TF), F32)

    h = h_ref[...]
    for w_ref, s_ref, halo_ref, t_ref in ((wv_ref, sv_ref, hv_ref, tv_ref), (wg_ref, sg_ref, hg_ref, tg_ref)):
        s_ref[0:SUBLANES, :] = halo_ref[j]
        s_ref[SUBLANES:SUBLANES + tm, :] = jnp.dot(h, w_ref[...], preferred_element_type=F32)
        halo_ref[j] = s_ref[tm:tm + SUBLANES, :]
        t_ref[...] = s_ref[tm + SUBLANES - 2:tm + SUBLANES, :]
    o = SUBLANES
    act = _ffn_act(sv_ref[o:o + tm, :], sg_ref[o:o + tm, :], sv_ref[o - 1:o - 1 + tm, :], sg_ref[o - 1:o - 1 + tm, :],
                   sv_ref[o - 2:o - 2 + tm, :], sg_ref[o - 2:o - 2 + tm, :], cwv_ref, cwg_ref, cbv_ref, cbg_ref)
    _ffn_finish(j, jnp.dot(act, wd_ref[...], preferred_element_type=F32), x1_ref, gt_ref, gf_ref, y_ref)


def _ffn_step_kernel(h_ref, x1_ref, gt_ref, wv_ref, wg_ref, cwv_ref, cwg_ref, cbv_ref, cbg_ref, wd_ref, gf_ref,
                     p0v_ref, p0g_ref, p1v_ref, p1g_ref, y_ref, av_ref, ag_ref):
    j = pl.program_id(1)
    h = h_ref[...]
    a_v = jnp.dot(h, wv_ref[...], preferred_element_type=F32)
    a_g = jnp.dot(h, wg_ref[...], preferred_element_type=F32)
    av_ref[...] = a_v
    ag_ref[...] = a_g
    act = _ffn_act(a_v, a_g, p1v_ref[...], p1g_ref[...], p0v_ref[...], p0g_ref[...],
                   cwv_ref, cwg_ref, cbv_ref, cbg_ref)
    _ffn_finish(j, jnp.dot(act, wd_ref[...], preferred_element_type=F32), x1_ref, gt_ref, gf_ref, y_ref)


def _ffn_specs(d, tm, r, tpm):
    row = pl.BlockSpec((tm, d), lambda i, j: (i, 0))
    return [row, row, pl.BlockSpec((None, r, d), lambda i, j: (i // tpm, 0, 0)),
            pl.BlockSpec((d, FFN_TF), lambda i, j: (0, j)), pl.BlockSpec((d, FFN_TF), lambda i, j: (0, j + FFN_NF)),
            pl.BlockSpec((CONV_W, FFN_TF), lambda i, j: (0, j)),
            pl.BlockSpec((CONV_W, FFN_TF), lambda i, j: (0, j + FFN_NF)),
            pl.BlockSpec((1, FFN_TF), lambda i, j: (0, j)), pl.BlockSpec((1, FFN_TF), lambda i, j: (0, j + FFN_NF)),
            pl.BlockSpec((FFN_TF, d), lambda i, j: (j, 0)), pl.BlockSpec((1, d), lambda i, j: (0, 0))]


def _ffn_seq(h2, x1, gt2, w_up, conv_w, conv_b, w_down, g_final, tm, rows_per_mod):
    n, d = x1.shape
    tpb = rows_per_mod // tm
    nt = n // tm
    cb = conv_b.reshape(1, -1)
    y, tv, tg = pl.pallas_call(
        functools.partial(_ffn_seq_kernel, tm=tm, tpb=tpb),
        grid=(nt, FFN_NF),
        in_specs=_ffn_specs(d, tm, gt2.shape[1], tpb),
        out_specs=[pl.BlockSpec((tm, d), lambda i, j: (i, 0)),
                   pl.BlockSpec((None, CONV_W - 1, FFN_TF), lambda i, j: (i, 0, j)),
                   pl.BlockSpec((None, CONV_W - 1, FFN_TF), lambda i, j: (i, 0, j))],
        out_shape=[jax.ShapeDtypeStruct((n, d), F32),
                   jax.ShapeDtypeStruct((nt, CONV_W - 1, D_FF), F32),
                   jax.ShapeDtypeStruct((nt, CONV_W - 1, D_FF), F32)],
        scratch_shapes=[pltpu.VMEM((tm + SUBLANES, FFN_TF), F32), pltpu.VMEM((tm + SUBLANES, FFN_TF), F32),
                        pltpu.VMEM((FFN_NF, SUBLANES, FFN_TF), F32), pltpu.VMEM((FFN_NF, SUBLANES, FFN_TF), F32)],
        compiler_params=_cparams(("arbitrary", "arbitrary")),
        name="ffn_seq",
    )(h2, x1, gt2, w_up, w_up, conv_w, conv_w, cb, cb, w_down, g_final.reshape(1, d))
    tails = jnp.concatenate([tv, tg], axis=-1)
    return y, tails[tpb - 1::tpb]


def _ffn_step(h2, x1, gt2, w_up, conv_w, conv_b, w_down, g_final, conv_prev):
    n, d = x1.shape
    cb = conv_b.reshape(1, -1)
    prev_v = pl.BlockSpec((n, FFN_TF), lambda i, j: (0, j))
    prev_g = pl.BlockSpec((n, FFN_TF), lambda i, j: (0, j + FFN_NF))
    p0, p1 = conv_prev[:, 0], conv_prev[:, 1]
    y, a_v, a_g = pl.pallas_call(
        _ffn_step_kernel,
        grid=(1, FFN_NF),
        in_specs=_ffn_specs(d, n, gt2.shape[1], 1) + [prev_v, prev_g, prev_v, prev_g],
        out_specs=[pl.BlockSpec((n, d), lambda i, j: (0, 0)),
                   pl.BlockSpec((n, FFN_TF), lambda i, j: (0, j)), pl.BlockSpec((n, FFN_TF), lambda i, j: (0, j))],
        out_shape=[jax.ShapeDtypeStruct((n, d), F32),
                   jax.ShapeDtypeStruct((n, D_FF), F32), jax.ShapeDtypeStruct((n, D_FF), F32)],
        compiler_params=_cparams(("arbitrary", "arbitrary")),
        name="ffn_step",
    )(h2, x1, gt2, w_up, w_up, conv_w, conv_w, cb, cb, w_down, g_final.reshape(1, d), p0, p0, p1, p1)
    return y, jnp.stack([p1, jnp.concatenate([a_v, a_g], axis=-1)], axis=1)


def kernel(x_prompt, x_sample, cache_cmp_kv, cache_slc_kv, cache_win_kv, state_ssm_re, state_ssm_im, state_conv,
           page_table, c_prompt, c_sample, w_ada, b_ada, g_norm1, w_in, ssm_lam_re, ssm_lam_im, ssm_log_dt,
           ssm_b_re, ssm_b_im, ssm_c_re, ssm_c_im, ssm_d, w_cmp, pe_cmp, g_out_ssm, g_out_att, w_out, g_norm2,
           w_up, conv_w, conv_b, w_down, g_final):
    depth = w_in.shape[0]
    b, s, d = x_prompt.shape
    db, ds, _ = x_sample.shape
    assert depth == 1 and ds == 1 and d == D_MODEL, "kernel is written for one layer and one new token per sequence"
    assert s % 512 == 0
    tm = 512
    gp = SSM_GROUPS * SSM_STATE
    kv_shape = (2, KV_GROUPS, HEAD_DIM)
    l = 0

    w_in_p = jnp.pad(w_in[l], ((0, 0), (0, IN_PAD - IN_WIDTH))).astype(BF16)
    w_out_b = w_out[l].astype(BF16)
    w_up_b = w_up[l].astype(BF16)
    w_down_b = w_down[l].astype(BF16)
    pwr, pwi, bbr, bbi = _s5_prep(ssm_lam_re[l], ssm_lam_im[l], ssm_log_dt[l], ssm_b_re[l], ssm_b_im[l])
    w1, w2, lvl_r, lvl_i, pw_r, pw_i = _s5_weights(pwr, pwi, bbr, bbi, ssm_c_re[l], ssm_c_im[l])
    wbd, pe8, wb = _cmp_weights(w_cmp[l], pe_cmp[l])

    n_c = b + db
    n_c_pad = -(-n_c // SUBLANES) * SUBLANES
    c_all = jnp.pad(jnp.concatenate([c_prompt, c_sample], axis=0), ((0, n_c_pad - n_c), (0, 0)))
    mod = _ada(c_all, w_ada[l], b_ada[l]).reshape(n_c_pad, 6, d)
    mod_p = [mod[:b, k].reshape(b, 1, d) for k in range(6)]
    mod_s = [mod[b:n_c, k].reshape(1, db, d) for k in range(6)]

    xp = x_prompt.reshape(b * s, d)
    za, kvc, kvs, kvw, graw = _inproj(xp, mod_p[1], mod_p[0], g_norm1[l], w_in_p, tm, s)
    m_ssm, st_re, st_im = _s5_prompt(za, b, s, w1, w2, lvl_r, lvl_i, pw_r, pw_i, ssm_d[l], g_out_ssm[l], 256)
    kc, vct = _cmpproj(kvc, b, s, wbd, pe8, wb)
    ks_b, vst, kw_b, vwt = _kvprep(kvs, kvw, b, s, 512)
    m_att = _nsa_prompt(za, graw, ks_b, vst, kw_b, vwt, kc, vct, g_out_att[l], b, s)
    x1, h2 = _outproj(xp, m_ssm, m_att, w_out_b, mod_p[2], mod_p[4], mod_p[3], g_norm2[l], tm, s)
    y_p, conv_p = _ffn_seq(h2, x1, mod_p[5], w_up_b, conv_w[l], conv_b[l], w_down_b, g_final, tm, s)
    wlen = min(WINDOW, s)
    win_p = kvw.reshape(b, s, *kv_shape)[:, s - wlen:]

    xs = x_sample.reshape(db, d)
    za_s, kvc_s, kvs_s, kvw_s, graw_s = _inproj(xs, mod_s[1], mod_s[0], g_norm1[l], w_in_p, db, db)
    m_ssm_s, st_re_s, st_im_s = _s5_sample(za_s, state_ssm_re[l].reshape(db, gp), state_ssm_im[l].reshape(db, gp),
                                           w1, w2, pw_r, pw_i, ssm_d[l], g_out_ssm[l])
    m_att_s, win_s = _nsa_sample(za_s, graw_s, kvs_s, kvw_s, cache_cmp_kv[l], cache_slc_kv[l], cache_win_kv[l],
                                 page_table, wbd, pe8, wb, g_out_att[l])
    x1_s, h2_s = _outproj(xs, m_ssm_s, m_att_s, w_out_b, mod_s[2], mod_s[4], mod_s[3], g_norm2[l], db, db)
    y_s, conv_s = _ffn_step(h2_s, x1_s, mod_s[5], w_up_b, conv_w[l], conv_b[l], w_down_b, g_final, state_conv[l])

    wbuf = cache_win_kv.shape[2]
    return (y_p.reshape(b, s, d), y_s.reshape(db, 1, d),
            kvc.reshape(1, b, s, *kv_shape), kvc_s.reshape(1, db, 1, *kv_shape),
            kvs.reshape(1, b, s, *kv_shape), kvs_s.reshape(1, db, 1, *kv_shape),
            win_p[None], win_s.reshape(1, db, wbuf, *kv_shape),
            st_re.reshape(1, b, SSM_GROUPS, SSM_STATE), st_im.reshape(1, b, SSM_GROUPS, SSM_STATE),
            st_re_s.reshape(1, db, SSM_GROUPS, SSM_STATE), st_im_s.reshape(1, db, SSM_GROUPS, SSM_STATE),
            conv_p[None], conv_s[None])
```

```python
import functools
import math

import jax
import jax.numpy as jnp
import numpy as np
from jax import lax
from jax.experimental import pallas as pl
from jax.experimental.pallas import tpu as pltpu

F32 = jnp.float32
BF16 = jnp.bfloat16

D_MODEL = 2048
SSM_WIDTH = D_MODEL // 2
ATT_WIDTH = D_MODEL - SSM_WIDTH
SSM_CH = 16
SSM_GROUPS = SSM_WIDTH // SSM_CH
SSM_STATE = 64
HEAD_DIM = 64
N_HEADS = ATT_WIDTH // HEAD_DIM
KV_GROUPS = 4
HEADS_PER_GROUP = N_HEADS // KV_GROUPS
KV_WIDTH = KV_GROUPS * HEAD_DIM
CMP_STRIDE = 16
CMP_BLOCK = 32
SLC_BLOCK = 64
N_SEL = 16
WINDOW = 512
PAGE_SIZE = 128
ATTN_SCALE = HEAD_DIM ** -0.5
NEG_INF = -1e30
FORCE_BONUS = 1e4
D_FF = 256 * ((8 * D_MODEL // 3 + 255) // 256)
CONV_W = 3
EPS = 1e-6
IN_WIDTH = 3 * SSM_WIDTH + 6 * KV_WIDTH + 3 * N_HEADS

LANES = 128
SUBLANES = 8
VMEM_LIMIT = 56 * 1024 * 1024

IN_TN = 512
IN_NA = 3 * SSM_WIDTH // IN_TN
IN_PAD = (IN_NA + 4) * IN_TN
GATE_PAD = LANES
SSM_LCH = 8 * SSM_STATE
SSM_NCH = SSM_GROUPS // 8
Q_TILE = 128
SEL_KT = 256
WIN_KT = 128
CMP_PG = 32
FFN_TF = 512
FFN_NF = D_FF // FFN_TF


def _cparams(sem):
    return pltpu.CompilerParams(dimension_semantics=sem, vmem_limit_bytes=VMEM_LIMIT)


def _rms(x, g):
    return x * lax.rsqrt(jnp.mean(x * x, axis=-1, keepdims=True) + EPS) * g


def _ada_kernel(c_ref, w_ref, b_ref, o_ref):
    c = c_ref[...]
    a = (c * jax.nn.sigmoid(c)).astype(BF16)
    o_ref[...] = jnp.dot(a, w_ref[...].astype(BF16), preferred_element_type=F32) + b_ref[...]


def _ada(c_all, w_ada, b_ada):
    r, d = c_all.shape
    n = w_ada.shape[1]
    tn = 1024
    return pl.pallas_call(
        _ada_kernel,
        grid=(n // tn,),
        in_specs=[pl.BlockSpec((r, d), lambda j: (0, 0)),
                  pl.BlockSpec((d, tn), lambda j: (0, j)),
                  pl.BlockSpec((1, tn), lambda j: (0, j))],
        out_specs=pl.BlockSpec((r, tn), lambda j: (0, j)),
        out_shape=jax.ShapeDtypeStruct((r, n), F32),
        compiler_params=_cparams(("arbitrary",)),
        name="ada",
    )(c_all, w_ada, b_ada.reshape(1, n))


def _inproj_kernel(x_ref, sc_ref, sh_ref, g_ref, w_ref, za_ref, kc_ref, ks_ref, kw_ref, gt_ref):
    h = (_rms(x_ref[...], g_ref[...]) * (1.0 + sc_ref[...]) + sh_ref[...]).astype(BF16)
    for j in range(IN_PAD // IN_TN):
        z = jnp.dot(h, w_ref[:, j * IN_TN:(j + 1) * IN_TN], preferred_element_type=F32)
        if j < IN_NA:
            za_ref[:, j * IN_TN:(j + 1) * IN_TN] = z
        elif j < IN_NA + 3:
            (kc_ref, ks_ref, kw_ref)[j - IN_NA][...] = z
        else:
            gt_ref[...] = z[:, :GATE_PAD]


def _inproj(x, sc, sh, g, w_pad, tm, rows_per_mod):
    n, d = x.shape
    r = sc.shape[1]
    tpm = rows_per_mod // tm
    mod_spec = pl.BlockSpec((None, r, d), lambda i: (i // tpm, 0, 0))
    row = lambda w: pl.BlockSpec((tm, w), lambda i: (i, 0))
    return pl.pallas_call(
        _inproj_kernel,
        grid=(n // tm,),
        in_specs=[row(d), mod_spec, mod_spec, pl.BlockSpec((1, d), lambda i: (0, 0)),
                  pl.BlockSpec((d, IN_PAD), lambda i: (0, 0), pipeline_mode=pl.Buffered(1))],
        out_specs=[row(3 * SSM_WIDTH), row(2 * KV_WIDTH), row(2 * KV_WIDTH), row(2 * KV_WIDTH), row(GATE_PAD)],
        out_shape=[jax.ShapeDtypeStruct((n, 3 * SSM_WIDTH), F32),
                   jax.ShapeDtypeStruct((n, 2 * KV_WIDTH), F32),
                   jax.ShapeDtypeStruct((n, 2 * KV_WIDTH), F32),
                   jax.ShapeDtypeStruct((n, 2 * KV_WIDTH), F32),
                   jax.ShapeDtypeStruct((n, GATE_PAD), F32)],
        compiler_params=_cparams(("arbitrary",)),
        name="inproj",
    )(x, sc, sh, g.reshape(1, d), w_pad)


def _s5_prep_kernel(lre_ref, lim_ref, ldt_ref, lrex_ref, limx_ref, bre_ref, bim_ref,
                    pwr_ref, pwi_ref, bbr_ref, bbi_ref):
    dt = jnp.exp(ldt_ref[...])

    def disc(lre, lim):
        mag = jnp.exp(lre * dt)
        ab_re = mag * jnp.cos(lim * dt)
        ab_im = mag * jnp.sin(lim * dt)
        den = lre * lre + lim * lim
        f_re = ((ab_re - 1.0) * lre + ab_im * lim) / den
        f_im = (ab_im * lre - (ab_re - 1.0) * lim) / den
        return ab_re, ab_im, f_re, f_im

    ab_re, ab_im, _, _ = disc(lre_ref[...], lim_ref[...])
    pr, pi = ab_re, ab_im
    pwr_ref[0] = pr
    pwi_ref[0] = pi
    for k in range(1, SUBLANES):
        pr, pi = pr * ab_re - pi * ab_im, pr * ab_im + pi * ab_re
        pwr_ref[k] = pr
        pwi_ref[k] = pi
    _, _, f_re, f_im = disc(lrex_ref[...], limx_ref[...])
    b_re, b_im = bre_ref[...], bim_ref[...]
    bbr_ref[...] = f_re * b_re - f_im * b_im
    bbi_ref[...] = f_re * b_im + f_im * b_re


def _s5_prep(lam_re, lam_im, log_dt, b_re, b_im):
    g, p = lam_re.shape
    ch = b_re.shape[-1]
    lrex = jnp.repeat(lam_re, ch, axis=1)
    limx = jnp.repeat(lam_im, ch, axis=1)
    full = lambda shape: pl.BlockSpec(shape, lambda: (0,) * len(shape))
    return pl.pallas_call(
        _s5_prep_kernel,
        in_specs=[full((g, p)), full((g, p)), full((g, 1)), full((g, p * ch)), full((g, p * ch)),
                  full((g, p * ch)), full((g, p * ch))],
        out_specs=[full((SUBLANES, g, p)), full((SUBLANES, g, p)), full((g, p * ch)), full((g, p * ch))],
        out_shape=[jax.ShapeDtypeStruct((SUBLANES, g, p), F32), jax.ShapeDtypeStruct((SUBLANES, g, p), F32),
                   jax.ShapeDtypeStruct((g, p * ch), F32), jax.ShapeDtypeStruct((g, p * ch), F32)],
        name="s5_prep",
    )(lam_re, lam_im, log_dt.reshape(g, 1), lrex, limx, b_re.reshape(g, p * ch), b_im.reshape(g, p * ch))


def _s5_weights(pwr, pwi, bbr, bbi, c_re, c_im):
    g, p, ch = SSM_GROUPS, SSM_STATE, SSM_CH
    eye = jnp.eye(8, dtype=F32)

    def w_in(bb):
        bb = bb.reshape(SSM_NCH, 8, p, ch)
        return jnp.einsum('jgpc,gh->jgchp', bb, eye).reshape(SSM_NCH, 8 * ch, 8 * p)

    def w_out(c):
        c = c.reshape(SSM_NCH, 8, ch, p)
        return jnp.einsum('jgcp,gh->jgphc', c, eye).reshape(SSM_NCH, 8 * p, 8 * ch)

    w1 = jnp.concatenate([w_in(bbr), w_in(bbi)], axis=-1).astype(BF16)
    w2 = jnp.concatenate([w_out(c_re), w_out(-c_im)], axis=1).astype(BF16)
    pw_r = pwr.reshape(SUBLANES, g * p)
    pw_i = pwi.reshape(SUBLANES, g * p)
    tau = jnp.arange(SUBLANES)[:, None]
    lvl_r = jnp.stack([jnp.where(tau >= d, pw_r[d - 1][None, :], 0.0) for d in (1, 2, 4)])
    lvl_i = jnp.stack([jnp.where(tau >= d, pw_i[d - 1][None, :], 0.0) for d in (1, 2, 4)])
    return w1, w2, lvl_r, lvl_i, pw_r, pw_i


def _s5_post(y, u, g_glu, d_skip, g_out):
    y = y + d_skip * u
    y = jax.nn.gelu(y) * jax.nn.sigmoid(g_glu)
    return _rms(y, g_out).astype(BF16)


def _s5_scan_kernel(u_ref, gg_ref, w1_ref, w2_ref, lr_ref, li_ref, pr_ref, pi_ref, d_ref, go_ref,
                    o_ref, sr_ref, si_ref, br_ref, bi_ref, y_ref, hr_ref, hi_ref, *, tt):
    t = pl.program_id(1)
    nrt = tt // SUBLANES

    @pl.when(t == 0)
    def _():
        hr_ref[...] = jnp.zeros_like(hr_ref)
        hi_ref[...] = jnp.zeros_like(hi_ref)

    def chunk(j, carry):
        lo = pl.multiple_of(j * SSM_LCH, SSM_LCH)
        uo = pl.multiple_of(j * LANES, LANES)
        ub = u_ref[:, pl.ds(uo, LANES)].astype(BF16)
        bu = jnp.dot(ub, w1_ref[j], preferred_element_type=F32)
        xr = bu[:, :SSM_LCH].reshape(nrt, SUBLANES, SSM_LCH)
        xi = bu[:, SSM_LCH:].reshape(nrt, SUBLANES, SSM_LCH)
        for lvl, d in enumerate((1, 2, 4)):
            ar = lr_ref[lvl, :, pl.ds(lo, SSM_LCH)]
            ai = li_ref[lvl, :, pl.ds(lo, SSM_LCH)]
            zr = pltpu.roll(xr, d, 1)
            zi = pltpu.roll(xi, d, 1)
            xr, xi = xr + ar * zr - ai * zi, xi + ar * zi + ai * zr
        br_ref[...] = xr.reshape(tt, SSM_LCH)
        bi_ref[...] = xi.reshape(tt, SSM_LCH)
        pr = pr_ref[:, pl.ds(lo, SSM_LCH)]
        pi = pi_ref[:, pl.ds(lo, SSM_LCH)]

        def tile(k, h):
            hr, hi = h
            r0 = pl.multiple_of(k * SUBLANES, SUBLANES)
            vr = br_ref[pl.ds(r0, SUBLANES), :] + pr * hr - pi * hi
            vi = bi_ref[pl.ds(r0, SUBLANES), :] + pr * hi + pi * hr
            br_ref[pl.ds(r0, SUBLANES), :] = vr
            bi_ref[pl.ds(r0, SUBLANES), :] = vi
            return vr[SUBLANES - 1:, :], vi[SUBLANES - 1:, :]

        hr, hi = lax.fori_loop(0, nrt, tile, (hr_ref[:, pl.ds(lo, SSM_LCH)], hi_ref[:, pl.ds(lo, SSM_LCH)]))
        hr_ref[:, pl.ds(lo, SSM_LCH)] = hr
        hi_ref[:, pl.ds(lo, SSM_LCH)] = hi
        hcat = jnp.concatenate([br_ref[...], bi_ref[...]], axis=1).astype(BF16)
        y_ref[:, pl.ds(uo, LANES)] = jnp.dot(hcat, w2_ref[j], preferred_element_type=F32)
        return carry

    lax.fori_loop(0, SSM_NCH, chunk, 0)
    o_ref[...] = _s5_post(y_ref[...], u_ref[...], gg_ref[...], d_ref[...], go_ref[...])
    sr_ref[...] = hr_ref[...]
    si_ref[...] = hi_ref[...]


def _s5_prompt(za, b, s, w1, w2, lvl_r, lvl_i, pw_r, pw_i, d_skip, g_out, tt):
    nt = s // tt
    gp = SSM_GROUPS * SSM_STATE
    const2 = lambda shape: pl.BlockSpec(shape, lambda bi, ti: (0,) * len(shape))
    st_spec = pl.BlockSpec((None, 1, gp), lambda bi, ti: (bi, 0, 0))
    return pl.pallas_call(
        functools.partial(_s5_scan_kernel, tt=tt),
        grid=(b, nt),
        in_specs=[pl.BlockSpec((tt, SSM_WIDTH), lambda bi, ti: (bi * nt + ti, 0)),
                  pl.BlockSpec((tt, SSM_WIDTH), lambda bi, ti: (bi * nt + ti, 1)),
                  const2(w1.shape), const2(w2.shape), const2(lvl_r.shape), const2(lvl_i.shape),
                  const2(pw_r.shape), const2(pw_i.shape), const2((1, SSM_WIDTH)), const2((1, SSM_WIDTH))],
        out_specs=[pl.BlockSpec((tt, SSM_WIDTH), lambda bi, ti: (bi * nt + ti, 0)), st_spec, st_spec],
        out_shape=[jax.ShapeDtypeStruct((b * s, SSM_WIDTH), BF16),
                   jax.ShapeDtypeStruct((b, 1, gp), F32), jax.ShapeDtypeStruct((b, 1, gp), F32)],
        scratch_shapes=[pltpu.VMEM((tt, SSM_LCH), F32), pltpu.VMEM((tt, SSM_LCH), F32),
                        pltpu.VMEM((tt, SSM_WIDTH), F32),
                        pltpu.VMEM((1, gp), F32), pltpu.VMEM((1, gp), F32)],
        compiler_params=_cparams(("arbitrary", "arbitrary")),
        name="s5_prompt",
    )(za, za, w1, w2, lvl_r, lvl_i, pw_r, pw_i, d_skip.reshape(1, -1), g_out.reshape(1, -1))


def _s5_step_kernel(u_ref, gg_ref, h0r_ref, h0i_ref, w1_ref, w2_ref, pr_ref, pi_ref, d_ref, go_ref,
                    o_ref, sr_ref, si_ref, y_ref):
    for j in range(SSM_NCH):
        lo, uo = j * SSM_LCH, j * LANES
        bu = jnp.dot(u_ref[:, uo:uo + LANES].astype(BF16), w1_ref[j], preferred_element_type=F32)
        ar = pr_ref[0:1, lo:lo + SSM_LCH]
        ai = pi_ref[0:1, lo:lo + SSM_LCH]
        h0r = h0r_ref[:, lo:lo + SSM_LCH]
        h0i = h0i_ref[:, lo:lo + SSM_LCH]
        hr = bu[:, :SSM_LCH] + (ar * h0r - ai * h0i)
        hi = bu[:, SSM_LCH:] + (ar * h0i + ai * h0r)
        sr_ref[:, lo:lo + SSM_LCH] = hr
        si_ref[:, lo:lo + SSM_LCH] = hi
        hcat = jnp.concatenate([hr, hi], axis=1).astype(BF16)
        y_ref[:, uo:uo + LANES] = jnp.dot(hcat, w2_ref[j], preferred_element_type=F32)
    o_ref[...] = _s5_post(y_ref[...], u_ref[...], gg_ref[...], d_ref[...], go_ref[...])


def _s5_sample(za, h0r, h0i, w1, w2, pw_r, pw_i, d_skip, g_out):
    n = za.shape[0]
    gp = SSM_GROUPS * SSM_STATE
    full = lambda shape: pl.BlockSpec(shape, lambda i: (0,) * len(shape))
    return pl.pallas_call(
        _s5_step_kernel,
        grid=(1,),
        in_specs=[pl.BlockSpec((n, SSM_WIDTH), lambda i: (0, 0)), pl.BlockSpec((n, SSM_WIDTH), lambda i: (0, 1)),
                  full((n, gp)), full((n, gp)), full(w1.shape), full(w2.shape),
                  full(pw_r.shape), full(pw_i.shape), full((1, SSM_WIDTH)), full((1, SSM_WIDTH))],
        out_specs=[full((n, SSM_WIDTH)), full((n, gp)), full((n, gp))],
        out_shape=[jax.ShapeDtypeStruct((n, SSM_WIDTH), BF16),
                   jax.ShapeDtypeStruct((n, gp), F32), jax.ShapeDtypeStruct((n, gp), F32)],
        scratch_shapes=[pltpu.VMEM((n, SSM_WIDTH), F32)],
        compiler_params=_cparams(("arbitrary",)),
        name="s5_sample",
    )(za, za, h0r, h0i, w1, w2, pw_r, pw_i, d_skip.reshape(1, -1), g_out.reshape(1, -1))


def _cmp_weights(w_cmp, pe_cmp):
    eye = jnp.eye(KV_GROUPS, dtype=F32)

    def bd(w):
        return jnp.einsum('sjde,gh->sjgdhe', w, eye).reshape(2, CMP_STRIDE, KV_WIDTH, KV_WIDTH)

    wbd = jnp.concatenate([bd(w_cmp[:, :CMP_STRIDE]), bd(w_cmp[:, CMP_STRIDE:])], axis=-1).astype(BF16)
    pe8 = jnp.zeros((2, SUBLANES, CMP_BLOCK * HEAD_DIM), F32).at[:, 0].set(pe_cmp.reshape(2, -1)).astype(BF16)
    wb = jnp.tile(w_cmp.reshape(2, CMP_BLOCK * HEAD_DIM, HEAD_DIM), (1, 1, KV_GROUPS)).astype(BF16)
    return wbd, pe8, wb


def _split3(x):
    a = x.astype(BF16)
    r = x - a.astype(F32)
    b = r.astype(BF16)
    c = (r - b.astype(F32)).astype(BF16)
    return a, b, c


def _dot_exact_left(e, x):
    a, b, c = _split3(x)
    return (jnp.dot(e, a, preferred_element_type=F32) + jnp.dot(e, b, preferred_element_type=F32)
            + jnp.dot(e, c, preferred_element_type=F32))


def _dot_exact_right(x, e):
    a, b, c = _split3(x)
    return (jnp.dot(a, e, preferred_element_type=F32) + jnp.dot(b, e, preferred_element_type=F32)
            + jnp.dot(c, e, preferred_element_type=F32))


def _rank(score, n_cand):
    n_tiles = score.shape[0] // SUBLANES
    tiles = [score[v * SUBLANES:(v + 1) * SUBLANES] for v in range(n_tiles)]
    ranks = [jnp.zeros(t.shape, F32) for t in tiles]
    row = lax.broadcasted_iota(jnp.int32, (SUBLANES, 1), 0)
    for i in range(n_cand):
        si = tiles[i // SUBLANES][i % SUBLANES:i % SUBLANES + 1]
        for v in range(n_tiles):
            ge = jnp.where(si >= tiles[v], 1.0, 0.0)
            gt = jnp.where(si > tiles[v], 1.0, 0.0)
            if i < v * SUBLANES:
                beats = ge
            elif i >= (v + 1) * SUBLANES:
                beats = gt
            else:
                beats = jnp.where(row > i % SUBLANES, ge, gt)
            ranks[v] = ranks[v] + beats
    return jnp.concatenate(ranks, axis=0)


def _col_reduce(x, op, final):
    slabs = [x[i * SUBLANES:(i + 1) * SUBLANES] for i in range(x.shape[0] // SUBLANES)]
    while len(slabs) > 1:
        pairs = [op(slabs[i], slabs[i + 1]) for i in range(0, len(slabs) - 1, 2)]
        slabs = pairs + ([slabs[-1]] if len(slabs) % 2 else [])
    return final(slabs[0], axis=0, keepdims=True)


def _flash_tiles(items, qst_ref, m_ref, l_ref, acc_ref):
    scores = [jnp.dot(kk, qst_ref[g], preferred_element_type=F32) for g, kk, _, _ in items]
    for (g, _, vt, bias), s in zip(items, scores):
        if bias is not None:
            s = s + jnp.concatenate([bias] * HEADS_PER_GROUP, axis=1)
        m_old = m_ref[g]
        m_new = jnp.maximum(m_old, _col_reduce(s, jnp.maximum, jnp.max))
        alpha = jnp.exp(m_old - m_new)
        p = jnp.exp(s - m_new)
        l_ref[g] = alpha * l_ref[g] + _col_reduce(p, jnp.add, jnp.sum)
        acc_ref[g] = alpha * acc_ref[g] + jnp.dot(vt, p.astype(BF16), preferred_element_type=F32)
        m_ref[g] = m_new


def _cmpproj_kernel(x_ref, wbd_ref, pe_ref, wb_ref, kc_ref, vct_ref):
    nch = x_ref.shape[0]
    outs = []
    for s in range(2):
        acc = jnp.zeros((nch, 2 * KV_WIDTH), F32)
        for j in range(CMP_STRIDE):
            c0 = j * 2 * KV_WIDTH + s * KV_WIDTH
            acc = acc + jnp.dot(x_ref[:, c0:c0 + KV_WIDTH].astype(BF16), wbd_ref[s, j],
                                preferred_element_type=F32)
        bias = jnp.dot(pe_ref[s], wb_ref[s], preferred_element_type=F32)[0:1]
        outs.append(acc[:, :KV_WIDTH] + pltpu.roll(acc[:, KV_WIDTH:], nch - 1, 0) + bias)
    kc_ref[...] = outs[0].astype(BF16)
    vct_ref[...] = jnp.transpose(outs[1]).reshape(KV_GROUPS, HEAD_DIM, nch).astype(BF16)


def _cmpproj(kvc, b, s, wbd, pe8, wb):
    nch = s // CMP_STRIDE
    x = kvc.reshape(b, nch, CMP_STRIDE * 2 * KV_WIDTH)
    const = lambda shape: pl.BlockSpec(shape, lambda bi: (0,) * len(shape))
    return pl.pallas_call(
        _cmpproj_kernel,
        grid=(b,),
        in_specs=[pl.BlockSpec((None, nch, CMP_STRIDE * 2 * KV_WIDTH), lambda bi: (bi, 0, 0)),
                  const(wbd.shape), const(pe8.shape), const(wb.shape)],
        out_specs=[pl.BlockSpec((None, nch, KV_WIDTH), lambda bi: (bi, 0, 0)),
                   pl.BlockSpec((None, KV_GROUPS, HEAD_DIM, nch), lambda bi: (bi, 0, 0, 0))],
        out_shape=[jax.ShapeDtypeStruct((b, nch, KV_WIDTH), BF16),
                   jax.ShapeDtypeStruct((b, KV_GROUPS, HEAD_DIM, nch), BF16)],
        compiler_params=_cparams(("arbitrary",)),
        name="cmpproj",
    )(x, wbd, pe8, wb)


def _kvprep_kernel(ks_ref, kw_ref, kso_ref, vsto_ref, kwo_ref, vwto_ref):
    tk = ks_ref.shape[0]
    for src, ko, vto in ((ks_ref, kso_ref, vsto_ref), (kw_ref, kwo_ref, vwto_ref)):
        x = src[...]
        ko[...] = x[:, :KV_WIDTH].astype(BF16)
        vto[...] = jnp.transpose(x[:, KV_WIDTH:]).reshape(KV_GROUPS, HEAD_DIM, tk).astype(BF16)


def _kvprep(kvs, kvw, b, s, tk):
    nt = s // tk
    in_spec = pl.BlockSpec((tk, 2 * KV_WIDTH), lambda bi, ti: (bi * nt + ti, 0))
    k_spec = pl.BlockSpec((None, tk, KV_WIDTH), lambda bi, ti: (bi, ti, 0))
    vt_spec = pl.BlockSpec((None, KV_GROUPS, HEAD_DIM, tk), lambda bi, ti: (bi, 0, 0, ti))
    k_shape = jax.ShapeDtypeStruct((b, s, KV_WIDTH), BF16)
    vt_shape = jax.ShapeDtypeStruct((b, KV_GROUPS, HEAD_DIM, s), BF16)
    return pl.pallas_call(
        _kvprep_kernel,
        grid=(b, nt),
        in_specs=[in_spec, in_spec],
        out_specs=[k_spec, vt_spec, k_spec, vt_spec],
        out_shape=[k_shape, vt_shape, k_shape, vt_shape],
        compiler_params=_cparams(("arbitrary", "arbitrary")),
        name="kvprep",
    )(kvs, kvw)


def _nsa_prompt_kernel(q_ref, gt_ref, ks_ref, vst_ref, kw_ref, vwt_ref, kc_ref, vct_ref, esum_ref, go_ref,
                       o_ref, yt_ref, qst_ref, selb_ref, oc_ref, os_ref, m_ref, l_ref, acc_ref, *, n_blk, n_sel):
    qi = pl.program_id(1)
    q0 = qi * Q_TILE
    ncb = kc_ref.shape[0]
    bpt = SEL_KT // SLC_BLOCK
    tcol = lax.broadcasted_iota(jnp.int32, (1, Q_TILE), 1)
    tpos = q0 + tcol
    tpos4 = jnp.concatenate([tpos] * HEADS_PER_GROUP, axis=1)
    cend = lax.broadcasted_iota(jnp.int32, (ncb, 1), 0) * CMP_STRIDE + (CMP_BLOCK - 1)
    blk = lax.broadcasted_iota(jnp.int32, (n_blk, 1), 0)
    cur = tpos // SLC_BLOCK
    valid = blk <= cur
    forced = (blk == 0) | (blk == cur) | (blk == cur - 1)
    flash_refs = (qst_ref, m_ref, l_ref, acc_ref)

    def keys(ref, k0, n, g):
        lane0 = (g // 2) * LANES
        return ref[pl.ds(k0, n), lane0:lane0 + LANES][:, (g % 2) * HEAD_DIM:(g % 2 + 1) * HEAD_DIM]

    def reset():
        m_ref[...] = jnp.full(m_ref.shape, NEG_INF, F32)
        l_ref[...] = jnp.zeros(l_ref.shape, F32)
        acc_ref[...] = jnp.zeros(acc_ref.shape, F32)

    for g in range(KV_GROUPS):
        qt = jnp.transpose(q_ref[:, g * 256:(g + 1) * 256] * ATTN_SCALE)
        qst = jnp.concatenate([qt[r * HEAD_DIM:(r + 1) * HEAD_DIM] for r in range(HEADS_PER_GROUP)],
                              axis=1).astype(BF16)
        qst_ref[g] = qst
        sc = jnp.dot(keys(kc_ref, 0, ncb, g), qst, preferred_element_type=F32)
        mcf = jnp.where(cend <= tpos4, 1.0, 0.0)
        sc = jnp.where(mcf > 0.5, sc, NEG_INF)
        p = jnp.exp(sc - jnp.max(sc, axis=0, keepdims=True)) * mcf
        p = p / jnp.maximum(jnp.sum(p, axis=0, keepdims=True), 1e-30)
        oc_ref[g] = jnp.dot(vct_ref[g], p.astype(BF16), preferred_element_type=F32)
        imp = p[:, 0:Q_TILE]
        for r in range(1, HEADS_PER_GROUP):
            imp = imp + p[:, r * Q_TILE:(r + 1) * Q_TILE]
        impb = _dot_exact_left(esum_ref[...], imp)
        score = jnp.where(valid, impb + jnp.where(forced, FORCE_BONUS, 0.0), NEG_INF)
        selb_ref[g] = jnp.where(_rank(score, n_blk) < n_sel, 0.0, NEG_INF)

    def sel_bias(g, kt):
        rows = selb_ref[g, pl.ds(kt * bpt, bpt), :]
        return jnp.concatenate([jnp.broadcast_to(rows[i:i + 1], (SLC_BLOCK, Q_TILE)) for i in range(bpt)], axis=0)

    def sel_tiles(kt, extra):
        k0 = pl.multiple_of(kt * SEL_KT, SEL_KT)
        items = []
        for g in range(KV_GROUPS):
            bias = sel_bias(g, kt)
            items.append((g, keys(ks_ref, k0, SEL_KT, g), vst_ref[g, :, pl.ds(k0, SEL_KT)],
                          bias if extra is None else bias + extra))
        _flash_tiles(items, *flash_refs)

    def sel_body(kt, carry):
        sel_tiles(kt, None)
        return carry

    reset()
    last = (q0 + Q_TILE + SEL_KT - 1) // SEL_KT - 1
    lax.fori_loop(0, last, sel_body, 0)
    kpos = last * SEL_KT + lax.broadcasted_iota(jnp.int32, (SEL_KT, 1), 0)
    sel_tiles(last, jnp.where(kpos <= tpos, 0.0, NEG_INF))
    for g in range(KV_GROUPS):
        os_ref[g] = acc_ref[g] / jnp.maximum(l_ref[g], 1e-30)

    reset()
    krow = lax.broadcasted_iota(jnp.int32, (WIN_KT, 1), 0)
    n_back = WINDOW // WIN_KT
    for t in range(n_back + 1):
        kt = qi - t
        if t == 0:
            bias = jnp.where(krow <= tcol, 0.0, NEG_INF)
        elif t == n_back:
            bias = jnp.where(krow > tcol, 0.0, NEG_INF)
        else:
            bias = None

        def win_tiles(kt=kt, bias=bias):
            k0 = pl.multiple_of(kt * WIN_KT, WIN_KT)
            _flash_tiles([(g, keys(kw_ref, k0, WIN_KT, g), vwt_ref[g, :, pl.ds(k0, WIN_KT)], bias)
                          for g in range(KV_GROUPS)], *flash_refs)

        if t == 0:
            win_tiles()
        else:
            pl.when(kt >= 0)(win_tiles)

    gates = jax.nn.sigmoid(jnp.transpose(gt_ref[...]))
    for g in range(KV_GROUPS):
        o_c, o_s = oc_ref[g], os_ref[g]
        o_w = acc_ref[g] / jnp.maximum(l_ref[g], 1e-30)
        for r in range(HEADS_PER_GROUP):
            cs = slice(r * Q_TILE, (r + 1) * Q_TILE)
            gi = (g * HEADS_PER_GROUP + r) * 3
            y = (gates[gi:gi + 1] * o_c[:, cs] + gates[gi + 1:gi + 2] * o_s[:, cs]
                 + gates[gi + 2:gi + 3] * o_w[:, cs])
            row0 = (g * HEADS_PER_GROUP + r) * HEAD_DIM
            yt_ref[row0:row0 + HEAD_DIM, :] = y

    o_ref[...] = _rms(jnp.transpose(yt_ref[...]), go_ref[...]).astype(BF16)


def _esum_matrix(n_blk, n_rows, row_of_block0):
    r = SLC_BLOCK // CMP_STRIDE
    j = np.arange(n_blk)[:, None]
    i = np.arange(n_rows)[None, :] - row_of_block0
    e = (i >= r * j - 1) & (i <= r * j + r - 1) & (i >= 0)
    return jnp.asarray(e, BF16)


def _nsa_prompt(za, gates, ks, vst, kw, vwt, kc, vct, g_out, b, s):
    assert WINDOW % WIN_KT == 0 and WIN_KT == Q_TILE and s % SEL_KT == 0
    nqt = s // Q_TILE
    nq = HEADS_PER_GROUP * Q_TILE
    n_blk = s // SLC_BLOCK
    n_sel = min(N_SEL, n_blk)
    ncb = s // CMP_STRIDE
    esum = _esum_matrix(n_blk, ncb, 0) * jnp.asarray(np.arange(ncb)[None, :] < ncb - 1, BF16)
    qcol = 2 * SSM_WIDTH // ATT_WIDTH
    per_b3 = lambda shape: pl.BlockSpec((None,) + shape, lambda bi, qi: (bi, 0, 0))
    per_b4 = lambda shape: pl.BlockSpec((None,) + shape, lambda bi, qi: (bi, 0, 0, 0))
    return pl.pallas_call(
        functools.partial(_nsa_prompt_kernel, n_blk=n_blk, n_sel=n_sel),
        grid=(b, nqt),
        in_specs=[pl.BlockSpec((Q_TILE, ATT_WIDTH), lambda bi, qi: (bi * nqt + qi, qcol)),
                  pl.BlockSpec((Q_TILE, GATE_PAD), lambda bi, qi: (bi * nqt + qi, 0)),
                  per_b3((s, KV_WIDTH)), per_b4((KV_GROUPS, HEAD_DIM, s)),
                  per_b3((s, KV_WIDTH)), per_b4((KV_GROUPS, HEAD_DIM, s)),
                  per_b3((ncb, KV_WIDTH)), per_b4((KV_GROUPS, HEAD_DIM, ncb)),
                  pl.BlockSpec((n_blk, ncb), lambda bi, qi: (0, 0)),
                  pl.BlockSpec((1, ATT_WIDTH), lambda bi, qi: (0, 0))],
        out_specs=pl.BlockSpec((Q_TILE, ATT_WIDTH), lambda bi, qi: (bi * nqt + qi, 0)),
        out_shape=jax.ShapeDtypeStruct((b * s, ATT_WIDTH), BF16),
        scratch_shapes=[pltpu.VMEM((ATT_WIDTH, Q_TILE), F32),
                        pltpu.VMEM((KV_GROUPS, HEAD_DIM, nq), BF16),
                        pltpu.VMEM((KV_GROUPS, n_blk, Q_TILE), F32),
                        pltpu.VMEM((KV_GROUPS, HEAD_DIM, nq), F32), pltpu.VMEM((KV_GROUPS, HEAD_DIM, nq), F32),
                        pltpu.VMEM((KV_GROUPS, 1, nq), F32), pltpu.VMEM((KV_GROUPS, 1, nq), F32),
                        pltpu.VMEM((KV_GROUPS, HEAD_DIM, nq), F32)],
        compiler_params=_cparams(("arbitrary", "arbitrary")),
        name="nsa_prompt",
    )(za, gates, ks, vst, kw, vwt, kc, vct, esum, g_out.reshape(1, -1))


def _cmp_pages_kernel(pt_ref, *refs, pg):
    x_refs = refs[:pg]
    wbd_ref, pe_ref, wb_ref, kc_ref, vc_ref, xs_ref, carry_ref = refs[pg:]
    rows = pg * (PAGE_SIZE // CMP_STRIDE)
    h = pl.program_id(1)

    @pl.when(h == 0)
    def _():
        carry_ref[...] = jnp.zeros_like(carry_ref)

    for k in range(pg):
        for s in range(2):
            for gp in range(KV_GROUPS // 2):
                t = x_refs[k][s, 2 * gp:2 * gp + 2].reshape(2 * HEAD_DIM, PAGE_SIZE)
                xs_ref[s, gp, k * PAGE_SIZE:(k + 1) * PAGE_SIZE, :] = jnp.transpose(t)

    row = lax.broadcasted_iota(jnp.int32, (rows, 1), 0)
    for s, out_ref in ((0, kc_ref), (1, vc_ref)):
        acc = jnp.zeros((rows, 2 * KV_WIDTH), F32)
        for j in range(CMP_STRIDE):
            xs = jnp.concatenate([xs_ref[s, gp, pl.ds(j, rows, stride=CMP_STRIDE), :]
                                  for gp in range(KV_GROUPS // 2)], axis=1).astype(BF16)
            acc = acc + jnp.dot(xs, wbd_ref[s, j], preferred_element_type=F32)
        bias = jnp.dot(pe_ref[s], wb_ref[s], preferred_element_type=F32)[0:1]
        lo = acc[:, :KV_WIDTH]
        prev = jnp.where(row == 0, carry_ref[s, SUBLANES - 1:SUBLANES, :], pltpu.roll(lo, 1, 0))
        out_ref[...] = (prev + acc[:, KV_WIDTH:] + bias).astype(BF16)
        carry_ref[s] = lo[rows - SUBLANES:, :]


def _cmp_pages(cache_cmp, page_table, wbd, pe8, wb):
    db, n_pages = page_table.shape
    cpp = PAGE_SIZE // CMP_STRIDE
    pg = min(CMP_PG, n_pages)
    nh = n_pages // pg
    page_shape = cache_cmp.shape[1:]
    page_specs = [pl.BlockSpec((None,) + page_shape, functools.partial(
        lambda bi, hi, pt, k: (pt[bi, hi * pg + k], 0, 0, 0, 0), k=k)) for k in range(pg)]
    const = lambda shape: pl.BlockSpec(shape, lambda bi, hi, pt: (0,) * len(shape))
    out_spec = pl.BlockSpec((None, pg * cpp, KV_WIDTH), lambda bi, hi, pt: (bi, hi, 0))
    out_shape = jax.ShapeDtypeStruct((db, n_pages * cpp, KV_WIDTH), BF16)
    return pl.pallas_call(
        functools.partial(_cmp_pages_kernel, pg=pg),
        grid_spec=pltpu.PrefetchScalarGridSpec(
            num_scalar_prefetch=1,
            grid=(db, nh),
            in_specs=page_specs + [const(wbd.shape), const(pe8.shape), const(wb.shape)],
            out_specs=[out_spec, out_spec],
            scratch_shapes=[pltpu.VMEM((2, KV_GROUPS // 2, pg * PAGE_SIZE, LANES), F32),
                            pltpu.VMEM((2, SUBLANES, KV_WIDTH), F32)]),
        out_shape=[out_shape, out_shape],
        compiler_params=_cparams(("arbitrary", "arbitrary")),
        name="cmp_pages",
    )(page_table, *([cache_cmp] * pg), wbd, pe8, wb)


def _query_blockdiag(q_ref):
    qt = jnp.transpose(q_ref[...] * ATTN_SCALE)
    tiled = jnp.concatenate([qt] * KV_GROUPS, axis=0)
    rowg = lax.broadcasted_iota(jnp.int32, (KV_WIDTH, 1), 0) // HEAD_DIM
    colg = lax.broadcasted_iota(jnp.int32, (1, LANES), 1) // HEADS_PER_GROUP
    return jnp.where(rowg == colg, tiled, 0.0).astype(BF16)


def _diag_heads(o):
    rowg = lax.broadcasted_iota(jnp.int32, (N_HEADS, 1), 0) // HEADS_PER_GROUP
    out = jnp.zeros((N_HEADS, HEAD_DIM), F32)
    for g in range(KV_GROUPS):
        out = out + jnp.where(rowg == g, o[:N_HEADS, g * HEAD_DIM:(g + 1) * HEAD_DIM], 0.0)
    return out


def _smp_cmp_kernel(q_ref, kc_ref, vc_ref, esum_ref, gsum_ref, oc_ref, imp_ref, *, qpos):
    qbd = _query_blockdiag(q_ref)
    nr = kc_ref.shape[0]
    s = jnp.dot(kc_ref[...], qbd, preferred_element_type=F32)
    row = lax.broadcasted_iota(jnp.int32, (nr, 1), 0)
    mf = jnp.where(row >= 1, jnp.where((row - 1) * CMP_STRIDE + CMP_BLOCK - 1 <= qpos, 1.0, 0.0), 0.0)
    s = jnp.where(mf > 0.5, s, NEG_INF)
    p = jnp.exp(s - jnp.max(s, axis=0, keepdims=True)) * mf
    p = p / jnp.maximum(jnp.sum(p, axis=0, keepdims=True), 1e-30)
    o = jnp.dot(jnp.transpose(p).astype(BF16), vc_ref[...], preferred_element_type=F32)
    oc_ref[...] = _diag_heads(o)
    impg = _dot_exact_right(p, gsum_ref[...])
    imp_ref[...] = _dot_exact_left(esum_ref[...], impg)


def _smp_cmp(q_pad, kc, vc, n_blk_pad, qpos):
    db, nr, _ = kc.shape
    n_blk = (qpos + SLC_BLOCK) // SLC_BLOCK
    esum = _esum_matrix(n_blk_pad, nr, 1) * jnp.asarray(np.arange(n_blk_pad)[:, None] < n_blk, BF16)
    gs = (np.arange(LANES)[:, None] // HEADS_PER_GROUP == np.arange(LANES)[None, :]) & (
        np.arange(LANES)[:, None] < N_HEADS)
    gsum = jnp.asarray(gs, BF16)
    per_b = lambda shape: pl.BlockSpec((None,) + shape, lambda bi: (bi, 0, 0))
    const = lambda shape: pl.BlockSpec(shape, lambda bi: (0,) * len(shape))
    return pl.pallas_call(
        functools.partial(_smp_cmp_kernel, qpos=qpos),
        grid=(db,),
        in_specs=[per_b((LANES, HEAD_DIM)), per_b((nr, KV_WIDTH)), per_b((nr, KV_WIDTH)),
                  const((n_blk_pad, nr)), const((LANES, LANES))],
        out_specs=[per_b((N_HEADS, HEAD_DIM)), per_b((n_blk_pad, LANES))],
        out_shape=[jax.ShapeDtypeStruct((db, N_HEADS, HEAD_DIM), F32),
                   jax.ShapeDtypeStruct((db, n_blk_pad, LANES), F32)],
        compiler_params=_cparams(("arbitrary",)),
        name="smp_cmp",
    )(q_pad, kc, vc, esum, gsum)


def _smp_topk_kernel(imp_ref, idx_ref, *, n_blk, n_sel):
    nbp = imp_ref.shape[0]
    blk = lax.broadcasted_iota(jnp.int32, (nbp, 1), 0)
    cur = n_blk - 1
    valid = blk <= cur
    forced = (blk == 0) | (blk == cur) | (blk == cur - 1)
    score = jnp.where(valid, imp_ref[...] + jnp.where(forced, FORCE_BONUS, 0.0), NEG_INF)
    rank = _rank(score, nbp)
    blkf = blk.astype(F32)
    for k in range(n_sel):
        pick = jnp.sum(jnp.where(rank == float(k), blkf, 0.0), axis=0, keepdims=True)
        idx_ref[k:k + 1, :] = pick.astype(jnp.int32)


def _smp_topk(score_t, n_blk, n_sel):
    nbp, w = score_t.shape
    full = lambda shape: pl.BlockSpec(shape, lambda: (0,) * len(shape))
    return pl.pallas_call(
        functools.partial(_smp_topk_kernel, n_blk=n_blk, n_sel=n_sel),
        in_specs=[full((nbp, w))],
        out_specs=full((n_sel, w)),
        out_shape=jax.ShapeDtypeStruct((n_sel, w), jnp.int32),
        name="smp_topk",
    )(score_t)


def _smp_attn_kernel(page_ref, half_ref, isnew_ref, *refs, n_slots, n_sel, past, wbuf):
    blk_refs = refs[:n_slots]
    (q_ref, ksn_ref, win_ref, kwn_ref, oc_ref, gt_ref, go_ref, o_ref, wout_ref) = refs[n_slots:]
    b = pl.program_id(0)
    lane_half = lax.broadcasted_iota(jnp.int32, (1, PAGE_SIZE), 1) // SLC_BLOCK
    row8 = lax.broadcasted_iota(jnp.int32, (SUBLANES, 1), 0)
    wlane = lax.broadcasted_iota(jnp.int32, (1, wbuf), 1)
    kposw = past - wbuf + wlane
    wbias = jnp.where((kposw > past - WINDOW) & (kposw >= 0), 0.0, NEG_INF)
    nt = (((1,), (1,)), ((), ()))

    def bf(x):
        return x.astype(BF16).astype(F32)

    ys = []
    for g in range(KV_GROUPS):
        q = q_ref[g] * ATTN_SCALE
        qb = q.astype(BF16)
        kn = bf(ksn_ref[:, g * HEAD_DIM:(g + 1) * HEAD_DIM])
        vn = bf(ksn_ref[:, KV_WIDTH + g * HEAD_DIM:KV_WIDTH + (g + 1) * HEAD_DIM])
        kwn = bf(kwn_ref[:, g * HEAD_DIM:(g + 1) * HEAD_DIM])
        vwn = bf(kwn_ref[:, KV_WIDTH + g * HEAD_DIM:KV_WIDTH + (g + 1) * HEAD_DIM])

        slots = range(g * n_sel, (g + 1) * n_sel)
        kt = jnp.concatenate([blk_refs[k][0] for k in slots], axis=1).astype(BF16)
        vt = jnp.concatenate([blk_refs[k][1] for k in slots], axis=1).astype(BF16)
        keeps = [jnp.where(isnew_ref[b, k] == 0, 1.0, 0.0) for k in slots]
        mf = jnp.concatenate([jnp.where(lane_half == half_ref[b, k], keep, 0.0)
                              for k, keep in zip(slots, keeps)], axis=1)
        has_new = 1.0 - functools.reduce(jnp.minimum, keeps)
        s = jnp.where(mf > 0.5, jnp.dot(qb, kt, preferred_element_type=F32), NEG_INF)
        s_n = jnp.where(has_new > 0.5, jnp.sum(bf(q) * kn, axis=1, keepdims=True), NEG_INF)
        m = jnp.maximum(jnp.max(s, axis=1, keepdims=True), s_n)
        p = jnp.exp(s - m) * mf
        p_n = jnp.exp(s_n - m) * has_new
        l = jnp.sum(p, axis=1, keepdims=True) + p_n
        o_s = (lax.dot_general(p.astype(BF16), vt, nt, preferred_element_type=F32) + bf(p_n) * vn)
        o_s = o_s / jnp.maximum(l, 1e-30)

        s = jnp.dot(qb, win_ref[0, g].astype(BF16), preferred_element_type=F32) + wbias
        s_n = jnp.sum(bf(q) * kwn, axis=1, keepdims=True)
        m = jnp.maximum(jnp.max(s, axis=1, keepdims=True), s_n)
        p = jnp.exp(s - m)
        p_n = jnp.exp(s_n - m)
        l = jnp.sum(p, axis=1, keepdims=True) + p_n
        o_w = (lax.dot_general(p.astype(BF16), win_ref[1, g].astype(BF16), nt, preferred_element_type=F32)
               + bf(p_n) * vwn)
        o_w = o_w / l

        gates = jax.nn.sigmoid(gt_ref[g])
        ys.append(gates[:, 0:1] * oc_ref[g] + gates[:, 1:2] * o_s + gates[:, 2:3] * o_w)

    real = row8 < HEADS_PER_GROUP
    ssq = functools.reduce(lambda a, c: a + c, [jnp.sum(jnp.where(real, y * y, 0.0), axis=1, keepdims=True)
                                                 for y in ys])
    scale = lax.rsqrt(jnp.sum(ssq, axis=0, keepdims=True) / ATT_WIDTH + EPS)
    for g in range(KV_GROUPS):
        o_ref[g] = (ys[g] * scale * go_ref[g]).astype(BF16)

    new_col = jnp.transpose(jnp.broadcast_to(kwn_ref[...], (SUBLANES, 2 * KV_WIDTH)))[:, 0:1]
    for s2 in range(2):
        for g in range(KV_GROUPS):
            c0 = (s2 * KV_GROUPS + g) * HEAD_DIM
            wout_ref[s2, g] = jnp.where(wlane == wbuf - 1, new_col[c0:c0 + HEAD_DIM],
                                        pltpu.roll(win_ref[s2, g], wbuf - 1, 1))


def _smp_attn(page, half, isnew, cache_slc, q_g, kvs_new, cache_win, kvw_new, o_c, gates, g_out, past):
    db, n_slots = page.shape
    n_sel = n_slots // KV_GROUPS
    wbuf = cache_win.shape[-1]
    width = 2 * KV_WIDTH
    blk_specs = [pl.BlockSpec((None, 2, None, HEAD_DIM, PAGE_SIZE), functools.partial(
        lambda bi, pg, hf, nw, k: (pg[bi, k], 0, k // n_sel, 0, 0), k=k)) for k in range(n_slots)]
    per_b3 = lambda shape: pl.BlockSpec((None,) + shape, lambda bi, pg, hf, nw: (bi, 0, 0))
    per_b4 = lambda shape: pl.BlockSpec((None,) + shape, lambda bi, pg, hf, nw: (bi, 0, 0, 0))
    per_b5 = lambda shape: pl.BlockSpec((None,) + shape, lambda bi, pg, hf, nw: (bi, 0, 0, 0, 0))
    win_shape = (2, KV_GROUPS, HEAD_DIM, wbuf)
    head_shape = (KV_GROUPS, SUBLANES, HEAD_DIM)
    return pl.pallas_call(
        functools.partial(_smp_attn_kernel, n_slots=n_slots, n_sel=n_sel, past=past, wbuf=wbuf),
        grid_spec=pltpu.PrefetchScalarGridSpec(
            num_scalar_prefetch=3,
            grid=(db,),
            in_specs=blk_specs + [per_b4(head_shape), per_b3((1, width)), per_b5(win_shape), per_b3((1, width)),
                                  per_b4(head_shape), per_b4((KV_GROUPS, SUBLANES, 3)),
                                  pl.BlockSpec(head_shape, lambda bi, pg, hf, nw: (0, 0, 0))],
            out_specs=[per_b4(head_shape), per_b5(win_shape)]),
        out_shape=[jax.ShapeDtypeStruct((db,) + head_shape, BF16),
                   jax.ShapeDtypeStruct((db,) + win_shape, F32)],
        compiler_params=_cparams(("arbitrary",)),
        name="smp_attn",
    )(page, half, isnew, *([cache_slc] * n_slots), q_g, kvs_new.reshape(db, 1, width), cache_win,
      kvw_new.reshape(db, 1, width), o_c, gates, g_out)


def _nsa_sample(za, gates_raw, kvs_new, kvw_new, cache_cmp, cache_slc, cache_win, page_table, wbd, pe8, wb,
                g_out):
    db, n_pages = page_table.shape
    past = n_pages * PAGE_SIZE
    n_blk = (past + SLC_BLOCK) // SLC_BLOCK
    n_sel = min(N_SEL, n_blk)
    n_blk_pad = -(-n_blk // SUBLANES) * SUBLANES
    q = za[:, 2 * SSM_WIDTH:].reshape(db, N_HEADS, HEAD_DIM)
    q_pad = jnp.pad(q, ((0, 0), (0, LANES - N_HEADS), (0, 0)))
    kc, vc = _cmp_pages(cache_cmp, page_table, wbd, pe8, wb)
    o_c, imp = _smp_cmp(q_pad, kc, vc, n_blk_pad, past)
    score_t = jnp.transpose(imp[:, :, :KV_GROUPS], (1, 0, 2)).reshape(n_blk_pad, db * KV_GROUPS)
    lane_pad = -(-db * KV_GROUPS // LANES) * LANES
    score_t = jnp.pad(score_t, ((0, 0), (0, lane_pad - db * KV_GROUPS)))
    idx = _smp_topk(score_t, n_blk, n_sel)[:, :db * KV_GROUPS]
    idx = jnp.transpose(idx.reshape(n_sel, db, KV_GROUPS), (1, 2, 0))
    n_past_blk = past // SLC_BLOCK
    per_page = PAGE_SIZE // SLC_BLOCK
    jp = jnp.minimum(idx, n_past_blk - 1).reshape(db, KV_GROUPS * n_sel)
    page = jnp.take_along_axis(page_table, jp // per_page, axis=1).astype(jnp.int32)
    half = (jp % per_page).astype(jnp.int32)
    isnew = (idx >= n_past_blk).reshape(db, KV_GROUPS * n_sel).astype(jnp.int32)
    pad_heads = lambda a: jnp.pad(a.reshape(a.shape[0], KV_GROUPS, HEADS_PER_GROUP, a.shape[-1]),
                                  ((0, 0), (0, 0), (0, SUBLANES - HEADS_PER_GROUP), (0, 0)))
    gates = pad_heads(gates_raw[:, :3 * N_HEADS].reshape(db, N_HEADS, 3))
    g_out_g = pad_heads(g_out.reshape(1, N_HEADS, HEAD_DIM))[0]
    y, win_new = _smp_attn(page, half, isnew, cache_slc, pad_heads(q), kvs_new, cache_win, kvw_new,
                           pad_heads(o_c), gates, g_out_g, past)
    return y[:, :, :HEADS_PER_GROUP].reshape(db, ATT_WIDTH), win_new


def _outproj_kernel(x_ref, ms_ref, ma_ref, w_ref, gt_ref, sc_ref, sh_ref, g_ref, x1_ref, h2_ref):
    mixed = jnp.concatenate([ms_ref[...], ma_ref[...]], axis=1)
    x1 = x_ref[...] + gt_ref[...] * jnp.dot(mixed, w_ref[...], preferred_element_type=F32)
    x1_ref[...] = x1
    h2_ref[...] = (_rms(x1, g_ref[...]) * (1.0 + sc_ref[...]) + sh_ref[...]).astype(BF16)


def _outproj(x, m_ssm, m_att, w_out, gt1, sc2, sh2, g2, tm, rows_per_mod):
    n, d = x.shape
    r = gt1.shape[1]
    tpm = rows_per_mod // tm
    mod_spec = pl.BlockSpec((None, r, d), lambda i: (i // tpm, 0, 0))
    row = lambda w: pl.BlockSpec((tm, w), lambda i: (i, 0))
    return pl.pallas_call(
        _outproj_kernel,
        grid=(n // tm,),
        in_specs=[row(d), row(SSM_WIDTH), row(ATT_WIDTH), pl.BlockSpec((d, d), lambda i: (0, 0)),
                  mod_spec, mod_spec, mod_spec, pl.BlockSpec((1, d), lambda i: (0, 0))],
        out_specs=[row(d), row(d)],
        out_shape=[jax.ShapeDtypeStruct((n, d), F32), jax.ShapeDtypeStruct((n, d), BF16)],
        compiler_params=_cparams(("arbitrary",)),
        name="outproj",
    )(x, m_ssm, m_att, w_out, gt1, sc2, sh2, g2.reshape(1, d))


def _ffn_act(a_v, a_g, a1_v, a1_g, a2_v, a2_g, cwv_ref, cwg_ref, cbv_ref, cbg_ref):
    val = cbv_ref[...] + cwv_ref[2:3, :] * a_v + cwv_ref[0:1, :] * a2_v + cwv_ref[1:2, :] * a1_v
    gate = cbg_ref[...] + cwg_ref[2:3, :] * a_g + cwg_ref[0:1, :] * a2_g + cwg_ref[1:2, :] * a1_g
    return (gate * jax.nn.sigmoid(gate) * val).astype(BF16)


def _ffn_finish(j, contrib, x1_ref, gt_ref, gf_ref, y_ref):
    @pl.when(j == 0)
    def _():
        y_ref[...] = contrib

    @pl.when(j > 0)
    def _():
        y_ref[...] += contrib

    @pl.when(j == FFN_NF - 1)
    def _():
        y_ref[...] = _rms(x1_ref[...] + gt_ref[...] * y_ref[...], gf_ref[...])


def _ffn_seq_kernel(h_ref, x1_ref, gt_ref, wv_ref, wg_ref, cwv_ref, cwg_ref, cbv_ref, cbg_ref, wd_ref, gf_ref,
                    y_ref, tv_ref, tg_ref, sv_ref, sg_ref, hv_ref, hg_ref, *, tm, tpb):
    i = pl.program_id(0)
    j = pl.program_id(1)

    @pl.when(i % tpb == 0)
    def _():
        hv_ref[j] = jnp.zeros((SUBLANES, FFN_TF), F32)
        hg_ref[j] = jnp.zeros((SUBLANES, FFN_TF), F32)

    h = h_ref[...]
    for w_ref, s_ref, halo_ref, t_ref in ((wv_ref, sv_ref, hv_ref, tv_ref), (wg_ref, sg_ref, hg_ref, tg_ref)):
        s_ref[0:SUBLANES, :] = halo_ref[j]
        s_ref[SUBLANES:SUBLANES + tm, :] = jnp.dot(h, w_ref[...], preferred_element_type=F32)
        halo_ref[j] = s_ref[tm:tm + SUBLANES, :]
        t_ref[...] = s_ref[tm + SUBLANES - 2:tm + SUBLANES, :]
    o = SUBLANES
    act = _ffn_act(sv_ref[o:o + tm, :], sg_ref[o:o + tm, :], sv_ref[o - 1:o - 1 + tm, :], sg_ref[o - 1:o - 1 + tm, :],
                   sv_ref[o - 2:o - 2 + tm, :], sg_ref[o - 2:o - 2 + tm, :], cwv_ref, cwg_ref, cbv_ref, cbg_ref)
    _ffn_finish(j, jnp.dot(act, wd_ref[...], preferred_element_type=F32), x1_ref, gt_ref, gf_ref, y_ref)


def _ffn_step_kernel(h_ref, x1_ref, gt_ref, wv_ref, wg_ref, cwv_ref, cwg_ref, cbv_ref, cbg_ref, wd_ref, gf_ref,
                     p0v_ref, p0g_ref, p1v_ref, p1g_ref, y_ref, av_ref, ag_ref):
    j = pl.program_id(1)
    h = h_ref[...]
    a_v = jnp.dot(h, wv_ref[...], preferred_element_type=F32)
    a_g = jnp.dot(h, wg_ref[...], preferred_element_type=F32)
    av_ref[...] = a_v
    ag_ref[...] = a_g
    act = _ffn_act(a_v, a_g, p1v_ref[...], p1g_ref[...], p0v_ref[...], p0g_ref[...],
                   cwv_ref, cwg_ref, cbv_ref, cbg_ref)
    _ffn_finish(j, jnp.dot(act, wd_ref[...], preferred_element_type=F32), x1_ref, gt_ref, gf_ref, y_ref)


def _ffn_specs(d, tm, r, tpm):
    row_once = pl.BlockSpec((tm, d), lambda i, j: (i, 0), pipeline_mode=pl.Buffered(1))
    return [row_once, row_once, pl.BlockSpec((None, r, d), lambda i, j: (i // tpm, 0, 0)),
            pl.BlockSpec((d, FFN_TF), lambda i, j: (0, j)), pl.BlockSpec((d, FFN_TF), lambda i, j: (0, j + FFN_NF)),
            pl.BlockSpec((CONV_W, FFN_TF), lambda i, j: (0, j)),
            pl.BlockSpec((CONV_W, FFN_TF), lambda i, j: (0, j + FFN_NF)),
            pl.BlockSpec((1, FFN_TF), lambda i, j: (0, j)), pl.BlockSpec((1, FFN_TF), lambda i, j: (0, j + FFN_NF)),
            pl.BlockSpec((FFN_TF, d), lambda i, j: (j, 0)), pl.BlockSpec((1, d), lambda i, j: (0, 0))]


def _ffn_seq(h2, x1, gt2, w_up, conv_w, conv_b, w_down, g_final, tm, rows_per_mod):
    n, d = x1.shape
    tpb = rows_per_mod // tm
    nt = n // tm
    cb = conv_b.reshape(1, -1)
    y, tv, tg = pl.pallas_call(
        functools.partial(_ffn_seq_kernel, tm=tm, tpb=tpb),
        grid=(nt, FFN_NF),
        in_specs=_ffn_specs(d, tm, gt2.shape[1], tpb),
        out_specs=[pl.BlockSpec((tm, d), lambda i, j: (i, 0), pipeline_mode=pl.Buffered(1)),
                   pl.BlockSpec((None, CONV_W - 1, FFN_TF), lambda i, j: (i, 0, j)),
                   pl.BlockSpec((None, CONV_W - 1, FFN_TF), lambda i, j: (i, 0, j))],
        out_shape=[jax.ShapeDtypeStruct((n, d), F32),
                   jax.ShapeDtypeStruct((nt, CONV_W - 1, D_FF), F32),
                   jax.ShapeDtypeStruct((nt, CONV_W - 1, D_FF), F32)],
        scratch_shapes=[pltpu.VMEM((tm + SUBLANES, FFN_TF), F32), pltpu.VMEM((tm + SUBLANES, FFN_TF), F32),
                        pltpu.VMEM((FFN_NF, SUBLANES, FFN_TF), F32), pltpu.VMEM((FFN_NF, SUBLANES, FFN_TF), F32)],
        compiler_params=_cparams(("arbitrary", "arbitrary")),
        name="ffn_seq",
    )(h2, x1, gt2, w_up, w_up, conv_w, conv_w, cb, cb, w_down, g_final.reshape(1, d))
    tails = jnp.concatenate([tv, tg], axis=-1)
    return y, tails[tpb - 1::tpb]


def _ffn_step(h2, x1, gt2, w_up, conv_w, conv_b, w_down, g_final, conv_prev):
    n, d = x1.shape
    cb = conv_b.reshape(1, -1)
    prev_v = pl.BlockSpec((n, FFN_TF), lambda i, j: (0, j))
    prev_g = pl.BlockSpec((n, FFN_TF), lambda i, j: (0, j + FFN_NF))
    p0, p1 = conv_prev[:, 0], conv_prev[:, 1]
    y, a_v, a_g = pl.pallas_call(
        _ffn_step_kernel,
        grid=(1, FFN_NF),
        in_specs=_ffn_specs(d, n, gt2.shape[1], 1) + [prev_v, prev_g, prev_v, prev_g],
        out_specs=[pl.BlockSpec((n, d), lambda i, j: (0, 0)),
                   pl.BlockSpec((n, FFN_TF), lambda i, j: (0, j)), pl.BlockSpec((n, FFN_TF), lambda i, j: (0, j))],
        out_shape=[jax.ShapeDtypeStruct((n, d), F32),
                   jax.ShapeDtypeStruct((n, D_FF), F32), jax.ShapeDtypeStruct((n, D_FF), F32)],
        compiler_params=_cparams(("arbitrary", "arbitrary")),
        name="ffn_step",
    )(h2, x1, gt2, w_up, w_up, conv_w, conv_w, cb, cb, w_down, g_final.reshape(1, d), p0, p0, p1, p1)
    return y, jnp.stack([p1, jnp.concatenate([a_v, a_g], axis=-1)], axis=1)


def kernel(x_prompt, x_sample, cache_cmp_kv, cache_slc_kv, cache_win_kv, state_ssm_re, state_ssm_im, state_conv,
           page_table, c_prompt, c_sample, w_ada, b_ada, g_norm1, w_in, ssm_lam_re, ssm_lam_im, ssm_log_dt,
           ssm_b_re, ssm_b_im, ssm_c_re, ssm_c_im, ssm_d, w_cmp, pe_cmp, g_out_ssm, g_out_att, w_out, g_norm2,
           w_up, conv_w, conv_b, w_down, g_final):
    depth = w_in.shape[0]
    b, s, d = x_prompt.shape
    db, ds, _ = x_sample.shape
    assert depth == 1 and ds == 1 and d == D_MODEL, "kernel is written for one layer and one new token per sequence"
    assert s % 512 == 0
    tm = 512
    tm_in = 256
    tm_ffn = min(1024, s)
    gp = SSM_GROUPS * SSM_STATE
    kv_shape = (2, KV_GROUPS, HEAD_DIM)
    l = 0

    w_in_p = jnp.pad(w_in[l], ((0, 0), (0, IN_PAD - IN_WIDTH))).astype(BF16)
    w_out_b = w_out[l].astype(BF16)
    w_up_b = w_up[l].astype(BF16)
    w_down_b = w_down[l].astype(BF16)
    pwr, pwi, bbr, bbi = _s5_prep(ssm_lam_re[l], ssm_lam_im[l], ssm_log_dt[l], ssm_b_re[l], ssm_b_im[l])
    w1, w2, lvl_r, lvl_i, pw_r, pw_i = _s5_weights(pwr, pwi, bbr, bbi, ssm_c_re[l], ssm_c_im[l])
    wbd, pe8, wb = _cmp_weights(w_cmp[l], pe_cmp[l])

    n_c = b + db
    n_c_pad = -(-n_c // SUBLANES) * SUBLANES
    c_all = jnp.pad(jnp.concatenate([c_prompt, c_sample], axis=0), ((0, n_c_pad - n_c), (0, 0)))
    mod = _ada(c_all, w_ada[l], b_ada[l]).reshape(n_c_pad, 6, d)
    mod_p = [mod[:b, k].reshape(b, 1, d) for k in range(6)]
    mod_s = [mod[b:n_c, k].reshape(1, db, d) for k in range(6)]

    xp = x_prompt.reshape(b * s, d)
    za, kvc, kvs, kvw, graw = _inproj(xp, mod_p[1], mod_p[0], g_norm1[l], w_in_p, tm_in, s)
    m_ssm, st_re, st_im = _s5_prompt(za, b, s, w1, w2, lvl_r, lvl_i, pw_r, pw_i, ssm_d[l], g_out_ssm[l], 256)
    kc, vct = _cmpproj(kvc, b, s, wbd, pe8, wb)
    ks_b, vst, kw_b, vwt = _kvprep(kvs, kvw, b, s, 512)
    m_att = _nsa_prompt(za, graw, ks_b, vst, kw_b, vwt, kc, vct, g_out_att[l], b, s)
    x1, h2 = _outproj(xp, m_ssm, m_att, w_out_b, mod_p[2], mod_p[4], mod_p[3], g_norm2[l], tm, s)
    y_p, conv_p = _ffn_seq(h2, x1, mod_p[5], w_up_b, conv_w[l], conv_b[l], w_down_b, g_final, tm_ffn, s)
    wlen = min(WINDOW, s)
    win_p = kvw.reshape(b, s, *kv_shape)[:, s - wlen:]

    xs = x_sample.reshape(db, d)
    za_s, kvc_s, kvs_s, kvw_s, graw_s = _inproj(xs, mod_s[1], mod_s[0], g_norm1[l], w_in_p, db, db)
    m_ssm_s, st_re_s, st_im_s = _s5_sample(za_s, state_ssm_re[l].reshape(db, gp), state_ssm_im[l].reshape(db, gp),
                                           w1, w2, pw_r, pw_i, ssm_d[l], g_out_ssm[l])
    rows_minor = lambda c: jnp.transpose(c, (0, 2, 3, 4, 1))
    m_att_s, win_s = _nsa_sample(za_s, graw_s, kvs_s, kvw_s, rows_minor(cache_cmp_kv[l]),
                                 rows_minor(cache_slc_kv[l]), rows_minor(cache_win_kv[l]),
                                 page_table, wbd, pe8, wb, g_out_att[l])
    win_s = jnp.transpose(win_s, (0, 4, 1, 2, 3))
    x1_s, h2_s = _outproj(xs, m_ssm_s, m_att_s, w_out_b, mod_s[2], mod_s[4], mod_s[3], g_norm2[l], db, db)
    y_s, conv_s = _ffn_step(h2_s, x1_s, mod_s[5], w_up_b, conv_w[l], conv_b[l], w_down_b, g_final, state_conv[l])

    wbuf = cache_win_kv.shape[2]
    return (y_p.reshape(b, s, d), y_s.reshape(db, 1, d),
            kvc.reshape(1, b, s, *kv_shape), kvc_s.reshape(1, db, 1, *kv_shape),
            kvs.reshape(1, b, s, *kv_shape), kvs_s.reshape(1, db, 1, *kv_shape),
            win_p[None], win_s.reshape(1, db, wbuf, *kv_shape),
            st_re.reshape(1, b, SSM_GROUPS, SSM_STATE), st_im.reshape(1, b, SSM_GROUPS, SSM_STATE),
            st_re_s.reshape(1, db, SSM_GROUPS, SSM_STATE), st_im_s.reshape(1, db, SSM_GROUPS, SSM_STATE),
            conv_p[None], conv_s[None])
```

```python
import functools
import math

import jax
import jax.numpy as jnp
import numpy as np
from jax import lax
from jax.experimental import pallas as pl
from jax.experimental.pallas import tpu as pltpu

F32 = jnp.float32
BF16 = jnp.bfloat16

D_MODEL = 2048
SSM_WIDTH = D_MODEL // 2
ATT_WIDTH = D_MODEL - SSM_WIDTH
SSM_CH = 16
SSM_GROUPS = SSM_WIDTH // SSM_CH
SSM_STATE = 64
HEAD_DIM = 64
N_HEADS = ATT_WIDTH // HEAD_DIM
KV_GROUPS = 4
HEADS_PER_GROUP = N_HEADS // KV_GROUPS
KV_WIDTH = KV_GROUPS * HEAD_DIM
CMP_STRIDE = 16
CMP_BLOCK = 32
SLC_BLOCK = 64
N_SEL = 16
WINDOW = 512
PAGE_SIZE = 128
ATTN_SCALE = HEAD_DIM ** -0.5
NEG_INF = -1e30
FORCE_BONUS = 1e4
D_FF = 256 * ((8 * D_MODEL // 3 + 255) // 256)
CONV_W = 3
EPS = 1e-6
IN_WIDTH = 3 * SSM_WIDTH + 6 * KV_WIDTH + 3 * N_HEADS

LANES = 128
SUBLANES = 8
VMEM_LIMIT = 56 * 1024 * 1024

IN_TN = 512
IN_NA = 3 * SSM_WIDTH // IN_TN
IN_PAD = (IN_NA + 4) * IN_TN
GATE_PAD = LANES
SSM_LCH = 8 * SSM_STATE
SSM_NCH = SSM_GROUPS // 8
Q_TILE = 128
SEL_KT = 256
WIN_KT = 128
CMP_PG = 32
FFN_TF = 512
FFN_NF = D_FF // FFN_TF
FFN_RS = 256


def _cparams(sem):
    return pltpu.CompilerParams(dimension_semantics=sem, vmem_limit_bytes=VMEM_LIMIT)


def _rms(x, g):
    return x * lax.rsqrt(jnp.mean(x * x, axis=-1, keepdims=True) + EPS) * g


def _ada_kernel(c_ref, w_ref, b_ref, o_ref):
    c = c_ref[...]
    a = (c * jax.nn.sigmoid(c)).astype(BF16)
    o_ref[...] = jnp.dot(a, w_ref[...].astype(BF16), preferred_element_type=F32) + b_ref[...]


def _ada(c_all, w_ada, b_ada):
    r, d = c_all.shape
    n = w_ada.shape[1]
    tn = 1024
    return pl.pallas_call(
        _ada_kernel,
        grid=(n // tn,),
        in_specs=[pl.BlockSpec((r, d), lambda j: (0, 0)),
                  pl.BlockSpec((d, tn), lambda j: (0, j)),
                  pl.BlockSpec((1, tn), lambda j: (0, j))],
        out_specs=pl.BlockSpec((r, tn), lambda j: (0, j)),
        out_shape=jax.ShapeDtypeStruct((r, n), F32),
        compiler_params=_cparams(("arbitrary",)),
        name="ada",
    )(c_all, w_ada, b_ada.reshape(1, n))


def _inproj_kernel(x_ref, sc_ref, sh_ref, g_ref, w_ref, za_ref, kc_ref, ks_ref, kw_ref, gt_ref):
    h = (_rms(x_ref[...], g_ref[...]) * (1.0 + sc_ref[...]) + sh_ref[...]).astype(BF16)
    for j in range(IN_PAD // IN_TN):
        z = jnp.dot(h, w_ref[:, j * IN_TN:(j + 1) * IN_TN], preferred_element_type=F32)
        if j < IN_NA:
            za_ref[:, j * IN_TN:(j + 1) * IN_TN] = z
        elif j < IN_NA + 3:
            (kc_ref, ks_ref, kw_ref)[j - IN_NA][...] = z
        else:
            gt_ref[...] = z[:, :GATE_PAD]


def _inproj(x, sc, sh, g, w_pad, tm, rows_per_mod):
    n, d = x.shape
    r = sc.shape[1]
    tpm = rows_per_mod // tm
    mod_spec = pl.BlockSpec((None, r, d), lambda i: (i // tpm, 0, 0))
    row = lambda w: pl.BlockSpec((tm, w), lambda i: (i, 0))
    return pl.pallas_call(
        _inproj_kernel,
        grid=(n // tm,),
        in_specs=[row(d), mod_spec, mod_spec, pl.BlockSpec((1, d), lambda i: (0, 0)),
                  pl.BlockSpec((d, IN_PAD), lambda i: (0, 0), pipeline_mode=pl.Buffered(1))],
        out_specs=[row(3 * SSM_WIDTH), row(2 * KV_WIDTH), row(2 * KV_WIDTH), row(2 * KV_WIDTH), row(GATE_PAD)],
        out_shape=[jax.ShapeDtypeStruct((n, 3 * SSM_WIDTH), F32),
                   jax.ShapeDtypeStruct((n, 2 * KV_WIDTH), F32),
                   jax.ShapeDtypeStruct((n, 2 * KV_WIDTH), F32),
                   jax.ShapeDtypeStruct((n, 2 * KV_WIDTH), F32),
                   jax.ShapeDtypeStruct((n, GATE_PAD), F32)],
        compiler_params=_cparams(("arbitrary",)),
        name="inproj",
    )(x, sc, sh, g.reshape(1, d), w_pad)


def _s5_prep_kernel(lre_ref, lim_ref, ldt_ref, lrex_ref, limx_ref, bre_ref, bim_ref,
                    pwr_ref, pwi_ref, bbr_ref, bbi_ref):
    dt = jnp.exp(ldt_ref[...])

    def disc(lre, lim):
        mag = jnp.exp(lre * dt)
        ab_re = mag * jnp.cos(lim * dt)
        ab_im = mag * jnp.sin(lim * dt)
        den = lre * lre + lim * lim
        f_re = ((ab_re - 1.0) * lre + ab_im * lim) / den
        f_im = (ab_im * lre - (ab_re - 1.0) * lim) / den
        return ab_re, ab_im, f_re, f_im

    ab_re, ab_im, _, _ = disc(lre_ref[...], lim_ref[...])
    pr, pi = ab_re, ab_im
    pwr_ref[0] = pr
    pwi_ref[0] = pi
    for k in range(1, SUBLANES):
        pr, pi = pr * ab_re - pi * ab_im, pr * ab_im + pi * ab_re
        pwr_ref[k] = pr
        pwi_ref[k] = pi
    _, _, f_re, f_im = disc(lrex_ref[...], limx_ref[...])
    b_re, b_im = bre_ref[...], bim_ref[...]
    bbr_ref[...] = f_re * b_re - f_im * b_im
    bbi_ref[...] = f_re * b_im + f_im * b_re


def _s5_prep(lam_re, lam_im, log_dt, b_re, b_im):
    g, p = lam_re.shape
    ch = b_re.shape[-1]
    lrex = jnp.repeat(lam_re, ch, axis=1)
    limx = jnp.repeat(lam_im, ch, axis=1)
    full = lambda shape: pl.BlockSpec(shape, lambda: (0,) * len(shape))
    return pl.pallas_call(
        _s5_prep_kernel,
        in_specs=[full((g, p)), full((g, p)), full((g, 1)), full((g, p * ch)), full((g, p * ch)),
                  full((g, p * ch)), full((g, p * ch))],
        out_specs=[full((SUBLANES, g, p)), full((SUBLANES, g, p)), full((g, p * ch)), full((g, p * ch))],
        out_shape=[jax.ShapeDtypeStruct((SUBLANES, g, p), F32), jax.ShapeDtypeStruct((SUBLANES, g, p), F32),
                   jax.ShapeDtypeStruct((g, p * ch), F32), jax.ShapeDtypeStruct((g, p * ch), F32)],
        name="s5_prep",
    )(lam_re, lam_im, log_dt.reshape(g, 1), lrex, limx, b_re.reshape(g, p * ch), b_im.reshape(g, p * ch))


def _s5_weights(pwr, pwi, bbr, bbi, c_re, c_im):
    g, p, ch = SSM_GROUPS, SSM_STATE, SSM_CH
    eye = jnp.eye(8, dtype=F32)

    def w_in(bb):
        bb = bb.reshape(SSM_NCH, 8, p, ch)
        return jnp.einsum('jgpc,gh->jgchp', bb, eye).reshape(SSM_NCH, 8 * ch, 8 * p)

    def w_out(c):
        c = c.reshape(SSM_NCH, 8, ch, p)
        return jnp.einsum('jgcp,gh->jgphc', c, eye).reshape(SSM_NCH, 8 * p, 8 * ch)

    w1 = jnp.concatenate([w_in(bbr), w_in(bbi)], axis=-1).astype(BF16)
    w2 = jnp.concatenate([w_out(c_re), w_out(-c_im)], axis=1).astype(BF16)
    pw_r = pwr.reshape(SUBLANES, g * p)
    pw_i = pwi.reshape(SUBLANES, g * p)
    tau = jnp.arange(SUBLANES)[:, None]
    lvl_r = jnp.stack([jnp.where(tau >= d, pw_r[d - 1][None, :], 0.0) for d in (1, 2, 4)])
    lvl_i = jnp.stack([jnp.where(tau >= d, pw_i[d - 1][None, :], 0.0) for d in (1, 2, 4)])
    return w1, w2, lvl_r, lvl_i, pw_r, pw_i


def _s5_post(y, u, g_glu, d_skip, g_out):
    y = y + d_skip * u
    y = jax.nn.gelu(y) * jax.nn.sigmoid(g_glu)
    return _rms(y, g_out).astype(BF16)


def _s5_scan_kernel(u_ref, gg_ref, w1_ref, w2_ref, lr_ref, li_ref, pr_ref, pi_ref, d_ref, go_ref,
                    o_ref, sr_ref, si_ref, br_ref, bi_ref, y_ref, hr_ref, hi_ref, *, tt):
    t = pl.program_id(1)
    nrt = tt // SUBLANES

    @pl.when(t == 0)
    def _():
        hr_ref[...] = jnp.zeros_like(hr_ref)
        hi_ref[...] = jnp.zeros_like(hi_ref)

    def chunk(j, carry):
        lo = pl.multiple_of(j * SSM_LCH, SSM_LCH)
        uo = pl.multiple_of(j * LANES, LANES)
        ub = u_ref[:, pl.ds(uo, LANES)].astype(BF16)
        bu = jnp.dot(ub, w1_ref[j], preferred_element_type=F32)
        xr = bu[:, :SSM_LCH].reshape(nrt, SUBLANES, SSM_LCH)
        xi = bu[:, SSM_LCH:].reshape(nrt, SUBLANES, SSM_LCH)
        for lvl, d in enumerate((1, 2, 4)):
            ar = lr_ref[lvl, :, pl.ds(lo, SSM_LCH)]
            ai = li_ref[lvl, :, pl.ds(lo, SSM_LCH)]
            zr = pltpu.roll(xr, d, 1)
            zi = pltpu.roll(xi, d, 1)
            xr, xi = xr + ar * zr - ai * zi, xi + ar * zi + ai * zr
        br_ref[...] = xr.reshape(tt, SSM_LCH)
        bi_ref[...] = xi.reshape(tt, SSM_LCH)
        pr = pr_ref[:, pl.ds(lo, SSM_LCH)]
        pi = pi_ref[:, pl.ds(lo, SSM_LCH)]

        def tile(k, h):
            hr, hi = h
            r0 = pl.multiple_of(k * SUBLANES, SUBLANES)
            vr = br_ref[pl.ds(r0, SUBLANES), :] + pr * hr - pi * hi
            vi = bi_ref[pl.ds(r0, SUBLANES), :] + pr * hi + pi * hr
            br_ref[pl.ds(r0, SUBLANES), :] = vr
            bi_ref[pl.ds(r0, SUBLANES), :] = vi
            return vr[SUBLANES - 1:, :], vi[SUBLANES - 1:, :]

        hr, hi = lax.fori_loop(0, nrt, tile, (hr_ref[:, pl.ds(lo, SSM_LCH)], hi_ref[:, pl.ds(lo, SSM_LCH)]))
        hr_ref[:, pl.ds(lo, SSM_LCH)] = hr
        hi_ref[:, pl.ds(lo, SSM_LCH)] = hi
        hcat = jnp.concatenate([br_ref[...], bi_ref[...]], axis=1).astype(BF16)
        y_ref[:, pl.ds(uo, LANES)] = jnp.dot(hcat, w2_ref[j], preferred_element_type=F32)
        return carry

    lax.fori_loop(0, SSM_NCH, chunk, 0)
    o_ref[...] = _s5_post(y_ref[...], u_ref[...], gg_ref[...], d_ref[...], go_ref[...])
    sr_ref[...] = hr_ref[...]
    si_ref[...] = hi_ref[...]


def _s5_prompt(za, b, s, w1, w2, lvl_r, lvl_i, pw_r, pw_i, d_skip, g_out, tt):
    nt = s // tt
    gp = SSM_GROUPS * SSM_STATE
    const2 = lambda shape: pl.BlockSpec(shape, lambda bi, ti: (0,) * len(shape))
    st_spec = pl.BlockSpec((None, 1, gp), lambda bi, ti: (bi, 0, 0))
    return pl.pallas_call(
        functools.partial(_s5_scan_kernel, tt=tt),
        grid=(b, nt),
        in_specs=[pl.BlockSpec((tt, SSM_WIDTH), lambda bi, ti: (bi * nt + ti, 0)),
                  pl.BlockSpec((tt, SSM_WIDTH), lambda bi, ti: (bi * nt + ti, 1)),
                  const2(w1.shape), const2(w2.shape), const2(lvl_r.shape), const2(lvl_i.shape),
                  const2(pw_r.shape), const2(pw_i.shape), const2((1, SSM_WIDTH)), const2((1, SSM_WIDTH))],
        out_specs=[pl.BlockSpec((tt, SSM_WIDTH), lambda bi, ti: (bi * nt + ti, 0)), st_spec, st_spec],
        out_shape=[jax.ShapeDtypeStruct((b * s, SSM_WIDTH), BF16),
                   jax.ShapeDtypeStruct((b, 1, gp), F32), jax.ShapeDtypeStruct((b, 1, gp), F32)],
        scratch_shapes=[pltpu.VMEM((tt, SSM_LCH), F32), pltpu.VMEM((tt, SSM_LCH), F32),
                        pltpu.VMEM((tt, SSM_WIDTH), F32),
                        pltpu.VMEM((1, gp), F32), pltpu.VMEM((1, gp), F32)],
        compiler_params=_cparams(("arbitrary", "arbitrary")),
        name="s5_prompt",
    )(za, za, w1, w2, lvl_r, lvl_i, pw_r, pw_i, d_skip.reshape(1, -1), g_out.reshape(1, -1))


def _s5_step_kernel(u_ref, gg_ref, h0r_ref, h0i_ref, w1_ref, w2_ref, pr_ref, pi_ref, d_ref, go_ref,
                    o_ref, sr_ref, si_ref, y_ref):
    for j in range(SSM_NCH):
        lo, uo = j * SSM_LCH, j * LANES
        bu = jnp.dot(u_ref[:, uo:uo + LANES].astype(BF16), w1_ref[j], preferred_element_type=F32)
        ar = pr_ref[0:1, lo:lo + SSM_LCH]
        ai = pi_ref[0:1, lo:lo + SSM_LCH]
        h0r = h0r_ref[:, lo:lo + SSM_LCH]
        h0i = h0i_ref[:, lo:lo + SSM_LCH]
        hr = bu[:, :SSM_LCH] + (ar * h0r - ai * h0i)
        hi = bu[:, SSM_LCH:] + (ar * h0i + ai * h0r)
        sr_ref[:, lo:lo + SSM_LCH] = hr
        si_ref[:, lo:lo + SSM_LCH] = hi
        hcat = jnp.concatenate([hr, hi], axis=1).astype(BF16)
        y_ref[:, uo:uo + LANES] = jnp.dot(hcat, w2_ref[j], preferred_element_type=F32)
    o_ref[...] = _s5_post(y_ref[...], u_ref[...], gg_ref[...], d_ref[...], go_ref[...])


def _s5_sample(za, h0r, h0i, w1, w2, pw_r, pw_i, d_skip, g_out):
    n = za.shape[0]
    gp = SSM_GROUPS * SSM_STATE
    full = lambda shape: pl.BlockSpec(shape, lambda i: (0,) * len(shape))
    return pl.pallas_call(
        _s5_step_kernel,
        grid=(1,),
        in_specs=[pl.BlockSpec((n, SSM_WIDTH), lambda i: (0, 0)), pl.BlockSpec((n, SSM_WIDTH), lambda i: (0, 1)),
                  full((n, gp)), full((n, gp)), full(w1.shape), full(w2.shape),
                  full(pw_r.shape), full(pw_i.shape), full((1, SSM_WIDTH)), full((1, SSM_WIDTH))],
        out_specs=[full((n, SSM_WIDTH)), full((n, gp)), full((n, gp))],
        out_shape=[jax.ShapeDtypeStruct((n, SSM_WIDTH), BF16),
                   jax.ShapeDtypeStruct((n, gp), F32), jax.ShapeDtypeStruct((n, gp), F32)],
        scratch_shapes=[pltpu.VMEM((n, SSM_WIDTH), F32)],
        compiler_params=_cparams(("arbitrary",)),
        name="s5_sample",
    )(za, za, h0r, h0i, w1, w2, pw_r, pw_i, d_skip.reshape(1, -1), g_out.reshape(1, -1))


def _cmp_weights(w_cmp, pe_cmp):
    eye = jnp.eye(KV_GROUPS, dtype=F32)

    def bd(w):
        return jnp.einsum('sjde,gh->sjgdhe', w, eye).reshape(2, CMP_STRIDE, KV_WIDTH, KV_WIDTH)

    wbd = jnp.concatenate([bd(w_cmp[:, :CMP_STRIDE]), bd(w_cmp[:, CMP_STRIDE:])], axis=-1).astype(BF16)
    pe8 = jnp.zeros((2, SUBLANES, CMP_BLOCK * HEAD_DIM), F32).at[:, 0].set(pe_cmp.reshape(2, -1)).astype(BF16)
    wb = jnp.tile(w_cmp.reshape(2, CMP_BLOCK * HEAD_DIM, HEAD_DIM), (1, 1, KV_GROUPS)).astype(BF16)
    return wbd, pe8, wb


def _split3(x):
    a = x.astype(BF16)
    r = x - a.astype(F32)
    b = r.astype(BF16)
    c = (r - b.astype(F32)).astype(BF16)
    return a, b, c


def _dot_exact_left(e, x):
    a, b, c = _split3(x)
    return (jnp.dot(e, a, preferred_element_type=F32) + jnp.dot(e, b, preferred_element_type=F32)
            + jnp.dot(e, c, preferred_element_type=F32))


def _dot_exact_right(x, e):
    a, b, c = _split3(x)
    return (jnp.dot(a, e, preferred_element_type=F32) + jnp.dot(b, e, preferred_element_type=F32)
            + jnp.dot(c, e, preferred_element_type=F32))


def _rank(score, n_cand):
    n_tiles = score.shape[0] // SUBLANES
    tiles = [score[v * SUBLANES:(v + 1) * SUBLANES] for v in range(n_tiles)]
    ranks = [jnp.zeros(t.shape, F32) for t in tiles]
    row = lax.broadcasted_iota(jnp.int32, (SUBLANES, 1), 0)
    for i in range(n_cand):
        si = tiles[i // SUBLANES][i % SUBLANES:i % SUBLANES + 1]
        for v in range(n_tiles):
            ge = jnp.where(si >= tiles[v], 1.0, 0.0)
            gt = jnp.where(si > tiles[v], 1.0, 0.0)
            if i < v * SUBLANES:
                beats = ge
            elif i >= (v + 1) * SUBLANES:
                beats = gt
            else:
                beats = jnp.where(row > i % SUBLANES, ge, gt)
            ranks[v] = ranks[v] + beats
    return jnp.concatenate(ranks, axis=0)


def _col_reduce(x, op, final):
    slabs = [x[i * SUBLANES:(i + 1) * SUBLANES] for i in range(x.shape[0] // SUBLANES)]
    while len(slabs) > 1:
        pairs = [op(slabs[i], slabs[i + 1]) for i in range(0, len(slabs) - 1, 2)]
        slabs = pairs + ([slabs[-1]] if len(slabs) % 2 else [])
    return final(slabs[0], axis=0, keepdims=True)


def _softmax_stage(g, s, bias, pv_prev, m_ref, l_ref, acc_ref, p_ref, rows):
    s = s + jnp.concatenate([bias] * HEADS_PER_GROUP, axis=1)
    m_old = m_ref[g]
    m_new = jnp.maximum(m_old, _col_reduce(s, jnp.maximum, jnp.max))
    alpha = jnp.exp(m_old - m_new)
    p = jnp.exp(s - m_new)
    l_ref[g] = alpha * l_ref[g] + _col_reduce(p, jnp.add, jnp.sum)
    acc_ref[g] = alpha * (acc_ref[g] + pv_prev)
    m_ref[g] = m_new
    p_ref[g, rows, :] = p.astype(BF16)


def _cmpproj_kernel(x_ref, wbd_ref, pe_ref, wb_ref, kc_ref, vct_ref):
    nch = x_ref.shape[0]
    outs = []
    for s in range(2):
        acc = jnp.zeros((nch, 2 * KV_WIDTH), F32)
        for j in range(CMP_STRIDE):
            c0 = j * 2 * KV_WIDTH + s * KV_WIDTH
            acc = acc + jnp.dot(x_ref[:, c0:c0 + KV_WIDTH].astype(BF16), wbd_ref[s, j],
                                preferred_element_type=F32)
        bias = jnp.dot(pe_ref[s], wb_ref[s], preferred_element_type=F32)[0:1]
        outs.append(acc[:, :KV_WIDTH] + pltpu.roll(acc[:, KV_WIDTH:], nch - 1, 0) + bias)
    kc_ref[...] = outs[0].astype(BF16)
    vct_ref[...] = jnp.transpose(outs[1]).reshape(KV_GROUPS, HEAD_DIM, nch).astype(BF16)


def _cmpproj(kvc, b, s, wbd, pe8, wb):
    nch = s // CMP_STRIDE
    x = kvc.reshape(b, nch, CMP_STRIDE * 2 * KV_WIDTH)
    const = lambda shape: pl.BlockSpec(shape, lambda bi: (0,) * len(shape))
    return pl.pallas_call(
        _cmpproj_kernel,
        grid=(b,),
        in_specs=[pl.BlockSpec((None, nch, CMP_STRIDE * 2 * KV_WIDTH), lambda bi: (bi, 0, 0)),
                  const(wbd.shape), const(pe8.shape), const(wb.shape)],
        out_specs=[pl.BlockSpec((None, nch, KV_WIDTH), lambda bi: (bi, 0, 0)),
                   pl.BlockSpec((None, KV_GROUPS, HEAD_DIM, nch), lambda bi: (bi, 0, 0, 0))],
        out_shape=[jax.ShapeDtypeStruct((b, nch, KV_WIDTH), BF16),
                   jax.ShapeDtypeStruct((b, KV_GROUPS, HEAD_DIM, nch), BF16)],
        compiler_params=_cparams(("arbitrary",)),
        name="cmpproj",
    )(x, wbd, pe8, wb)


def _kvprep_kernel(ks_ref, kw_ref, kso_ref, vsto_ref, kwo_ref, vwto_ref):
    tk = ks_ref.shape[0]
    for src, ko, vto in ((ks_ref, kso_ref, vsto_ref), (kw_ref, kwo_ref, vwto_ref)):
        x = src[...]
        ko[...] = x[:, :KV_WIDTH].astype(BF16)
        vto[...] = jnp.transpose(x[:, KV_WIDTH:]).reshape(KV_GROUPS, HEAD_DIM, tk).astype(BF16)


def _kvprep(kvs, kvw, b, s, tk):
    nt = s // tk
    in_spec = pl.BlockSpec((tk, 2 * KV_WIDTH), lambda bi, ti: (bi * nt + ti, 0))
    k_spec = pl.BlockSpec((None, tk, KV_WIDTH), lambda bi, ti: (bi, ti, 0))
    vt_spec = pl.BlockSpec((None, KV_GROUPS, HEAD_DIM, tk), lambda bi, ti: (bi, 0, 0, ti))
    k_shape = jax.ShapeDtypeStruct((b, s, KV_WIDTH), BF16)
    vt_shape = jax.ShapeDtypeStruct((b, KV_GROUPS, HEAD_DIM, s), BF16)
    return pl.pallas_call(
        _kvprep_kernel,
        grid=(b, nt),
        in_specs=[in_spec, in_spec],
        out_specs=[k_spec, vt_spec, k_spec, vt_spec],
        out_shape=[k_shape, vt_shape, k_shape, vt_shape],
        compiler_params=_cparams(("arbitrary", "arbitrary")),
        name="kvprep",
    )(kvs, kvw)


def _nsa_prompt_kernel(q_ref, gt_ref, ks_ref, vst_ref, kw_ref, vwt_ref, kc_ref, vct_ref, esum_ref, go_ref,
                       o_ref, yt_ref, qst_ref, selb_ref, oc_ref, os_ref, m_ref, l_ref, acc_ref,
                       sa_ref, sb_ref, pa_ref, pb_ref, *, n_blk, n_sel):
    qi = pl.program_id(1)
    q0 = qi * Q_TILE
    ncb = kc_ref.shape[0]
    bpt = SEL_KT // SLC_BLOCK
    tcol = lax.broadcasted_iota(jnp.int32, (1, Q_TILE), 1)
    tpos = q0 + tcol
    tpos4 = jnp.concatenate([tpos] * HEADS_PER_GROUP, axis=1)
    cend = lax.broadcasted_iota(jnp.int32, (ncb, 1), 0) * CMP_STRIDE + (CMP_BLOCK - 1)
    blk = lax.broadcasted_iota(jnp.int32, (n_blk, 1), 0)
    cur = tpos // SLC_BLOCK
    valid = blk <= cur
    forced = (blk == 0) | (blk == cur) | (blk == cur - 1)
    flash_refs = (qst_ref, m_ref, l_ref, acc_ref)

    def keys(ref, k0, n, g):
        lane0 = (g // 2) * LANES
        return ref[pl.ds(k0, n), lane0:lane0 + LANES][:, (g % 2) * HEAD_DIM:(g % 2 + 1) * HEAD_DIM]

    def reset():
        m_ref[...] = jnp.full(m_ref.shape, NEG_INF, F32)
        l_ref[...] = jnp.zeros(l_ref.shape, F32)
        acc_ref[...] = jnp.zeros(acc_ref.shape, F32)

    for g in range(KV_GROUPS):
        qt = jnp.transpose(q_ref[:, g * 256:(g + 1) * 256] * ATTN_SCALE)
        qst = jnp.concatenate([qt[r * HEAD_DIM:(r + 1) * HEAD_DIM] for r in range(HEADS_PER_GROUP)],
                              axis=1).astype(BF16)
        qst_ref[g] = qst
        sc = jnp.dot(keys(kc_ref, 0, ncb, g), qst, preferred_element_type=F32)
        mcf = jnp.where(cend <= tpos4, 1.0, 0.0)
        sc = jnp.where(mcf > 0.5, sc, NEG_INF)
        p = jnp.exp(sc - jnp.max(sc, axis=0, keepdims=True)) * mcf
        p = p / jnp.maximum(jnp.sum(p, axis=0, keepdims=True), 1e-30)
        oc_ref[g] = jnp.dot(vct_ref[g], p.astype(BF16), preferred_element_type=F32)
        imp = p[:, 0:Q_TILE]
        for r in range(1, HEADS_PER_GROUP):
            imp = imp + p[:, r * Q_TILE:(r + 1) * Q_TILE]
        impb = _dot_exact_left(esum_ref[...], imp)
        score = jnp.where(valid, impb + jnp.where(forced, FORCE_BONUS, 0.0), NEG_INF)
        selb_ref[g] = jnp.where(_rank(score, n_blk) < n_sel, 0.0, NEG_INF)

    def sel_bias(g, kt):
        rows = selb_ref[g, pl.ds(kt * bpt, bpt), :]
        return jnp.concatenate([jnp.broadcast_to(rows[i:i + 1], (SLC_BLOCK, Q_TILE)) for i in range(bpt)], axis=0)

    n_tiles = (q0 + Q_TILE + SEL_KT - 1) // SEL_KT
    krow_s = lax.broadcasted_iota(jnp.int32, (SEL_KT, 1), 0)

    def sel_scores(kt, s_ref):
        k0 = pl.multiple_of(jnp.minimum(kt, n_tiles - 1) * SEL_KT, SEL_KT)
        for g in range(KV_GROUPS):
            s_ref[g] = jnp.dot(keys(ks_ref, k0, SEL_KT, g), qst_ref[g], preferred_element_type=F32)

    def sel_pv(kt, p_ref):
        k0 = pl.multiple_of(jnp.clip(kt, 0, n_tiles - 1) * SEL_KT, SEL_KT)
        return [jnp.dot(vst_ref[g, :, pl.ds(k0, SEL_KT)], p_ref[g], preferred_element_type=F32)
                for g in range(KV_GROUPS)]

    def sel_softmax(kt, s_ref, p_ref, pv):
        causal = jnp.where(kt * SEL_KT + krow_s <= tpos, 0.0, NEG_INF)
        for g in range(KV_GROUPS):
            _softmax_stage(g, s_ref[g], sel_bias(g, kt) + causal, pv[g], m_ref, l_ref, acc_ref, p_ref,
                           slice(0, SEL_KT))

    def sel_step(kt, s_cur, s_nxt, p_cur, p_prv):
        sel_scores(kt + 1, s_nxt)
        pv = sel_pv(kt - 1, p_prv)
        sel_softmax(kt, s_cur, p_cur, pv)

    def sel_pair(i, carry):
        sel_step(2 * i, sa_ref, sb_ref, pa_ref, pb_ref)
        sel_step(2 * i + 1, sb_ref, sa_ref, pb_ref, pa_ref)
        return carry

    reset()
    pb_ref[...] = jnp.zeros(pb_ref.shape, BF16)
    sel_scores(0, sa_ref)
    n_pairs = (n_tiles + 1) // 2
    lax.fori_loop(0, n_pairs, sel_pair, 0)
    pv = sel_pv(2 * n_pairs - 1, pb_ref)
    for g in range(KV_GROUPS):
        os_ref[g] = (acc_ref[g] + pv[g]) / jnp.maximum(l_ref[g], 1e-30)

    reset()
    krow = lax.broadcasted_iota(jnp.int32, (WIN_KT, 1), 0)
    n_back = WINDOW // WIN_KT
    wrows = slice(0, WIN_KT)

    def win_k0(t):
        return pl.multiple_of(jnp.maximum(qi - t, 0) * WIN_KT, WIN_KT)

    def win_scores(t, s_ref):
        for g in range(KV_GROUPS):
            s_ref[g, wrows, :] = jnp.dot(keys(kw_ref, win_k0(t), WIN_KT, g), qst_ref[g],
                                         preferred_element_type=F32)

    def win_pv(t, p_ref):
        return [jnp.dot(vwt_ref[g, :, pl.ds(win_k0(t), WIN_KT)], p_ref[g, wrows, :], preferred_element_type=F32)
                for g in range(KV_GROUPS)]

    def win_bias(t):
        if t == 0:
            return jnp.where(krow <= tcol, 0.0, NEG_INF)
        off = jnp.where(qi - t >= 0, 0.0, NEG_INF)
        if t == n_back:
            return jnp.where(krow > tcol, off, NEG_INF)
        return jnp.zeros((WIN_KT, Q_TILE), F32) + off

    bufs = ((sa_ref, pa_ref), (sb_ref, pb_ref))
    win_scores(0, sa_ref)
    for t in range(n_back + 1):
        (s_cur, p_cur), (s_nxt, p_prv) = bufs[t % 2], bufs[(t + 1) % 2]
        if t < n_back:
            win_scores(t + 1, s_nxt)
        pv = win_pv(t - 1, p_prv) if t > 0 else [0.0] * KV_GROUPS
        bias = win_bias(t)
        for g in range(KV_GROUPS):
            _softmax_stage(g, s_cur[g, wrows, :], bias, pv[g], m_ref, l_ref, acc_ref, p_cur, wrows)
    pv = win_pv(n_back, bufs[n_back % 2][1])

    gates = jax.nn.sigmoid(jnp.transpose(gt_ref[...]))
    for g in range(KV_GROUPS):
        o_c, o_s = oc_ref[g], os_ref[g]
        o_w = (acc_ref[g] + pv[g]) / jnp.maximum(l_ref[g], 1e-30)
        for r in range(HEADS_PER_GROUP):
            cs = slice(r * Q_TILE, (r + 1) * Q_TILE)
            gi = (g * HEADS_PER_GROUP + r) * 3
            y = (gates[gi:gi + 1] * o_c[:, cs] + gates[gi + 1:gi + 2] * o_s[:, cs]
                 + gates[gi + 2:gi + 3] * o_w[:, cs])
            row0 = (g * HEADS_PER_GROUP + r) * HEAD_DIM
            yt_ref[row0:row0 + HEAD_DIM, :] = y

    o_ref[...] = _rms(jnp.transpose(yt_ref[...]), go_ref[...]).astype(BF16)


def _esum_matrix(n_blk, n_rows, row_of_block0):
    r = SLC_BLOCK // CMP_STRIDE
    j = np.arange(n_blk)[:, None]
    i = np.arange(n_rows)[None, :] - row_of_block0
    e = (i >= r * j - 1) & (i <= r * j + r - 1) & (i >= 0)
    return jnp.asarray(e, BF16)


def _nsa_prompt(za, gates, ks, vst, kw, vwt, kc, vct, g_out, b, s):
    assert WINDOW % WIN_KT == 0 and WIN_KT == Q_TILE and s % SEL_KT == 0
    nqt = s // Q_TILE
    nq = HEADS_PER_GROUP * Q_TILE
    n_blk = s // SLC_BLOCK
    n_sel = min(N_SEL, n_blk)
    ncb = s // CMP_STRIDE
    esum = _esum_matrix(n_blk, ncb, 0) * jnp.asarray(np.arange(ncb)[None, :] < ncb - 1, BF16)
    qcol = 2 * SSM_WIDTH // ATT_WIDTH
    per_b3 = lambda shape: pl.BlockSpec((None,) + shape, lambda bi, qi: (bi, 0, 0))
    per_b4 = lambda shape: pl.BlockSpec((None,) + shape, lambda bi, qi: (bi, 0, 0, 0))
    return pl.pallas_call(
        functools.partial(_nsa_prompt_kernel, n_blk=n_blk, n_sel=n_sel),
        grid=(b, nqt),
        in_specs=[pl.BlockSpec((Q_TILE, ATT_WIDTH), lambda bi, qi: (bi * nqt + qi, qcol)),
                  pl.BlockSpec((Q_TILE, GATE_PAD), lambda bi, qi: (bi * nqt + qi, 0)),
                  per_b3((s, KV_WIDTH)), per_b4((KV_GROUPS, HEAD_DIM, s)),
                  per_b3((s, KV_WIDTH)), per_b4((KV_GROUPS, HEAD_DIM, s)),
                  per_b3((ncb, KV_WIDTH)), per_b4((KV_GROUPS, HEAD_DIM, ncb)),
                  pl.BlockSpec((n_blk, ncb), lambda bi, qi: (0, 0)),
                  pl.BlockSpec((1, ATT_WIDTH), lambda bi, qi: (0, 0))],
        out_specs=pl.BlockSpec((Q_TILE, ATT_WIDTH), lambda bi, qi: (bi * nqt + qi, 0)),
        out_shape=jax.ShapeDtypeStruct((b * s, ATT_WIDTH), BF16),
        scratch_shapes=[pltpu.VMEM((ATT_WIDTH, Q_TILE), F32),
                        pltpu.VMEM((KV_GROUPS, HEAD_DIM, nq), BF16),
                        pltpu.VMEM((KV_GROUPS, n_blk, Q_TILE), F32),
                        pltpu.VMEM((KV_GROUPS, HEAD_DIM, nq), F32), pltpu.VMEM((KV_GROUPS, HEAD_DIM, nq), F32),
                        pltpu.VMEM((KV_GROUPS, 1, nq), F32), pltpu.VMEM((KV_GROUPS, 1, nq), F32),
                        pltpu.VMEM((KV_GROUPS, HEAD_DIM, nq), F32),
                        pltpu.VMEM((KV_GROUPS, SEL_KT, nq), F32), pltpu.VMEM((KV_GROUPS, SEL_KT, nq), F32),
                        pltpu.VMEM((KV_GROUPS, SEL_KT, nq), BF16), pltpu.VMEM((KV_GROUPS, SEL_KT, nq), BF16)],
        compiler_params=_cparams(("arbitrary", "arbitrary")),
        name="nsa_prompt",
    )(za, gates, ks, vst, kw, vwt, kc, vct, esum, g_out.reshape(1, -1))


def _cmp_pages_kernel(pt_ref, *refs, pg):
    x_refs = refs[:pg]
    wbd_ref, pe_ref, wb_ref, kc_ref, vc_ref, xk_ref, xv_ref, carry_ref = refs[pg:]
    rows = pg * (PAGE_SIZE // CMP_STRIDE)
    h = pl.program_id(1)

    @pl.when(h == 0)
    def _():
        carry_ref[...] = jnp.zeros_like(carry_ref)

    cpp = PAGE_SIZE // CMP_STRIDE
    pitch = rows + SUBLANES
    def regroup(s, xs_ref, pages):
        for k in pages:
            for gp in range(KV_GROUPS // 2):
                t = jnp.transpose(x_refs[k][s, 2 * gp:2 * gp + 2].reshape(2 * HEAD_DIM, PAGE_SIZE))
                for n in range(cpp):
                    xs_ref[gp, pl.ds(k * cpp + n, CMP_STRIDE, stride=pitch), :] = (
                        t[n * CMP_STRIDE:(n + 1) * CMP_STRIDE])

    regroup(0, xk_ref, range(pg))
    ppj = -(-pg // CMP_STRIDE)
    row = lax.broadcasted_iota(jnp.int32, (rows, 1), 0)
    for s, out_ref, xs_ref in ((0, kc_ref, xk_ref), (1, vc_ref, xv_ref)):
        acc = jnp.zeros((rows, 2 * KV_WIDTH), F32)
        for j in range(CMP_STRIDE):
            xs = jnp.concatenate([xs_ref[gp, j * pitch:j * pitch + rows, :]
                                  for gp in range(KV_GROUPS // 2)], axis=1).astype(BF16)
            acc = acc + jnp.dot(xs, wbd_ref[s, j], preferred_element_type=F32)
            if s == 0:
                regroup(1, xv_ref, range(min(j * ppj, pg), min((j + 1) * ppj, pg)))
        bias = jnp.dot(pe_ref[s], wb_ref[s], preferred_element_type=F32)[0:1]
        lo = acc[:, :KV_WIDTH]
        prev = jnp.where(row == 0, carry_ref[s, SUBLANES - 1:SUBLANES, :], pltpu.roll(lo, 1, 0))
        out_ref[...] = (prev + acc[:, KV_WIDTH:] + bias).astype(BF16)
        carry_ref[s] = lo[rows - SUBLANES:, :]


def _cmp_pages(cache_cmp, page_table, wbd, pe8, wb):
    db, n_pages = page_table.shape
    cpp = PAGE_SIZE // CMP_STRIDE
    pg = min(CMP_PG, n_pages)
    nh = n_pages // pg
    page_shape = cache_cmp.shape[1:]
    page_specs = [pl.BlockSpec((None,) + page_shape, functools.partial(
        lambda bi, hi, pt, k: (pt[bi, hi * pg + k], 0, 0, 0, 0), k=k)) for k in range(pg)]
    const = lambda shape: pl.BlockSpec(shape, lambda bi, hi, pt: (0,) * len(shape))
    out_spec = pl.BlockSpec((None, pg * cpp, KV_WIDTH), lambda bi, hi, pt: (bi, hi, 0))
    out_shape = jax.ShapeDtypeStruct((db, n_pages * cpp, KV_WIDTH), BF16)
    return pl.pallas_call(
        functools.partial(_cmp_pages_kernel, pg=pg),
        grid_spec=pltpu.PrefetchScalarGridSpec(
            num_scalar_prefetch=1,
            grid=(db, nh),
            in_specs=page_specs + [const(wbd.shape), const(pe8.shape), const(wb.shape)],
            out_specs=[out_spec, out_spec],
            scratch_shapes=[pltpu.VMEM((KV_GROUPS // 2, CMP_STRIDE * (pg * cpp + SUBLANES), LANES), F32),
                            pltpu.VMEM((KV_GROUPS // 2, CMP_STRIDE * (pg * cpp + SUBLANES), LANES), F32),
                            pltpu.VMEM((2, SUBLANES, KV_WIDTH), F32)]),
        out_shape=[out_shape, out_shape],
        compiler_params=_cparams(("arbitrary", "arbitrary")),
        name="cmp_pages",
    )(page_table, *([cache_cmp] * pg), wbd, pe8, wb)


def _query_blockdiag(q_ref):
    qt = jnp.transpose(q_ref[...] * ATTN_SCALE)
    tiled = jnp.concatenate([qt] * KV_GROUPS, axis=0)
    rowg = lax.broadcasted_iota(jnp.int32, (KV_WIDTH, 1), 0) // HEAD_DIM
    colg = lax.broadcasted_iota(jnp.int32, (1, LANES), 1) // HEADS_PER_GROUP
    return jnp.where(rowg == colg, tiled, 0.0).astype(BF16)


def _diag_heads(o):
    rowg = lax.broadcasted_iota(jnp.int32, (N_HEADS, 1), 0) // HEADS_PER_GROUP
    out = jnp.zeros((N_HEADS, HEAD_DIM), F32)
    for g in range(KV_GROUPS):
        out = out + jnp.where(rowg == g, o[:N_HEADS, g * HEAD_DIM:(g + 1) * HEAD_DIM], 0.0)
    return out


def _smp_cmp_kernel(q_ref, kc_ref, vc_ref, esum_ref, gsum_ref, oc_ref, imp_ref, *, qpos):
    qbd = _query_blockdiag(q_ref)
    nr = kc_ref.shape[0]
    s = jnp.dot(kc_ref[...], qbd, preferred_element_type=F32)
    row = lax.broadcasted_iota(jnp.int32, (nr, 1), 0)
    mf = jnp.where(row >= 1, jnp.where((row - 1) * CMP_STRIDE + CMP_BLOCK - 1 <= qpos, 1.0, 0.0), 0.0)
    s = jnp.where(mf > 0.5, s, NEG_INF)
    p = jnp.exp(s - jnp.max(s, axis=0, keepdims=True)) * mf
    p = p / jnp.maximum(jnp.sum(p, axis=0, keepdims=True), 1e-30)
    o = jnp.dot(jnp.transpose(p).astype(BF16), vc_ref[...], preferred_element_type=F32)
    oc_ref[...] = _diag_heads(o)
    impg = _dot_exact_right(p, gsum_ref[...])
    imp_ref[...] = _dot_exact_left(esum_ref[...], impg)


def _smp_cmp(q_pad, kc, vc, n_blk_pad, qpos):
    db, nr, _ = kc.shape
    n_blk = (qpos + SLC_BLOCK) // SLC_BLOCK
    esum = _esum_matrix(n_blk_pad, nr, 1) * jnp.asarray(np.arange(n_blk_pad)[:, None] < n_blk, BF16)
    gs = (np.arange(LANES)[:, None] // HEADS_PER_GROUP == np.arange(LANES)[None, :]) & (
        np.arange(LANES)[:, None] < N_HEADS)
    gsum = jnp.asarray(gs, BF16)
    per_b = lambda shape: pl.BlockSpec((None,) + shape, lambda bi: (bi, 0, 0))
    const = lambda shape: pl.BlockSpec(shape, lambda bi: (0,) * len(shape))
    return pl.pallas_call(
        functools.partial(_smp_cmp_kernel, qpos=qpos),
        grid=(db,),
        in_specs=[per_b((LANES, HEAD_DIM)), per_b((nr, KV_WIDTH)), per_b((nr, KV_WIDTH)),
                  const((n_blk_pad, nr)), const((LANES, LANES))],
        out_specs=[per_b((N_HEADS, HEAD_DIM)), per_b((n_blk_pad, LANES))],
        out_shape=[jax.ShapeDtypeStruct((db, N_HEADS, HEAD_DIM), F32),
                   jax.ShapeDtypeStruct((db, n_blk_pad, LANES), F32)],
        compiler_params=_cparams(("arbitrary",)),
        name="smp_cmp",
    )(q_pad, kc, vc, esum, gsum)


def _smp_topk_kernel(imp_ref, idx_ref, *, n_blk, n_sel):
    nbp = imp_ref.shape[0]
    blk = lax.broadcasted_iota(jnp.int32, (nbp, 1), 0)
    cur = n_blk - 1
    valid = blk <= cur
    forced = (blk == 0) | (blk == cur) | (blk == cur - 1)
    score = jnp.where(valid, imp_ref[...] + jnp.where(forced, FORCE_BONUS, 0.0), NEG_INF)
    rank = _rank(score, nbp)
    blkf = blk.astype(F32)
    for k in range(n_sel):
        pick = jnp.sum(jnp.where(rank == float(k), blkf, 0.0), axis=0, keepdims=True)
        idx_ref[k:k + 1, :] = pick.astype(jnp.int32)


def _smp_topk(score_t, n_blk, n_sel):
    nbp, w = score_t.shape
    full = lambda shape: pl.BlockSpec(shape, lambda: (0,) * len(shape))
    return pl.pallas_call(
        functools.partial(_smp_topk_kernel, n_blk=n_blk, n_sel=n_sel),
        in_specs=[full((nbp, w))],
        out_specs=full((n_sel, w)),
        out_shape=jax.ShapeDtypeStruct((n_sel, w), jnp.int32),
        name="smp_topk",
    )(score_t)


def _smp_attn_kernel(page_ref, half_ref, isnew_ref, *refs, n_slots, n_sel, past, wbuf):
    blk_refs = refs[:n_slots]
    (q_ref, ksn_ref, win_ref, kwn_ref, oc_ref, gt_ref, go_ref, o_ref, wout_ref) = refs[n_slots:]
    b = pl.program_id(0)
    lane_half = lax.broadcasted_iota(jnp.int32, (1, PAGE_SIZE), 1) // SLC_BLOCK
    row8 = lax.broadcasted_iota(jnp.int32, (SUBLANES, 1), 0)
    wlane = lax.broadcasted_iota(jnp.int32, (1, wbuf), 1)
    kposw = past - wbuf + wlane
    wbias = jnp.where((kposw > past - WINDOW) & (kposw >= 0), 0.0, NEG_INF)
    nt = (((1,), (1,)), ((), ()))

    def bf(x):
        return x.astype(BF16).astype(F32)

    ys = []
    for g in range(KV_GROUPS):
        q = q_ref[g] * ATTN_SCALE
        qb = q.astype(BF16)
        kn = bf(ksn_ref[:, g * HEAD_DIM:(g + 1) * HEAD_DIM])
        vn = bf(ksn_ref[:, KV_WIDTH + g * HEAD_DIM:KV_WIDTH + (g + 1) * HEAD_DIM])
        kwn = bf(kwn_ref[:, g * HEAD_DIM:(g + 1) * HEAD_DIM])
        vwn = bf(kwn_ref[:, KV_WIDTH + g * HEAD_DIM:KV_WIDTH + (g + 1) * HEAD_DIM])

        slots = range(g * n_sel, (g + 1) * n_sel)
        kt = jnp.concatenate([blk_refs[k][0] for k in slots], axis=1).astype(BF16)
        vt = jnp.concatenate([blk_refs[k][1] for k in slots], axis=1).astype(BF16)
        keeps = [jnp.where(isnew_ref[b, k] == 0, 1.0, 0.0) for k in slots]
        mf = jnp.concatenate([jnp.where(lane_half == half_ref[b, k], keep, 0.0)
                              for k, keep in zip(slots, keeps)], axis=1)
        has_new = 1.0 - functools.reduce(jnp.minimum, keeps)
        s = jnp.where(mf > 0.5, jnp.dot(qb, kt, preferred_element_type=F32), NEG_INF)
        s_n = jnp.where(has_new > 0.5, jnp.sum(bf(q) * kn, axis=1, keepdims=True), NEG_INF)
        m = jnp.maximum(jnp.max(s, axis=1, keepdims=True), s_n)
        p = jnp.exp(s - m) * mf
        p_n = jnp.exp(s_n - m) * has_new
        l = jnp.sum(p, axis=1, keepdims=True) + p_n
        o_s = (lax.dot_general(p.astype(BF16), vt, nt, preferred_element_type=F32) + bf(p_n) * vn)
        o_s = o_s / jnp.maximum(l, 1e-30)

        s = jnp.dot(qb, win_ref[0, g].astype(BF16), preferred_element_type=F32) + wbias
        s_n = jnp.sum(bf(q) * kwn, axis=1, keepdims=True)
        m = jnp.maximum(jnp.max(s, axis=1, keepdims=True), s_n)
        p = jnp.exp(s - m)
        p_n = jnp.exp(s_n - m)
        l = jnp.sum(p, axis=1, keepdims=True) + p_n
        o_w = (lax.dot_general(p.astype(BF16), win_ref[1, g].astype(BF16), nt, preferred_element_type=F32)
               + bf(p_n) * vwn)
        o_w = o_w / l

        gates = jax.nn.sigmoid(gt_ref[g])
        ys.append(gates[:, 0:1] * oc_ref[g] + gates[:, 1:2] * o_s + gates[:, 2:3] * o_w)

    real = row8 < HEADS_PER_GROUP
    ssq = functools.reduce(lambda a, c: a + c, [jnp.sum(jnp.where(real, y * y, 0.0), axis=1, keepdims=True)
                                                 for y in ys])
    scale = lax.rsqrt(jnp.sum(ssq, axis=0, keepdims=True) / ATT_WIDTH + EPS)
    for g in range(KV_GROUPS):
        o_ref[g] = (ys[g] * scale * go_ref[g]).astype(BF16)

    new_col = jnp.transpose(jnp.broadcast_to(kwn_ref[...], (SUBLANES, 2 * KV_WIDTH)))[:, 0:1]
    for s2 in range(2):
        for g in range(KV_GROUPS):
            c0 = (s2 * KV_GROUPS + g) * HEAD_DIM
            wout_ref[s2, g] = jnp.where(wlane == wbuf - 1, new_col[c0:c0 + HEAD_DIM],
                                        pltpu.roll(win_ref[s2, g], wbuf - 1, 1))


def _smp_attn(page, half, isnew, cache_slc, q_g, kvs_new, cache_win, kvw_new, o_c, gates, g_out, past):
    db, n_slots = page.shape
    n_sel = n_slots // KV_GROUPS
    wbuf = cache_win.shape[-1]
    width = 2 * KV_WIDTH
    blk_specs = [pl.BlockSpec((None, 2, None, HEAD_DIM, PAGE_SIZE), functools.partial(
        lambda bi, pg, hf, nw, k: (pg[bi, k], 0, k // n_sel, 0, 0), k=k)) for k in range(n_slots)]
    per_b3 = lambda shape: pl.BlockSpec((None,) + shape, lambda bi, pg, hf, nw: (bi, 0, 0))
    per_b4 = lambda shape: pl.BlockSpec((None,) + shape, lambda bi, pg, hf, nw: (bi, 0, 0, 0))
    per_b5 = lambda shape: pl.BlockSpec((None,) + shape, lambda bi, pg, hf, nw: (bi, 0, 0, 0, 0))
    win_shape = (2, KV_GROUPS, HEAD_DIM, wbuf)
    head_shape = (KV_GROUPS, SUBLANES, HEAD_DIM)
    return pl.pallas_call(
        functools.partial(_smp_attn_kernel, n_slots=n_slots, n_sel=n_sel, past=past, wbuf=wbuf),
        grid_spec=pltpu.PrefetchScalarGridSpec(
            num_scalar_prefetch=3,
            grid=(db,),
            in_specs=blk_specs + [per_b4(head_shape), per_b3((1, width)), per_b5(win_shape), per_b3((1, width)),
                                  per_b4(head_shape), per_b4((KV_GROUPS, SUBLANES, 3)),
                                  pl.BlockSpec(head_shape, lambda bi, pg, hf, nw: (0, 0, 0))],
            out_specs=[per_b4(head_shape), per_b5(win_shape)]),
        out_shape=[jax.ShapeDtypeStruct((db,) + head_shape, BF16),
                   jax.ShapeDtypeStruct((db,) + win_shape, F32)],
        compiler_params=_cparams(("arbitrary",)),
        name="smp_attn",
    )(page, half, isnew, *([cache_slc] * n_slots), q_g, kvs_new.reshape(db, 1, width), cache_win,
      kvw_new.reshape(db, 1, width), o_c, gates, g_out)


def _nsa_sample(za, gates_raw, kvs_new, kvw_new, cache_cmp, cache_slc, cache_win, page_table, wbd, pe8, wb,
                g_out):
    db, n_pages = page_table.shape
    past = n_pages * PAGE_SIZE
    n_blk = (past + SLC_BLOCK) // SLC_BLOCK
    n_sel = min(N_SEL, n_blk)
    n_blk_pad = -(-n_blk // SUBLANES) * SUBLANES
    q = za[:, 2 * SSM_WIDTH:].reshape(db, N_HEADS, HEAD_DIM)
    q_pad = jnp.pad(q, ((0, 0), (0, LANES - N_HEADS), (0, 0)))
    kc, vc = _cmp_pages(cache_cmp, page_table, wbd, pe8, wb)
    o_c, imp = _smp_cmp(q_pad, kc, vc, n_blk_pad, past)
    score_t = jnp.transpose(imp[:, :, :KV_GROUPS], (1, 0, 2)).reshape(n_blk_pad, db * KV_GROUPS)
    lane_pad = -(-db * KV_GROUPS // LANES) * LANES
    score_t = jnp.pad(score_t, ((0, 0), (0, lane_pad - db * KV_GROUPS)))
    idx = _smp_topk(score_t, n_blk, n_sel)[:, :db * KV_GROUPS]
    idx = jnp.transpose(idx.reshape(n_sel, db, KV_GROUPS), (1, 2, 0))
    n_past_blk = past // SLC_BLOCK
    per_page = PAGE_SIZE // SLC_BLOCK
    jp = jnp.minimum(idx, n_past_blk - 1).reshape(db, KV_GROUPS * n_sel)
    page = jnp.take_along_axis(page_table, jp // per_page, axis=1).astype(jnp.int32)
    half = (jp % per_page).astype(jnp.int32)
    isnew = (idx >= n_past_blk).reshape(db, KV_GROUPS * n_sel).astype(jnp.int32)
    pad_heads = lambda a: jnp.pad(a.reshape(a.shape[0], KV_GROUPS, HEADS_PER_GROUP, a.shape[-1]),
                                  ((0, 0), (0, 0), (0, SUBLANES - HEADS_PER_GROUP), (0, 0)))
    gates = pad_heads(gates_raw[:, :3 * N_HEADS].reshape(db, N_HEADS, 3))
    g_out_g = pad_heads(g_out.reshape(1, N_HEADS, HEAD_DIM))[0]
    y, win_new = _smp_attn(page, half, isnew, cache_slc, pad_heads(q), kvs_new, cache_win, kvw_new,
                           pad_heads(o_c), gates, g_out_g, past)
    return y[:, :, :HEADS_PER_GROUP].reshape(db, ATT_WIDTH), win_new


def _outproj_kernel(x_ref, ms_ref, ma_ref, w_ref, gt_ref, sc_ref, sh_ref, g_ref, x1_ref, h2_ref):
    mixed = jnp.concatenate([ms_ref[...], ma_ref[...]], axis=1)
    x1 = x_ref[...] + gt_ref[...] * jnp.dot(mixed, w_ref[...], preferred_element_type=F32)
    x1_ref[...] = x1
    h2_ref[...] = (_rms(x1, g_ref[...]) * (1.0 + sc_ref[...]) + sh_ref[...]).astype(BF16)


def _outproj(x, m_ssm, m_att, w_out, gt1, sc2, sh2, g2, tm, rows_per_mod):
    n, d = x.shape
    r = gt1.shape[1]
    tpm = rows_per_mod // tm
    mod_spec = pl.BlockSpec((None, r, d), lambda i: (i // tpm, 0, 0))
    row = lambda w: pl.BlockSpec((tm, w), lambda i: (i, 0))
    return pl.pallas_call(
        _outproj_kernel,
        grid=(n // tm,),
        in_specs=[row(d), row(SSM_WIDTH), row(ATT_WIDTH), pl.BlockSpec((d, d), lambda i: (0, 0)),
                  mod_spec, mod_spec, mod_spec, pl.BlockSpec((1, d), lambda i: (0, 0))],
        out_specs=[row(d), row(d)],
        out_shape=[jax.ShapeDtypeStruct((n, d), F32), jax.ShapeDtypeStruct((n, d), BF16)],
        compiler_params=_cparams(("arbitrary",)),
        name="outproj",
    )(x, m_ssm, m_att, w_out, gt1, sc2, sh2, g2.reshape(1, d))


def _ffn_act(a_v, a_g, a1_v, a1_g, a2_v, a2_g, cwv_ref, cwg_ref, cbv_ref, cbg_ref):
    val = cbv_ref[...] + cwv_ref[2:3, :] * a_v + cwv_ref[0:1, :] * a2_v + cwv_ref[1:2, :] * a1_v
    gate = cbg_ref[...] + cwg_ref[2:3, :] * a_g + cwg_ref[0:1, :] * a2_g + cwg_ref[1:2, :] * a1_g
    return (gate * jax.nn.sigmoid(gate) * val).astype(BF16)


def _ffn_finish(j, contrib, x1_ref, gt_ref, gf_ref, y_ref):
    @pl.when(j == 0)
    def _():
        y_ref[...] = contrib

    @pl.when(j > 0)
    def _():
        y_ref[...] += contrib

    @pl.when(j == FFN_NF - 1)
    def _():
        y_ref[...] = _rms(x1_ref[...] + gt_ref[...] * y_ref[...], gf_ref[...])


def _ffn_seq_kernel(h_ref, x1_ref, gt_ref, wv_ref, wg_ref, cwv_ref, cwg_ref, cbv_ref, cbg_ref, wd_ref, gf_ref,
                    y_ref, tv_ref, tg_ref, sv_ref, sg_ref, hv_ref, hg_ref, *, tm, tpb, rs):
    i = pl.program_id(0)
    j = pl.program_id(1)

    @pl.when(i % tpb == 0)
    def _():
        hv_ref[j] = jnp.zeros((SUBLANES, FFN_TF), F32)
        hg_ref[j] = jnp.zeros((SUBLANES, FFN_TF), F32)

    @pl.when(j == 0)
    def _():
        y_ref[...] = jnp.zeros(y_ref.shape, F32)

    sv_ref[0:SUBLANES, :] = hv_ref[j]
    sg_ref[0:SUBLANES, :] = hg_ref[j]

    def up(k):
        hk = h_ref[k * rs:(k + 1) * rs, :]
        o = SUBLANES + k * rs
        sv_ref[o:o + rs, :] = jnp.dot(hk, wv_ref[...], preferred_element_type=F32)
        sg_ref[o:o + rs, :] = jnp.dot(hk, wg_ref[...], preferred_element_type=F32)

    def down(k):
        o = SUBLANES + k * rs
        act = _ffn_act(sv_ref[o:o + rs, :], sg_ref[o:o + rs, :], sv_ref[o - 1:o - 1 + rs, :],
                       sg_ref[o - 1:o - 1 + rs, :], sv_ref[o - 2:o - 2 + rs, :], sg_ref[o - 2:o - 2 + rs, :],
                       cwv_ref, cwg_ref, cbv_ref, cbg_ref)
        y_ref[k * rs:(k + 1) * rs, :] += jnp.dot(act, wd_ref[...], preferred_element_type=F32)

    up(0)
    for k in range(tm // rs):
        if k + 1 < tm // rs:
            up(k + 1)
        down(k)

    for s_ref, halo_ref, t_ref in ((sv_ref, hv_ref, tv_ref), (sg_ref, hg_ref, tg_ref)):
        halo_ref[j] = s_ref[tm:tm + SUBLANES, :]
        t_ref[...] = s_ref[tm + SUBLANES - 2:tm + SUBLANES, :]

    @pl.when(j == FFN_NF - 1)
    def _():
        y_ref[...] = _rms(x1_ref[...] + gt_ref[...] * y_ref[...], gf_ref[...])


def _ffn_step_kernel(h_ref, x1_ref, gt_ref, wv_ref, wg_ref, cwv_ref, cwg_ref, cbv_ref, cbg_ref, wd_ref, gf_ref,
                     p0v_ref, p0g_ref, p1v_ref, p1g_ref, y_ref, av_ref, ag_ref):
    j = pl.program_id(1)
    h = h_ref[...]
    a_v = jnp.dot(h, wv_ref[...], preferred_element_type=F32)
    a_g = jnp.dot(h, wg_ref[...], preferred_element_type=F32)
    av_ref[...] = a_v
    ag_ref[...] = a_g
    act = _ffn_act(a_v, a_g, p1v_ref[...], p1g_ref[...], p0v_ref[...], p0g_ref[...],
                   cwv_ref, cwg_ref, cbv_ref, cbg_ref)
    _ffn_finish(j, jnp.dot(act, wd_ref[...], preferred_element_type=F32), x1_ref, gt_ref, gf_ref, y_ref)


def _ffn_specs(d, tm, r, tpm):
    row_once = pl.BlockSpec((tm, d), lambda i, j: (i, 0), pipeline_mode=pl.Buffered(1))
    return [row_once, row_once, pl.BlockSpec((None, r, d), lambda i, j: (i // tpm, 0, 0)),
            pl.BlockSpec((d, FFN_TF), lambda i, j: (0, j)), pl.BlockSpec((d, FFN_TF), lambda i, j: (0, j + FFN_NF)),
            pl.BlockSpec((CONV_W, FFN_TF), lambda i, j: (0, j)),
            pl.BlockSpec((CONV_W, FFN_TF), lambda i, j: (0, j + FFN_NF)),
            pl.BlockSpec((1, FFN_TF), lambda i, j: (0, j)), pl.BlockSpec((1, FFN_TF), lambda i, j: (0, j + FFN_NF)),
            pl.BlockSpec((FFN_TF, d), lambda i, j: (j, 0)), pl.BlockSpec((1, d), lambda i, j: (0, 0))]


def _ffn_seq(h2, x1, gt2, w_up, conv_w, conv_b, w_down, g_final, tm, rows_per_mod):
    n, d = x1.shape
    tpb = rows_per_mod // tm
    nt = n // tm
    cb = conv_b.reshape(1, -1)
    y, tv, tg = pl.pallas_call(
        functools.partial(_ffn_seq_kernel, tm=tm, tpb=tpb, rs=min(FFN_RS, tm)),
        grid=(nt, FFN_NF),
        in_specs=_ffn_specs(d, tm, gt2.shape[1], tpb),
        out_specs=[pl.BlockSpec((tm, d), lambda i, j: (i, 0), pipeline_mode=pl.Buffered(1)),
                   pl.BlockSpec((None, CONV_W - 1, FFN_TF), lambda i, j: (i, 0, j)),
                   pl.BlockSpec((None, CONV_W - 1, FFN_TF), lambda i, j: (i, 0, j))],
        out_shape=[jax.ShapeDtypeStruct((n, d), F32),
                   jax.ShapeDtypeStruct((nt, CONV_W - 1, D_FF), F32),
                   jax.ShapeDtypeStruct((nt, CONV_W - 1, D_FF), F32)],
        scratch_shapes=[pltpu.VMEM((tm + SUBLANES, FFN_TF), F32), pltpu.VMEM((tm + SUBLANES, FFN_TF), F32),
                        pltpu.VMEM((FFN_NF, SUBLANES, FFN_TF), F32), pltpu.VMEM((FFN_NF, SUBLANES, FFN_TF), F32)],
        compiler_params=_cparams(("arbitrary", "arbitrary")),
        name="ffn_seq",
    )(h2, x1, gt2, w_up, w_up, conv_w, conv_w, cb, cb, w_down, g_final.reshape(1, d))
    tails = jnp.concatenate([tv, tg], axis=-1)
    return y, tails[tpb - 1::tpb]


def _ffn_step(h2, x1, gt2, w_up, conv_w, conv_b, w_down, g_final, conv_prev):
    n, d = x1.shape
    cb = conv_b.reshape(1, -1)
    prev_v = pl.BlockSpec((n, FFN_TF), lambda i, j: (0, j))
    prev_g = pl.BlockSpec((n, FFN_TF), lambda i, j: (0, j + FFN_NF))
    p0, p1 = conv_prev[:, 0], conv_prev[:, 1]
    y, a_v, a_g = pl.pallas_call(
        _ffn_step_kernel,
        grid=(1, FFN_NF),
        in_specs=_ffn_specs(d, n, gt2.shape[1], 1) + [prev_v, prev_g, prev_v, prev_g],
        out_specs=[pl.BlockSpec((n, d), lambda i, j: (0, 0)),
                   pl.BlockSpec((n, FFN_TF), lambda i, j: (0, j)), pl.BlockSpec((n, FFN_TF), lambda i, j: (0, j))],
        out_shape=[jax.ShapeDtypeStruct((n, d), F32),
                   jax.ShapeDtypeStruct((n, D_FF), F32), jax.ShapeDtypeStruct((n, D_FF), F32)],
        compiler_params=_cparams(("arbitrary", "arbitrary")),
        name="ffn_step",
    )(h2, x1, gt2, w_up, w_up, conv_w, conv_w, cb, cb, w_down, g_final.reshape(1, d), p0, p0, p1, p1)
    return y, jnp.stack([p1, jnp.concatenate([a_v, a_g], axis=-1)], axis=1)


def kernel(x_prompt, x_sample, cache_cmp_kv, cache_slc_kv, cache_win_kv, state_ssm_re, state_ssm_im, state_conv,
           page_table, c_prompt, c_sample, w_ada, b_ada, g_norm1, w_in, ssm_lam_re, ssm_lam_im, ssm_log_dt,
           ssm_b_re, ssm_b_im, ssm_c_re, ssm_c_im, ssm_d, w_cmp, pe_cmp, g_out_ssm, g_out_att, w_out, g_norm2,
           w_up, conv_w, conv_b, w_down, g_final):
    depth = w_in.shape[0]
    b, s, d = x_prompt.shape
    db, ds, _ = x_sample.shape
    assert depth == 1 and ds == 1 and d == D_MODEL, "kernel is written for one layer and one new token per sequence"
    assert s % 512 == 0
    tm = 512
    tm_in = 256
    tm_ffn = min(1024, s)
    gp = SSM_GROUPS * SSM_STATE
    kv_shape = (2, KV_GROUPS, HEAD_DIM)
    l = 0

    w_in_p = jnp.pad(w_in[l], ((0, 0), (0, IN_PAD - IN_WIDTH))).astype(BF16)
    w_out_b = w_out[l].astype(BF16)
    w_up_b = w_up[l].astype(BF16)
    w_down_b = w_down[l].astype(BF16)
    pwr, pwi, bbr, bbi = _s5_prep(ssm_lam_re[l], ssm_lam_im[l], ssm_log_dt[l], ssm_b_re[l], ssm_b_im[l])
    w1, w2, lvl_r, lvl_i, pw_r, pw_i = _s5_weights(pwr, pwi, bbr, bbi, ssm_c_re[l], ssm_c_im[l])
    wbd, pe8, wb = _cmp_weights(w_cmp[l], pe_cmp[l])

    n_c = b + db
    n_c_pad = -(-n_c // SUBLANES) * SUBLANES
    c_all = jnp.pad(jnp.concatenate([c_prompt, c_sample], axis=0), ((0, n_c_pad - n_c), (0, 0)))
    mod = _ada(c_all, w_ada[l], b_ada[l]).reshape(n_c_pad, 6, d)
    mod_p = [mod[:b, k].reshape(b, 1, d) for k in range(6)]
    mod_s = [mod[b:n_c, k].reshape(1, db, d) for k in range(6)]

    xp = x_prompt.reshape(b * s, d)
    za, kvc, kvs, kvw, graw = _inproj(xp, mod_p[1], mod_p[0], g_norm1[l], w_in_p, tm_in, s)
    m_ssm, st_re, st_im = _s5_prompt(za, b, s, w1, w2, lvl_r, lvl_i, pw_r, pw_i, ssm_d[l], g_out_ssm[l], 256)
    kc, vct = _cmpproj(kvc, b, s, wbd, pe8, wb)
    ks_b, vst, kw_b, vwt = _kvprep(kvs, kvw, b, s, 512)
    m_att = _nsa_prompt(za, graw, ks_b, vst, kw_b, vwt, kc, vct, g_out_att[l], b, s)
    x1, h2 = _outproj(xp, m_ssm, m_att, w_out_b, mod_p[2], mod_p[4], mod_p[3], g_norm2[l], tm, s)
    y_p, conv_p = _ffn_seq(h2, x1, mod_p[5], w_up_b, conv_w[l], conv_b[l], w_down_b, g_final, tm_ffn, s)
    wlen = min(WINDOW, s)
    win_p = kvw.reshape(b, s, *kv_shape)[:, s - wlen:]

    xs = x_sample.reshape(db, d)
    za_s, kvc_s, kvs_s, kvw_s, graw_s = _inproj(xs, mod_s[1], mod_s[0], g_norm1[l], w_in_p, db, db)
    m_ssm_s, st_re_s, st_im_s = _s5_sample(za_s, state_ssm_re[l].reshape(db, gp), state_ssm_im[l].reshape(db, gp),
                                           w1, w2, pw_r, pw_i, ssm_d[l], g_out_ssm[l])
    rows_minor = lambda c: jnp.transpose(c, (0, 2, 3, 4, 1))
    m_att_s, win_s = _nsa_sample(za_s, graw_s, kvs_s, kvw_s, rows_minor(cache_cmp_kv[l]),
                                 rows_minor(cache_slc_kv[l]), rows_minor(cache_win_kv[l]),
                                 page_table, wbd, pe8, wb, g_out_att[l])
    win_s = jnp.transpose(win_s, (0, 4, 1, 2, 3))
    x1_s, h2_s = _outproj(xs, m_ssm_s, m_att_s, w_out_b, mod_s[2], mod_s[4], mod_s[3], g_norm2[l], db, db)
    y_s, conv_s = _ffn_step(h2_s, x1_s, mod_s[5], w_up_b, conv_w[l], conv_b[l], w_down_b, g_final, state_conv[l])

    wbuf = cache_win_kv.shape[2]
    return (y_p.reshape(b, s, d), y_s.reshape(db, 1, d),
            kvc.reshape(1, b, s, *kv_shape), kvc_s.reshape(1, db, 1, *kv_shape),
            kvs.reshape(1, b, s, *kv_shape), kvs_s.reshape(1, db, 1, *kv_shape),
            win_p[None], win_s.reshape(1, db, wbuf, *kv_shape),
            st_re.reshape(1, b, SSM_GROUPS, SSM_STATE), st_im.reshape(1, b, SSM_GROUPS, SSM_STATE),
            st_re_s.reshape(1, db, SSM_GROUPS, SSM_STATE), st_im_s.reshape(1, db, SSM_GROUPS, SSM_STATE),
            conv_p[None], conv_s[None])
```

```python
import functools
import math

import jax
import jax.numpy as jnp
import numpy as np
from jax import lax
from jax.experimental import pallas as pl
from jax.experimental.pallas import tpu as pltpu

F32 = jnp.float32
BF16 = jnp.bfloat16

D_MODEL = 2048
SSM_WIDTH = D_MODEL // 2
ATT_WIDTH = D_MODEL - SSM_WIDTH
SSM_CH = 16
SSM_GROUPS = SSM_WIDTH // SSM_CH
SSM_STATE = 64
HEAD_DIM = 64
N_HEADS = ATT_WIDTH // HEAD_DIM
KV_GROUPS = 4
HEADS_PER_GROUP = N_HEADS // KV_GROUPS
KV_WIDTH = KV_GROUPS * HEAD_DIM
CMP_STRIDE = 16
CMP_BLOCK = 32
SLC_BLOCK = 64
N_SEL = 16
WINDOW = 512
PAGE_SIZE = 128
ATTN_SCALE = HEAD_DIM ** -0.5
NEG_INF = -1e30
FORCE_BONUS = 1e4
D_FF = 256 * ((8 * D_MODEL // 3 + 255) // 256)
CONV_W = 3
EPS = 1e-6
IN_WIDTH = 3 * SSM_WIDTH + 6 * KV_WIDTH + 3 * N_HEADS

LANES = 128
SUBLANES = 8
VMEM_LIMIT = 56 * 1024 * 1024

IN_TN = 512
IN_NA = 3 * SSM_WIDTH // IN_TN
IN_PAD = (IN_NA + 4) * IN_TN
GATE_PAD = LANES
SSM_LCH = 8 * SSM_STATE
SSM_NCH = SSM_GROUPS // 8
Q_TILE = 128
SEL_KT = 256
WIN_KT = 128
CMP_PG = 32
FFN_TF = 512
FFN_NF = D_FF // FFN_TF
FFN_RS = 256


def _cparams(sem):
    return pltpu.CompilerParams(dimension_semantics=sem, vmem_limit_bytes=VMEM_LIMIT)


def _rms(x, g):
    return x * lax.rsqrt(jnp.mean(x * x, axis=-1, keepdims=True) + EPS) * g


def _ada_kernel(c_ref, w_ref, b_ref, o_ref):
    c = c_ref[...]
    a = (c * jax.nn.sigmoid(c)).astype(BF16)
    o_ref[...] = jnp.dot(a, w_ref[...].astype(BF16), preferred_element_type=F32) + b_ref[...]


def _ada(c_all, w_ada, b_ada):
    r, d = c_all.shape
    n = w_ada.shape[1]
    tn = 1024
    return pl.pallas_call(
        _ada_kernel,
        grid=(n // tn,),
        in_specs=[pl.BlockSpec((r, d), lambda j: (0, 0)),
                  pl.BlockSpec((d, tn), lambda j: (0, j)),
                  pl.BlockSpec((1, tn), lambda j: (0, j))],
        out_specs=pl.BlockSpec((r, tn), lambda j: (0, j)),
        out_shape=jax.ShapeDtypeStruct((r, n), F32),
        compiler_params=_cparams(("arbitrary",)),
        name="ada",
    )(c_all, w_ada, b_ada.reshape(1, n))


def _inproj_kernel(x_ref, sc_ref, sh_ref, g_ref, w_ref, za_ref, kc_ref, ks_ref, kw_ref, gt_ref):
    h = (_rms(x_ref[...], g_ref[...]) * (1.0 + sc_ref[...]) + sh_ref[...]).astype(BF16)
    for j in range(IN_PAD // IN_TN):
        z = jnp.dot(h, w_ref[:, j * IN_TN:(j + 1) * IN_TN], preferred_element_type=F32)
        if j < IN_NA:
            za_ref[:, j * IN_TN:(j + 1) * IN_TN] = z
        elif j < IN_NA + 3:
            (kc_ref, ks_ref, kw_ref)[j - IN_NA][...] = z
        else:
            gt_ref[...] = z[:, :GATE_PAD]


def _inproj(x, sc, sh, g, w_pad, tm, rows_per_mod):
    n, d = x.shape
    r = sc.shape[1]
    tpm = rows_per_mod // tm
    mod_spec = pl.BlockSpec((None, r, d), lambda i: (i // tpm, 0, 0))
    row = lambda w: pl.BlockSpec((tm, w), lambda i: (i, 0))
    return pl.pallas_call(
        _inproj_kernel,
        grid=(n // tm,),
        in_specs=[row(d), mod_spec, mod_spec, pl.BlockSpec((1, d), lambda i: (0, 0)),
                  pl.BlockSpec((d, IN_PAD), lambda i: (0, 0), pipeline_mode=pl.Buffered(1))],
        out_specs=[row(3 * SSM_WIDTH), row(2 * KV_WIDTH), row(2 * KV_WIDTH), row(2 * KV_WIDTH), row(GATE_PAD)],
        out_shape=[jax.ShapeDtypeStruct((n, 3 * SSM_WIDTH), F32),
                   jax.ShapeDtypeStruct((n, 2 * KV_WIDTH), F32),
                   jax.ShapeDtypeStruct((n, 2 * KV_WIDTH), F32),
                   jax.ShapeDtypeStruct((n, 2 * KV_WIDTH), F32),
                   jax.ShapeDtypeStruct((n, GATE_PAD), F32)],
        compiler_params=_cparams(("arbitrary",)),
        name="inproj",
    )(x, sc, sh, g.reshape(1, d), w_pad)


def _s5_prep_kernel(lre_ref, lim_ref, ldt_ref, lrex_ref, limx_ref, bre_ref, bim_ref,
                    pwr_ref, pwi_ref, bbr_ref, bbi_ref):
    dt = jnp.exp(ldt_ref[...])

    def disc(lre, lim):
        mag = jnp.exp(lre * dt)
        ab_re = mag * jnp.cos(lim * dt)
        ab_im = mag * jnp.sin(lim * dt)
        den = lre * lre + lim * lim
        f_re = ((ab_re - 1.0) * lre + ab_im * lim) / den
        f_im = (ab_im * lre - (ab_re - 1.0) * lim) / den
        return ab_re, ab_im, f_re, f_im

    ab_re, ab_im, _, _ = disc(lre_ref[...], lim_ref[...])
    pr, pi = ab_re, ab_im
    pwr_ref[0] = pr
    pwi_ref[0] = pi
    for k in range(1, SUBLANES):
        pr, pi = pr * ab_re - pi * ab_im, pr * ab_im + pi * ab_re
        pwr_ref[k] = pr
        pwi_ref[k] = pi
    _, _, f_re, f_im = disc(lrex_ref[...], limx_ref[...])
    b_re, b_im = bre_ref[...], bim_ref[...]
    bbr_ref[...] = f_re * b_re - f_im * b_im
    bbi_ref[...] = f_re * b_im + f_im * b_re


def _s5_prep(lam_re, lam_im, log_dt, b_re, b_im):
    g, p = lam_re.shape
    ch = b_re.shape[-1]
    lrex = jnp.repeat(lam_re, ch, axis=1)
    limx = jnp.repeat(lam_im, ch, axis=1)
    full = lambda shape: pl.BlockSpec(shape, lambda: (0,) * len(shape))
    return pl.pallas_call(
        _s5_prep_kernel,
        in_specs=[full((g, p)), full((g, p)), full((g, 1)), full((g, p * ch)), full((g, p * ch)),
                  full((g, p * ch)), full((g, p * ch))],
        out_specs=[full((SUBLANES, g, p)), full((SUBLANES, g, p)), full((g, p * ch)), full((g, p * ch))],
        out_shape=[jax.ShapeDtypeStruct((SUBLANES, g, p), F32), jax.ShapeDtypeStruct((SUBLANES, g, p), F32),
                   jax.ShapeDtypeStruct((g, p * ch), F32), jax.ShapeDtypeStruct((g, p * ch), F32)],
        name="s5_prep",
    )(lam_re, lam_im, log_dt.reshape(g, 1), lrex, limx, b_re.reshape(g, p * ch), b_im.reshape(g, p * ch))


def _s5_weights(pwr, pwi, bbr, bbi, c_re, c_im):
    g, p, ch = SSM_GROUPS, SSM_STATE, SSM_CH
    eye = jnp.eye(8, dtype=F32)

    def w_in(bb):
        bb = bb.reshape(SSM_NCH, 8, p, ch)
        return jnp.einsum('jgpc,gh->jgchp', bb, eye).reshape(SSM_NCH, 8 * ch, 8 * p)

    def w_out(c):
        c = c.reshape(SSM_NCH, 8, ch, p)
        return jnp.einsum('jgcp,gh->jgphc', c, eye).reshape(SSM_NCH, 8 * p, 8 * ch)

    w1 = jnp.concatenate([w_in(bbr), w_in(bbi)], axis=-1).astype(BF16)
    w2 = jnp.concatenate([w_out(c_re), w_out(-c_im)], axis=1).astype(BF16)
    pw_r = pwr.reshape(SUBLANES, g * p)
    pw_i = pwi.reshape(SUBLANES, g * p)
    tau = jnp.arange(SUBLANES)[:, None]
    lvl_r = jnp.stack([jnp.where(tau >= d, pw_r[d - 1][None, :], 0.0) for d in (1, 2, 4)])
    lvl_i = jnp.stack([jnp.where(tau >= d, pw_i[d - 1][None, :], 0.0) for d in (1, 2, 4)])
    return w1, w2, lvl_r, lvl_i, pw_r, pw_i


def _s5_post(y, u, g_glu, d_skip, g_out):
    y = y + d_skip * u
    y = jax.nn.gelu(y) * jax.nn.sigmoid(g_glu)
    return _rms(y, g_out).astype(BF16)


def _s5_scan_kernel(u_ref, gg_ref, w1_ref, w2_ref, lr_ref, li_ref, pr_ref, pi_ref, d_ref, go_ref,
                    o_ref, sr_ref, si_ref, br_ref, bi_ref, y_ref, hr_ref, hi_ref, *, tt):
    t = pl.program_id(1)
    nrt = tt // SUBLANES

    @pl.when(t == 0)
    def _():
        hr_ref[...] = jnp.zeros_like(hr_ref)
        hi_ref[...] = jnp.zeros_like(hi_ref)

    def chunk(j, carry):
        lo = pl.multiple_of(j * SSM_LCH, SSM_LCH)
        uo = pl.multiple_of(j * LANES, LANES)
        ub = u_ref[:, pl.ds(uo, LANES)].astype(BF16)
        bu = jnp.dot(ub, w1_ref[j], preferred_element_type=F32)
        xr = bu[:, :SSM_LCH].reshape(nrt, SUBLANES, SSM_LCH)
        xi = bu[:, SSM_LCH:].reshape(nrt, SUBLANES, SSM_LCH)
        for lvl, d in enumerate((1, 2, 4)):
            ar = lr_ref[lvl, :, pl.ds(lo, SSM_LCH)]
            ai = li_ref[lvl, :, pl.ds(lo, SSM_LCH)]
            zr = pltpu.roll(xr, d, 1)
            zi = pltpu.roll(xi, d, 1)
            xr, xi = xr + ar * zr - ai * zi, xi + ar * zi + ai * zr
        br_ref[...] = xr.reshape(tt, SSM_LCH)
        bi_ref[...] = xi.reshape(tt, SSM_LCH)
        pr = pr_ref[:, pl.ds(lo, SSM_LCH)]
        pi = pi_ref[:, pl.ds(lo, SSM_LCH)]

        def tile(k, h):
            hr, hi = h
            r0 = pl.multiple_of(k * SUBLANES, SUBLANES)
            vr = br_ref[pl.ds(r0, SUBLANES), :] + pr * hr - pi * hi
            vi = bi_ref[pl.ds(r0, SUBLANES), :] + pr * hi + pi * hr
            br_ref[pl.ds(r0, SUBLANES), :] = vr
            bi_ref[pl.ds(r0, SUBLANES), :] = vi
            return vr[SUBLANES - 1:, :], vi[SUBLANES - 1:, :]

        hr, hi = lax.fori_loop(0, nrt, tile, (hr_ref[:, pl.ds(lo, SSM_LCH)], hi_ref[:, pl.ds(lo, SSM_LCH)]))
        hr_ref[:, pl.ds(lo, SSM_LCH)] = hr
        hi_ref[:, pl.ds(lo, SSM_LCH)] = hi
        hcat = jnp.concatenate([br_ref[...], bi_ref[...]], axis=1).astype(BF16)
        y_ref[:, pl.ds(uo, LANES)] = jnp.dot(hcat, w2_ref[j], preferred_element_type=F32)
        return carry

    lax.fori_loop(0, SSM_NCH, chunk, 0)
    o_ref[...] = _s5_post(y_ref[...], u_ref[...], gg_ref[...], d_ref[...], go_ref[...])
    sr_ref[...] = hr_ref[...]
    si_ref[...] = hi_ref[...]


def _s5_prompt(za, b, s, w1, w2, lvl_r, lvl_i, pw_r, pw_i, d_skip, g_out, tt):
    nt = s // tt
    gp = SSM_GROUPS * SSM_STATE
    const2 = lambda shape: pl.BlockSpec(shape, lambda bi, ti: (0,) * len(shape))
    st_spec = pl.BlockSpec((None, 1, gp), lambda bi, ti: (bi, 0, 0))
    return pl.pallas_call(
        functools.partial(_s5_scan_kernel, tt=tt),
        grid=(b, nt),
        in_specs=[pl.BlockSpec((tt, SSM_WIDTH), lambda bi, ti: (bi * nt + ti, 0)),
                  pl.BlockSpec((tt, SSM_WIDTH), lambda bi, ti: (bi * nt + ti, 1)),
                  const2(w1.shape), const2(w2.shape), const2(lvl_r.shape), const2(lvl_i.shape),
                  const2(pw_r.shape), const2(pw_i.shape), const2((1, SSM_WIDTH)), const2((1, SSM_WIDTH))],
        out_specs=[pl.BlockSpec((tt, SSM_WIDTH), lambda bi, ti: (bi * nt + ti, 0)), st_spec, st_spec],
        out_shape=[jax.ShapeDtypeStruct((b * s, SSM_WIDTH), BF16),
                   jax.ShapeDtypeStruct((b, 1, gp), F32), jax.ShapeDtypeStruct((b, 1, gp), F32)],
        scratch_shapes=[pltpu.VMEM((tt, SSM_LCH), F32), pltpu.VMEM((tt, SSM_LCH), F32),
                        pltpu.VMEM((tt, SSM_WIDTH), F32),
                        pltpu.VMEM((1, gp), F32), pltpu.VMEM((1, gp), F32)],
        compiler_params=_cparams(("arbitrary", "arbitrary")),
        name="s5_prompt",
    )(za, za, w1, w2, lvl_r, lvl_i, pw_r, pw_i, d_skip.reshape(1, -1), g_out.reshape(1, -1))


def _s5_step_kernel(u_ref, gg_ref, h0r_ref, h0i_ref, w1_ref, w2_ref, pr_ref, pi_ref, d_ref, go_ref,
                    o_ref, sr_ref, si_ref, y_ref):
    for j in range(SSM_NCH):
        lo, uo = j * SSM_LCH, j * LANES
        bu = jnp.dot(u_ref[:, uo:uo + LANES].astype(BF16), w1_ref[j], preferred_element_type=F32)
        ar = pr_ref[0:1, lo:lo + SSM_LCH]
        ai = pi_ref[0:1, lo:lo + SSM_LCH]
        h0r = h0r_ref[:, lo:lo + SSM_LCH]
        h0i = h0i_ref[:, lo:lo + SSM_LCH]
        hr = bu[:, :SSM_LCH] + (ar * h0r - ai * h0i)
        hi = bu[:, SSM_LCH:] + (ar * h0i + ai * h0r)
        sr_ref[:, lo:lo + SSM_LCH] = hr
        si_ref[:, lo:lo + SSM_LCH] = hi
        hcat = jnp.concatenate([hr, hi], axis=1).astype(BF16)
        y_ref[:, uo:uo + LANES] = jnp.dot(hcat, w2_ref[j], preferred_element_type=F32)
    o_ref[...] = _s5_post(y_ref[...], u_ref[...], gg_ref[...], d_ref[...], go_ref[...])


def _s5_sample(za, h0r, h0i, w1, w2, pw_r, pw_i, d_skip, g_out):
    n = za.shape[0]
    gp = SSM_GROUPS * SSM_STATE
    full = lambda shape: pl.BlockSpec(shape, lambda i: (0,) * len(shape))
    return pl.pallas_call(
        _s5_step_kernel,
        grid=(1,),
        in_specs=[pl.BlockSpec((n, SSM_WIDTH), lambda i: (0, 0)), pl.BlockSpec((n, SSM_WIDTH), lambda i: (0, 1)),
                  full((n, gp)), full((n, gp)), full(w1.shape), full(w2.shape),
                  full(pw_r.shape), full(pw_i.shape), full((1, SSM_WIDTH)), full((1, SSM_WIDTH))],
        out_specs=[full((n, SSM_WIDTH)), full((n, gp)), full((n, gp))],
        out_shape=[jax.ShapeDtypeStruct((n, SSM_WIDTH), BF16),
                   jax.ShapeDtypeStruct((n, gp), F32), jax.ShapeDtypeStruct((n, gp), F32)],
        scratch_shapes=[pltpu.VMEM((n, SSM_WIDTH), F32)],
        compiler_params=_cparams(("arbitrary",)),
        name="s5_sample",
    )(za, za, h0r, h0i, w1, w2, pw_r, pw_i, d_skip.reshape(1, -1), g_out.reshape(1, -1))


def _cmp_weights(w_cmp, pe_cmp):
    eye = jnp.eye(KV_GROUPS, dtype=F32)

    def bd(w):
        return jnp.einsum('sjde,gh->sjgdhe', w, eye).reshape(2, CMP_STRIDE, KV_WIDTH, KV_WIDTH)

    wbd = jnp.concatenate([bd(w_cmp[:, :CMP_STRIDE]), bd(w_cmp[:, CMP_STRIDE:])], axis=-1).astype(BF16)
    pe8 = jnp.zeros((2, SUBLANES, CMP_BLOCK * HEAD_DIM), F32).at[:, 0].set(pe_cmp.reshape(2, -1)).astype(BF16)
    wb = jnp.tile(w_cmp.reshape(2, CMP_BLOCK * HEAD_DIM, HEAD_DIM), (1, 1, KV_GROUPS)).astype(BF16)
    return wbd, pe8, wb


def _split3(x):
    a = x.astype(BF16)
    r = x - a.astype(F32)
    b = r.astype(BF16)
    c = (r - b.astype(F32)).astype(BF16)
    return a, b, c


def _dot_exact_left(e, x):
    a, b, c = _split3(x)
    return (jnp.dot(e, a, preferred_element_type=F32) + jnp.dot(e, b, preferred_element_type=F32)
            + jnp.dot(e, c, preferred_element_type=F32))


def _dot_exact_right(x, e):
    a, b, c = _split3(x)
    return (jnp.dot(a, e, preferred_element_type=F32) + jnp.dot(b, e, preferred_element_type=F32)
            + jnp.dot(c, e, preferred_element_type=F32))


def _rank(score, n_cand):
    n_tiles = score.shape[0] // SUBLANES
    tiles = [score[v * SUBLANES:(v + 1) * SUBLANES] for v in range(n_tiles)]
    ranks = [jnp.zeros(t.shape, F32) for t in tiles]
    row = lax.broadcasted_iota(jnp.int32, (SUBLANES, 1), 0)
    for i in range(n_cand):
        si = tiles[i // SUBLANES][i % SUBLANES:i % SUBLANES + 1]
        for v in range(n_tiles):
            ge = jnp.where(si >= tiles[v], 1.0, 0.0)
            gt = jnp.where(si > tiles[v], 1.0, 0.0)
            if i < v * SUBLANES:
                beats = ge
            elif i >= (v + 1) * SUBLANES:
                beats = gt
            else:
                beats = jnp.where(row > i % SUBLANES, ge, gt)
            ranks[v] = ranks[v] + beats
    return jnp.concatenate(ranks, axis=0)


def _col_reduce(x, op, final):
    slabs = [x[i * SUBLANES:(i + 1) * SUBLANES] for i in range(x.shape[0] // SUBLANES)]
    while len(slabs) > 1:
        pairs = [op(slabs[i], slabs[i + 1]) for i in range(0, len(slabs) - 1, 2)]
        slabs = pairs + ([slabs[-1]] if len(slabs) % 2 else [])
    return final(slabs[0], axis=0, keepdims=True)


def _softmax_stage(g, s, bias, pv_prev, m_ref, l_ref, acc_ref, p_ref, rows):
    if bias is not None:
        s = s + jnp.concatenate([bias] * HEADS_PER_GROUP, axis=1)
    m_old = m_ref[g]
    m_new = jnp.maximum(m_old, _col_reduce(s, jnp.maximum, jnp.max))
    alpha = jnp.exp(m_old - m_new)
    p = jnp.exp(s - m_new)
    l_ref[g] = alpha * l_ref[g] + _col_reduce(p, jnp.add, jnp.sum)
    acc_ref[g] = alpha * (acc_ref[g] + pv_prev)
    m_ref[g] = m_new
    p_ref[g, rows, :] = p.astype(BF16)


def _cmpproj_kernel(x_ref, wbd_ref, pe_ref, wb_ref, kc_ref, vct_ref):
    nch = x_ref.shape[0]
    outs = []
    for s in range(2):
        acc = jnp.zeros((nch, 2 * KV_WIDTH), F32)
        for j in range(CMP_STRIDE):
            c0 = j * 2 * KV_WIDTH + s * KV_WIDTH
            acc = acc + jnp.dot(x_ref[:, c0:c0 + KV_WIDTH].astype(BF16), wbd_ref[s, j],
                                preferred_element_type=F32)
        bias = jnp.dot(pe_ref[s], wb_ref[s], preferred_element_type=F32)[0:1]
        outs.append(acc[:, :KV_WIDTH] + pltpu.roll(acc[:, KV_WIDTH:], nch - 1, 0) + bias)
    kc_ref[...] = outs[0].astype(BF16)
    vct_ref[...] = jnp.transpose(outs[1]).reshape(KV_GROUPS, HEAD_DIM, nch).astype(BF16)


def _cmpproj(kvc, b, s, wbd, pe8, wb):
    nch = s // CMP_STRIDE
    x = kvc.reshape(b, nch, CMP_STRIDE * 2 * KV_WIDTH)
    const = lambda shape: pl.BlockSpec(shape, lambda bi: (0,) * len(shape))
    return pl.pallas_call(
        _cmpproj_kernel,
        grid=(b,),
        in_specs=[pl.BlockSpec((None, nch, CMP_STRIDE * 2 * KV_WIDTH), lambda bi: (bi, 0, 0)),
                  const(wbd.shape), const(pe8.shape), const(wb.shape)],
        out_specs=[pl.BlockSpec((None, nch, KV_WIDTH), lambda bi: (bi, 0, 0)),
                   pl.BlockSpec((None, KV_GROUPS, HEAD_DIM, nch), lambda bi: (bi, 0, 0, 0))],
        out_shape=[jax.ShapeDtypeStruct((b, nch, KV_WIDTH), BF16),
                   jax.ShapeDtypeStruct((b, KV_GROUPS, HEAD_DIM, nch), BF16)],
        compiler_params=_cparams(("arbitrary",)),
        name="cmpproj",
    )(x, wbd, pe8, wb)


def _kvprep_kernel(kc_ref, ks_ref, kw_ref, kct_ref, kst_ref, ksa_ref, vsto_ref, kwo_ref, vwto_ref):
    tk = ks_ref.shape[0]
    kct_ref[...] = jnp.transpose(kc_ref[...]).reshape(2, KV_GROUPS, HEAD_DIM, tk)
    xs = ks_ref[...]
    xst = jnp.transpose(xs).reshape(2, KV_GROUPS, HEAD_DIM, tk)
    kst_ref[...] = xst
    vsto_ref[...] = xst[1].astype(BF16)
    row = lax.broadcasted_iota(jnp.int32, (tk, HEAD_DIM), 0)
    lane = lax.broadcasted_iota(jnp.int32, (tk, HEAD_DIM), 1)
    onehot = jnp.where(lane == (row // SLC_BLOCK) % (SEL_KT // SLC_BLOCK), 1.0, 0.0)
    ksa_ref[...] = jnp.concatenate(
        [piece for g in range(KV_GROUPS) for piece in (xs[:, g * HEAD_DIM:(g + 1) * HEAD_DIM], onehot)],
        axis=1).astype(BF16)
    xw = kw_ref[...]
    kwo_ref[...] = xw[:, :KV_WIDTH].astype(BF16)
    vwto_ref[...] = jnp.transpose(xw[:, KV_WIDTH:]).reshape(KV_GROUPS, HEAD_DIM, tk).astype(BF16)


def _kvprep(kvc, kvs, kvw, b, s, tk):
    nt = s // tk
    in_spec = pl.BlockSpec((tk, 2 * KV_WIDTH), lambda bi, ti: (bi * nt + ti, 0))
    t_spec = pl.BlockSpec((None, 2, KV_GROUPS, HEAD_DIM, tk), lambda bi, ti: (bi, 0, 0, 0, ti))
    t_shape = jax.ShapeDtypeStruct((b, 2, KV_GROUPS, HEAD_DIM, s), F32)
    vt_spec = pl.BlockSpec((None, KV_GROUPS, HEAD_DIM, tk), lambda bi, ti: (bi, 0, 0, ti))
    vt_shape = jax.ShapeDtypeStruct((b, KV_GROUPS, HEAD_DIM, s), BF16)
    return pl.pallas_call(
        _kvprep_kernel,
        grid=(b, nt),
        in_specs=[in_spec, in_spec, in_spec],
        out_specs=[t_spec, t_spec,
                   pl.BlockSpec((None, tk, KV_GROUPS * LANES), lambda bi, ti: (bi, ti, 0)), vt_spec,
                   pl.BlockSpec((None, tk, KV_WIDTH), lambda bi, ti: (bi, ti, 0)), vt_spec],
        out_shape=[t_shape, t_shape, jax.ShapeDtypeStruct((b, s, KV_GROUPS * LANES), BF16), vt_shape,
                   jax.ShapeDtypeStruct((b, s, KV_WIDTH), BF16), vt_shape],
        compiler_params=_cparams(("arbitrary", "arbitrary")),
        name="kvprep",
    )(kvc, kvs, kvw)


def _nsa_prompt_kernel(q_ref, gt_ref, ks_ref, vst_ref, kw_ref, vwt_ref, kc_ref, vct_ref, esum_ref, go_ref,
                       o_ref, yt_ref, qst_ref, selb_ref, oc_ref, os_ref, m_ref, l_ref, acc_ref,
                       sa_ref, sb_ref, pa_ref, pb_ref, rhs_ref, *, n_blk, n_sel):
    qi = pl.program_id(1)
    q0 = qi * Q_TILE
    ncb = kc_ref.shape[0]
    bpt = SEL_KT // SLC_BLOCK
    tcol = lax.broadcasted_iota(jnp.int32, (1, Q_TILE), 1)
    tpos = q0 + tcol
    tpos4 = jnp.concatenate([tpos] * HEADS_PER_GROUP, axis=1)
    cend = lax.broadcasted_iota(jnp.int32, (ncb, 1), 0) * CMP_STRIDE + (CMP_BLOCK - 1)
    blk = lax.broadcasted_iota(jnp.int32, (n_blk, 1), 0)
    cur = tpos // SLC_BLOCK
    valid = blk <= cur
    forced = (blk == 0) | (blk == cur) | (blk == cur - 1)

    def keys(ref, k0, n, g):
        lane0 = (g // 2) * LANES
        return ref[pl.ds(k0, n), lane0:lane0 + LANES][:, (g % 2) * HEAD_DIM:(g % 2 + 1) * HEAD_DIM]

    def reset():
        m_ref[...] = jnp.full(m_ref.shape, NEG_INF, F32)
        l_ref[...] = jnp.zeros(l_ref.shape, F32)
        acc_ref[...] = jnp.zeros(acc_ref.shape, F32)

    for g in range(KV_GROUPS):
        qt = jnp.transpose(q_ref[:, g * 256:(g + 1) * 256] * ATTN_SCALE)
        qst = jnp.concatenate([qt[r * HEAD_DIM:(r + 1) * HEAD_DIM] for r in range(HEADS_PER_GROUP)],
                              axis=1).astype(BF16)
        qst_ref[g] = qst
        rhs_ref[g, 0:HEAD_DIM, :] = qst
        rhs_ref[g, HEAD_DIM:, :] = jnp.zeros((LANES - HEAD_DIM, HEADS_PER_GROUP * Q_TILE), BF16)
        sc = jnp.dot(keys(kc_ref, 0, ncb, g), qst, preferred_element_type=F32)
        mcf = jnp.where(cend <= tpos4, 1.0, 0.0)
        sc = jnp.where(mcf > 0.5, sc, NEG_INF)
        p = jnp.exp(sc - jnp.max(sc, axis=0, keepdims=True)) * mcf
        p = p / jnp.maximum(jnp.sum(p, axis=0, keepdims=True), 1e-30)
        oc_ref[g] = jnp.dot(vct_ref[g], p.astype(BF16), preferred_element_type=F32)
        imp = p[:, 0:Q_TILE]
        for r in range(1, HEADS_PER_GROUP):
            imp = imp + p[:, r * Q_TILE:(r + 1) * Q_TILE]
        impb = _dot_exact_left(esum_ref[...], imp)
        score = jnp.where(valid, impb + jnp.where(forced, FORCE_BONUS, 0.0), NEG_INF)
        selb_ref[g] = jnp.where(_rank(score, n_blk) < n_sel, 0.0, NEG_INF)

    bufs = ((sa_ref, pa_ref), (sb_ref, pb_ref))
    srows = slice(0, WIN_KT)
    krow = lax.broadcasted_iota(jnp.int32, (WIN_KT, 1), 0)
    zeros_pv = [0.0] * KV_GROUPS

    def static_tiles(tiles, pv):
        def scores(t, s_ref):
            for g in range(KV_GROUPS):
                s_ref[g, srows, :] = jnp.dot(tiles[t][0](g), qst_ref[g], preferred_element_type=F32)

        def pvs(t, p_ref):
            return [jnp.dot(tiles[t][1](g), p_ref[g, srows, :], preferred_element_type=F32)
                    for g in range(KV_GROUPS)]

        scores(0, sa_ref)
        for t in range(len(tiles)):
            (s_cur, p_cur), (s_nxt, p_prv) = bufs[t % 2], bufs[(t + 1) % 2]
            if t + 1 < len(tiles):
                scores(t + 1, s_nxt)
            if t > 0:
                pv = pvs(t - 1, p_prv)
            for g in range(KV_GROUPS):
                _softmax_stage(g, s_cur[g, srows, :], tiles[t][2](g), pv[g], m_ref, l_ref, acc_ref, p_cur, srows)
        return pvs(len(tiles) - 1, bufs[(len(tiles) - 1) % 2][1])

    n_full = qi // (SEL_KT // Q_TILE)
    last_full = jnp.maximum(n_full - 1, 0)

    def sel_scores(kt, s_ref):
        kc = jnp.clip(kt, 0, last_full)
        k0 = pl.multiple_of(kc * SEL_KT, SEL_KT)
        live = kt < n_full
        pad = jnp.zeros((2 * SUBLANES - bpt, HEADS_PER_GROUP * Q_TILE), F32)
        for g in range(KV_GROUPS):
            rows = jnp.where(live, selb_ref[g, pl.ds(kc * bpt, bpt), :], NEG_INF)
            rows = jnp.concatenate([jnp.concatenate([rows] * HEADS_PER_GROUP, axis=1), pad], axis=0)
            rhs_ref[g, HEAD_DIM:HEAD_DIM + 2 * SUBLANES, :] = rows.astype(BF16)
            s_ref[g] = jnp.dot(ks_ref[pl.ds(k0, SEL_KT), g * LANES:(g + 1) * LANES], rhs_ref[g],
                               preferred_element_type=F32)

    def sel_pv(kt, p_ref):
        k0 = pl.multiple_of(jnp.clip(kt, 0, last_full) * SEL_KT, SEL_KT)
        return [jnp.dot(vst_ref[g, :, pl.ds(k0, SEL_KT)], p_ref[g], preferred_element_type=F32)
                for g in range(KV_GROUPS)]

    def sel_step(kt, s_cur, s_nxt, p_cur, p_prv):
        sel_scores(kt + 1, s_nxt)
        pv = sel_pv(kt - 1, p_prv)
        for g in range(KV_GROUPS):
            _softmax_stage(g, s_cur[g], None, pv[g], m_ref, l_ref, acc_ref, p_cur, slice(0, SEL_KT))

    def sel_pair(i, carry):
        sel_step(2 * i, sa_ref, sb_ref, pa_ref, pb_ref)
        sel_step(2 * i + 1, sb_ref, sa_ref, pb_ref, pa_ref)
        return carry

    reset()
    pb_ref[...] = jnp.zeros(pb_ref.shape, BF16)
    sel_scores(0, sa_ref)
    n_pairs = (n_full + 1) // 2
    lax.fori_loop(0, n_pairs, sel_pair, 0)
    pv = sel_pv(2 * n_pairs - 1, pb_ref)

    def block_bias(g, blk0):
        rows = selb_ref[g, pl.ds(blk0, WIN_KT // SLC_BLOCK), :]
        return jnp.concatenate([jnp.broadcast_to(rows[i:i + 1], (SLC_BLOCK, Q_TILE))
                                for i in range(WIN_KT // SLC_BLOCK)], axis=0)

    def sel_tail_tile(k0, blk0, extra):
        return (lambda g: ks_ref[pl.ds(k0, WIN_KT), g * LANES:(g + 1) * LANES][:, :HEAD_DIM],
                lambda g: vst_ref[g, :, pl.ds(k0, WIN_KT)],
                lambda g: block_bias(g, blk0) + extra)

    bpq = Q_TILE // SLC_BLOCK
    odd = jnp.where(qi % (SEL_KT // Q_TILE) == 1, 0.0, NEG_INF)
    prev0 = pl.multiple_of(jnp.maximum(qi - 1, 0) * Q_TILE, Q_TILE)
    pv = static_tiles([sel_tail_tile(pl.multiple_of(q0, Q_TILE), qi * bpq, jnp.where(krow <= tcol, 0.0, NEG_INF)),
                       sel_tail_tile(prev0, jnp.maximum(qi - 1, 0) * bpq, odd)], pv)
    for g in range(KV_GROUPS):
        os_ref[g] = (acc_ref[g] + pv[g]) / jnp.maximum(l_ref[g], 1e-30)

    reset()
    n_back = WINDOW // WIN_KT

    def win_tile(t):
        k0 = pl.multiple_of(jnp.maximum(qi - t, 0) * WIN_KT, WIN_KT)
        if t == 0:
            bias = jnp.where(krow <= tcol, 0.0, NEG_INF)
        else:
            off = jnp.where(qi - t >= 0, 0.0, NEG_INF)
            bias = jnp.where(krow > tcol, off, NEG_INF) if t == n_back else jnp.zeros((WIN_KT, Q_TILE), F32) + off
        return (lambda g: keys(kw_ref, k0, WIN_KT, g), lambda g: vwt_ref[g, :, pl.ds(k0, WIN_KT)], lambda g: bias)

    pv = static_tiles([win_tile(t) for t in range(n_back + 1)], zeros_pv)

    gates = jax.nn.sigmoid(jnp.transpose(gt_ref[...]))
    for g in range(KV_GROUPS):
        o_c, o_s = oc_ref[g], os_ref[g]
        o_w = (acc_ref[g] + pv[g]) / jnp.maximum(l_ref[g], 1e-30)
        for r in range(HEADS_PER_GROUP):
            cs = slice(r * Q_TILE, (r + 1) * Q_TILE)
            gi = (g * HEADS_PER_GROUP + r) * 3
            y = (gates[gi:gi + 1] * o_c[:, cs] + gates[gi + 1:gi + 2] * o_s[:, cs]
                 + gates[gi + 2:gi + 3] * o_w[:, cs])
            row0 = (g * HEADS_PER_GROUP + r) * HEAD_DIM
            yt_ref[row0:row0 + HEAD_DIM, :] = y

    o_ref[...] = _rms(jnp.transpose(yt_ref[...]), go_ref[...]).astype(BF16)


def _esum_matrix(n_blk, n_rows, row_of_block0):
    r = SLC_BLOCK // CMP_STRIDE
    j = np.arange(n_blk)[:, None]
    i = np.arange(n_rows)[None, :] - row_of_block0
    e = (i >= r * j - 1) & (i <= r * j + r - 1) & (i >= 0)
    return jnp.asarray(e, BF16)


def _nsa_prompt(za, gates, ks, vst, kw, vwt, kc, vct, g_out, b, s):
    assert WINDOW % WIN_KT == 0 and WIN_KT == Q_TILE and s % SEL_KT == 0
    nqt = s // Q_TILE
    nq = HEADS_PER_GROUP * Q_TILE
    n_blk = s // SLC_BLOCK
    n_sel = min(N_SEL, n_blk)
    ncb = s // CMP_STRIDE
    esum = _esum_matrix(n_blk, ncb, 0) * jnp.asarray(np.arange(ncb)[None, :] < ncb - 1, BF16)
    qcol = 2 * SSM_WIDTH // ATT_WIDTH
    per_b3 = lambda shape: pl.BlockSpec((None,) + shape, lambda bi, qi: (bi, 0, 0))
    per_b4 = lambda shape: pl.BlockSpec((None,) + shape, lambda bi, qi: (bi, 0, 0, 0))
    return pl.pallas_call(
        functools.partial(_nsa_prompt_kernel, n_blk=n_blk, n_sel=n_sel),
        grid=(b, nqt),
        in_specs=[pl.BlockSpec((Q_TILE, ATT_WIDTH), lambda bi, qi: (bi * nqt + qi, qcol)),
                  pl.BlockSpec((Q_TILE, GATE_PAD), lambda bi, qi: (bi * nqt + qi, 0)),
                  per_b3((s, KV_GROUPS * LANES)), per_b4((KV_GROUPS, HEAD_DIM, s)),
                  per_b3((s, KV_WIDTH)), per_b4((KV_GROUPS, HEAD_DIM, s)),
                  per_b3((ncb, KV_WIDTH)), per_b4((KV_GROUPS, HEAD_DIM, ncb)),
                  pl.BlockSpec((n_blk, ncb), lambda bi, qi: (0, 0)),
                  pl.BlockSpec((1, ATT_WIDTH), lambda bi, qi: (0, 0))],
        out_specs=pl.BlockSpec((Q_TILE, ATT_WIDTH), lambda bi, qi: (bi * nqt + qi, 0)),
        out_shape=jax.ShapeDtypeStruct((b * s, ATT_WIDTH), BF16),
        scratch_shapes=[pltpu.VMEM((ATT_WIDTH, Q_TILE), F32),
                        pltpu.VMEM((KV_GROUPS, HEAD_DIM, nq), BF16),
                        pltpu.VMEM((KV_GROUPS, n_blk, Q_TILE), F32),
                        pltpu.VMEM((KV_GROUPS, HEAD_DIM, nq), F32), pltpu.VMEM((KV_GROUPS, HEAD_DIM, nq), F32),
                        pltpu.VMEM((KV_GROUPS, 1, nq), F32), pltpu.VMEM((KV_GROUPS, 1, nq), F32),
                        pltpu.VMEM((KV_GROUPS, HEAD_DIM, nq), F32),
                        pltpu.VMEM((KV_GROUPS, SEL_KT, nq), F32), pltpu.VMEM((KV_GROUPS, SEL_KT, nq), F32),
                        pltpu.VMEM((KV_GROUPS, SEL_KT, nq), BF16), pltpu.VMEM((KV_GROUPS, SEL_KT, nq), BF16),
                        pltpu.VMEM((KV_GROUPS, LANES, nq), BF16)],
        compiler_params=_cparams(("arbitrary", "arbitrary")),
        name="nsa_prompt",
    )(za, gates, ks, vst, kw, vwt, kc, vct, esum, g_out.reshape(1, -1))


def _cmp_pages_kernel(pt_ref, *refs, pg):
    x_refs = refs[:pg]
    wbd_ref, pe_ref, wb_ref, kc_ref, vc_ref, xk_ref, xv_ref, carry_ref = refs[pg:]
    rows = pg * (PAGE_SIZE // CMP_STRIDE)
    h = pl.program_id(1)

    @pl.when(h == 0)
    def _():
        carry_ref[...] = jnp.zeros_like(carry_ref)

    cpp = PAGE_SIZE // CMP_STRIDE
    pitch = rows + SUBLANES
    def regroup(s, xs_ref, pages):
        for k in pages:
            for gp in range(KV_GROUPS // 2):
                t = jnp.transpose(x_refs[k][s, 2 * gp:2 * gp + 2].reshape(2 * HEAD_DIM, PAGE_SIZE))
                for n in range(cpp):
                    xs_ref[gp, pl.ds(k * cpp + n, CMP_STRIDE, stride=pitch), :] = (
                        t[n * CMP_STRIDE:(n + 1) * CMP_STRIDE])

    regroup(0, xk_ref, range(pg))
    ppj = -(-pg // CMP_STRIDE)
    row = lax.broadcasted_iota(jnp.int32, (rows, 1), 0)
    for s, out_ref, xs_ref in ((0, kc_ref, xk_ref), (1, vc_ref, xv_ref)):
        acc = jnp.zeros((rows, 2 * KV_WIDTH), F32)
        for j in range(CMP_STRIDE):
            xs = jnp.concatenate([xs_ref[gp, j * pitch:j * pitch + rows, :]
                                  for gp in range(KV_GROUPS // 2)], axis=1).astype(BF16)
            acc = acc + jnp.dot(xs, wbd_ref[s, j], preferred_element_type=F32)
            if s == 0:
                regroup(1, xv_ref, range(min(j * ppj, pg), min((j + 1) * ppj, pg)))
        bias = jnp.dot(pe_ref[s], wb_ref[s], preferred_element_type=F32)[0:1]
        lo = acc[:, :KV_WIDTH]
        prev = jnp.where(row == 0, carry_ref[s, SUBLANES - 1:SUBLANES, :], pltpu.roll(lo, 1, 0))
        out_ref[...] = (prev + acc[:, KV_WIDTH:] + bias).astype(BF16)
        carry_ref[s] = lo[rows - SUBLANES:, :]


def _cmp_pages(cache_cmp, page_table, wbd, pe8, wb):
    db, n_pages = page_table.shape
    cpp = PAGE_SIZE // CMP_STRIDE
    pg = min(CMP_PG, n_pages)
    nh = n_pages // pg
    page_shape = cache_cmp.shape[1:]
    page_specs = [pl.BlockSpec((None,) + page_shape, functools.partial(
        lambda bi, hi, pt, k: (pt[bi, hi * pg + k], 0, 0, 0, 0), k=k)) for k in range(pg)]
    const = lambda shape: pl.BlockSpec(shape, lambda bi, hi, pt: (0,) * len(shape))
    out_spec = pl.BlockSpec((None, pg * cpp, KV_WIDTH), lambda bi, hi, pt: (bi, hi, 0))
    out_shape = jax.ShapeDtypeStruct((db, n_pages * cpp, KV_WIDTH), BF16)
    return pl.pallas_call(
        functools.partial(_cmp_pages_kernel, pg=pg),
        grid_spec=pltpu.PrefetchScalarGridSpec(
            num_scalar_prefetch=1,
            grid=(db, nh),
            in_specs=page_specs + [const(wbd.shape), const(pe8.shape), const(wb.shape)],
            out_specs=[out_spec, out_spec],
            scratch_shapes=[pltpu.VMEM((KV_GROUPS // 2, CMP_STRIDE * (pg * cpp + SUBLANES), LANES), F32),
                            pltpu.VMEM((KV_GROUPS // 2, CMP_STRIDE * (pg * cpp + SUBLANES), LANES), F32),
                            pltpu.VMEM((2, SUBLANES, KV_WIDTH), F32)]),
        out_shape=[out_shape, out_shape],
        compiler_params=_cparams(("arbitrary", "arbitrary")),
        name="cmp_pages",
    )(page_table, *([cache_cmp] * pg), wbd, pe8, wb)


def _query_blockdiag(q_ref):
    qt = jnp.transpose(q_ref[...] * ATTN_SCALE)
    tiled = jnp.concatenate([qt] * KV_GROUPS, axis=0)
    rowg = lax.broadcasted_iota(jnp.int32, (KV_WIDTH, 1), 0) // HEAD_DIM
    colg = lax.broadcasted_iota(jnp.int32, (1, LANES), 1) // HEADS_PER_GROUP
    return jnp.where(rowg == colg, tiled, 0.0).astype(BF16)


def _diag_heads(o):
    rowg = lax.broadcasted_iota(jnp.int32, (N_HEADS, 1), 0) // HEADS_PER_GROUP
    out = jnp.zeros((N_HEADS, HEAD_DIM), F32)
    for g in range(KV_GROUPS):
        out = out + jnp.where(rowg == g, o[:N_HEADS, g * HEAD_DIM:(g + 1) * HEAD_DIM], 0.0)
    return out


def _smp_cmp_kernel(q_ref, kc_ref, vc_ref, esum_ref, gsum_ref, oc_ref, imp_ref, *, qpos):
    qbd = _query_blockdiag(q_ref)
    nr = kc_ref.shape[0]
    s = jnp.dot(kc_ref[...], qbd, preferred_element_type=F32)
    row = lax.broadcasted_iota(jnp.int32, (nr, 1), 0)
    mf = jnp.where(row >= 1, jnp.where((row - 1) * CMP_STRIDE + CMP_BLOCK - 1 <= qpos, 1.0, 0.0), 0.0)
    s = jnp.where(mf > 0.5, s, NEG_INF)
    p = jnp.exp(s - jnp.max(s, axis=0, keepdims=True)) * mf
    p = p / jnp.maximum(jnp.sum(p, axis=0, keepdims=True), 1e-30)
    o = jnp.dot(jnp.transpose(p).astype(BF16), vc_ref[...], preferred_element_type=F32)
    oc_ref[...] = _diag_heads(o)
    impg = _dot_exact_right(p, gsum_ref[...])
    imp_ref[...] = _dot_exact_left(esum_ref[...], impg)


def _smp_cmp(q_pad, kc, vc, n_blk_pad, qpos):
    db, nr, _ = kc.shape
    n_blk = (qpos + SLC_BLOCK) // SLC_BLOCK
    esum = _esum_matrix(n_blk_pad, nr, 1) * jnp.asarray(np.arange(n_blk_pad)[:, None] < n_blk, BF16)
    gs = (np.arange(LANES)[:, None] // HEADS_PER_GROUP == np.arange(LANES)[None, :]) & (
        np.arange(LANES)[:, None] < N_HEADS)
    gsum = jnp.asarray(gs, BF16)
    per_b = lambda shape: pl.BlockSpec((None,) + shape, lambda bi: (bi, 0, 0))
    const = lambda shape: pl.BlockSpec(shape, lambda bi: (0,) * len(shape))
    return pl.pallas_call(
        functools.partial(_smp_cmp_kernel, qpos=qpos),
        grid=(db,),
        in_specs=[per_b((LANES, HEAD_DIM)), per_b((nr, KV_WIDTH)), per_b((nr, KV_WIDTH)),
                  const((n_blk_pad, nr)), const((LANES, LANES))],
        out_specs=[per_b((N_HEADS, HEAD_DIM)), per_b((n_blk_pad, LANES))],
        out_shape=[jax.ShapeDtypeStruct((db, N_HEADS, HEAD_DIM), F32),
                   jax.ShapeDtypeStruct((db, n_blk_pad, LANES), F32)],
        compiler_params=_cparams(("arbitrary",)),
        name="smp_cmp",
    )(q_pad, kc, vc, esum, gsum)


def _smp_topk_kernel(imp_ref, idx_ref, *, n_blk, n_sel):
    nbp = imp_ref.shape[0]
    blk = lax.broadcasted_iota(jnp.int32, (nbp, 1), 0)
    cur = n_blk - 1
    valid = blk <= cur
    forced = (blk == 0) | (blk == cur) | (blk == cur - 1)
    score = jnp.where(valid, imp_ref[...] + jnp.where(forced, FORCE_BONUS, 0.0), NEG_INF)
    rank = _rank(score, nbp)
    blkf = blk.astype(F32)
    for k in range(n_sel):
        pick = jnp.sum(jnp.where(rank == float(k), blkf, 0.0), axis=0, keepdims=True)
        idx_ref[k:k + 1, :] = pick.astype(jnp.int32)


def _smp_topk(score_t, n_blk, n_sel):
    nbp, w = score_t.shape
    full = lambda shape: pl.BlockSpec(shape, lambda: (0,) * len(shape))
    return pl.pallas_call(
        functools.partial(_smp_topk_kernel, n_blk=n_blk, n_sel=n_sel),
        in_specs=[full((nbp, w))],
        out_specs=full((n_sel, w)),
        out_shape=jax.ShapeDtypeStruct((n_sel, w), jnp.int32),
        name="smp_topk",
    )(score_t)


def _smp_attn_kernel(page_ref, half_ref, isnew_ref, *refs, n_slots, n_sel, past, wbuf):
    blk_refs = refs[:n_slots]
    (q_ref, ksn_ref, win_ref, kwn_ref, oc_ref, gt_ref, go_ref, o_ref, wout_ref) = refs[n_slots:]
    b = pl.program_id(0)
    lane_half = lax.broadcasted_iota(jnp.int32, (1, PAGE_SIZE), 1) // SLC_BLOCK
    row8 = lax.broadcasted_iota(jnp.int32, (SUBLANES, 1), 0)
    wlane = lax.broadcasted_iota(jnp.int32, (1, wbuf), 1)
    kposw = past - wbuf + wlane
    wbias = jnp.where((kposw > past - WINDOW) & (kposw >= 0), 0.0, NEG_INF)
    nt = (((1,), (1,)), ((), ()))

    def bf(x):
        return x.astype(BF16).astype(F32)

    ys = []
    for g in range(KV_GROUPS):
        q = q_ref[g] * ATTN_SCALE
        qb = q.astype(BF16)
        kn = bf(ksn_ref[:, g * HEAD_DIM:(g + 1) * HEAD_DIM])
        vn = bf(ksn_ref[:, KV_WIDTH + g * HEAD_DIM:KV_WIDTH + (g + 1) * HEAD_DIM])
        kwn = bf(kwn_ref[:, g * HEAD_DIM:(g + 1) * HEAD_DIM])
        vwn = bf(kwn_ref[:, KV_WIDTH + g * HEAD_DIM:KV_WIDTH + (g + 1) * HEAD_DIM])

        slots = range(g * n_sel, (g + 1) * n_sel)
        kt = jnp.concatenate([blk_refs[k][0] for k in slots], axis=1).astype(BF16)
        vt = jnp.concatenate([blk_refs[k][1] for k in slots], axis=1).astype(BF16)
        keeps = [jnp.where(isnew_ref[b, k] == 0, 1.0, 0.0) for k in slots]
        mf = jnp.concatenate([jnp.where(lane_half == half_ref[b, k], keep, 0.0)
                              for k, keep in zip(slots, keeps)], axis=1)
        has_new = 1.0 - functools.reduce(jnp.minimum, keeps)
        s = jnp.where(mf > 0.5, jnp.dot(qb, kt, preferred_element_type=F32), NEG_INF)
        s_n = jnp.where(has_new > 0.5, jnp.sum(bf(q) * kn, axis=1, keepdims=True), NEG_INF)
        m = jnp.maximum(jnp.max(s, axis=1, keepdims=True), s_n)
        p = jnp.exp(s - m) * mf
        p_n = jnp.exp(s_n - m) * has_new
        l = jnp.sum(p, axis=1, keepdims=True) + p_n
        o_s = (lax.dot_general(p.astype(BF16), vt, nt, preferred_element_type=F32) + bf(p_n) * vn)
        o_s = o_s / jnp.maximum(l, 1e-30)

        s = jnp.dot(qb, win_ref[0, g].astype(BF16), preferred_element_type=F32) + wbias
        s_n = jnp.sum(bf(q) * kwn, axis=1, keepdims=True)
        m = jnp.maximum(jnp.max(s, axis=1, keepdims=True), s_n)
        p = jnp.exp(s - m)
        p_n = jnp.exp(s_n - m)
        l = jnp.sum(p, axis=1, keepdims=True) + p_n
        o_w = (lax.dot_general(p.astype(BF16), win_ref[1, g].astype(BF16), nt, preferred_element_type=F32)
               + bf(p_n) * vwn)
        o_w = o_w / l

        gates = jax.nn.sigmoid(gt_ref[g])
        ys.append(gates[:, 0:1] * oc_ref[g] + gates[:, 1:2] * o_s + gates[:, 2:3] * o_w)

    real = row8 < HEADS_PER_GROUP
    ssq = functools.reduce(lambda a, c: a + c, [jnp.sum(jnp.where(real, y * y, 0.0), axis=1, keepdims=True)
                                                 for y in ys])
    scale = lax.rsqrt(jnp.sum(ssq, axis=0, keepdims=True) / ATT_WIDTH + EPS)
    for g in range(KV_GROUPS):
        o_ref[g] = (ys[g] * scale * go_ref[g]).astype(BF16)

    new_col = jnp.transpose(jnp.broadcast_to(kwn_ref[...], (SUBLANES, 2 * KV_WIDTH)))[:, 0:1]
    for s2 in range(2):
        for g in range(KV_GROUPS):
            c0 = (s2 * KV_GROUPS + g) * HEAD_DIM
            wout_ref[s2, g] = jnp.where(wlane == wbuf - 1, new_col[c0:c0 + HEAD_DIM],
                                        pltpu.roll(win_ref[s2, g], wbuf - 1, 1))


def _smp_attn(page, half, isnew, cache_slc, q_g, kvs_new, cache_win, kvw_new, o_c, gates, g_out, past):
    db, n_slots = page.shape
    n_sel = n_slots // KV_GROUPS
    wbuf = cache_win.shape[-1]
    width = 2 * KV_WIDTH
    blk_specs = [pl.BlockSpec((None, 2, None, HEAD_DIM, PAGE_SIZE), functools.partial(
        lambda bi, pg, hf, nw, k: (pg[bi, k], 0, k // n_sel, 0, 0), k=k)) for k in range(n_slots)]
    per_b3 = lambda shape: pl.BlockSpec((None,) + shape, lambda bi, pg, hf, nw: (bi, 0, 0))
    per_b4 = lambda shape: pl.BlockSpec((None,) + shape, lambda bi, pg, hf, nw: (bi, 0, 0, 0))
    per_b5 = lambda shape: pl.BlockSpec((None,) + shape, lambda bi, pg, hf, nw: (bi, 0, 0, 0, 0))
    win_shape = (2, KV_GROUPS, HEAD_DIM, wbuf)
    head_shape = (KV_GROUPS, SUBLANES, HEAD_DIM)
    return pl.pallas_call(
        functools.partial(_smp_attn_kernel, n_slots=n_slots, n_sel=n_sel, past=past, wbuf=wbuf),
        grid_spec=pltpu.PrefetchScalarGridSpec(
            num_scalar_prefetch=3,
            grid=(db,),
            in_specs=blk_specs + [per_b4(head_shape), per_b3((1, width)), per_b5(win_shape), per_b3((1, width)),
                                  per_b4(head_shape), per_b4((KV_GROUPS, SUBLANES, 3)),
                                  pl.BlockSpec(head_shape, lambda bi, pg, hf, nw: (0, 0, 0))],
            out_specs=[per_b4(head_shape), per_b5(win_shape)]),
        out_shape=[jax.ShapeDtypeStruct((db,) + head_shape, BF16),
                   jax.ShapeDtypeStruct((db,) + win_shape, F32)],
        compiler_params=_cparams(("arbitrary",)),
        name="smp_attn",
    )(page, half, isnew, *([cache_slc] * n_slots), q_g, kvs_new.reshape(db, 1, width), cache_win,
      kvw_new.reshape(db, 1, width), o_c, gates, g_out)


def _nsa_sample(za, gates_raw, kvs_new, kvw_new, cache_cmp, cache_slc, cache_win, page_table, wbd, pe8, wb,
                g_out):
    db, n_pages = page_table.shape
    past = n_pages * PAGE_SIZE
    n_blk = (past + SLC_BLOCK) // SLC_BLOCK
    n_sel = min(N_SEL, n_blk)
    n_blk_pad = -(-n_blk // SUBLANES) * SUBLANES
    q = za[:, 2 * SSM_WIDTH:].reshape(db, N_HEADS, HEAD_DIM)
    q_pad = jnp.pad(q, ((0, 0), (0, LANES - N_HEADS), (0, 0)))
    kc, vc = _cmp_pages(cache_cmp, page_table, wbd, pe8, wb)
    o_c, imp = _smp_cmp(q_pad, kc, vc, n_blk_pad, past)
    score_t = jnp.transpose(imp[:, :, :KV_GROUPS], (1, 0, 2)).reshape(n_blk_pad, db * KV_GROUPS)
    lane_pad = -(-db * KV_GROUPS // LANES) * LANES
    score_t = jnp.pad(score_t, ((0, 0), (0, lane_pad - db * KV_GROUPS)))
    idx = _smp_topk(score_t, n_blk, n_sel)[:, :db * KV_GROUPS]
    idx = jnp.transpose(idx.reshape(n_sel, db, KV_GROUPS), (1, 2, 0))
    n_past_blk = past // SLC_BLOCK
    per_page = PAGE_SIZE // SLC_BLOCK
    jp = jnp.minimum(idx, n_past_blk - 1).reshape(db, KV_GROUPS * n_sel)
    page = jnp.take_along_axis(page_table, jp // per_page, axis=1).astype(jnp.int32)
    half = (jp % per_page).astype(jnp.int32)
    isnew = (idx >= n_past_blk).reshape(db, KV_GROUPS * n_sel).astype(jnp.int32)
    pad_heads = lambda a: jnp.pad(a.reshape(a.shape[0], KV_GROUPS, HEADS_PER_GROUP, a.shape[-1]),
                                  ((0, 0), (0, 0), (0, SUBLANES - HEADS_PER_GROUP), (0, 0)))
    gates = pad_heads(gates_raw[:, :3 * N_HEADS].reshape(db, N_HEADS, 3))
    g_out_g = pad_heads(g_out.reshape(1, N_HEADS, HEAD_DIM))[0]
    y, win_new = _smp_attn(page, half, isnew, cache_slc, pad_heads(q), kvs_new, cache_win, kvw_new,
                           pad_heads(o_c), gates, g_out_g, past)
    return y[:, :, :HEADS_PER_GROUP].reshape(db, ATT_WIDTH), win_new


def _outproj_kernel(x_ref, ms_ref, ma_ref, w_ref, gt_ref, sc_ref, sh_ref, g_ref, x1_ref, h2_ref):
    mixed = jnp.concatenate([ms_ref[...], ma_ref[...]], axis=1)
    x1 = x_ref[...] + gt_ref[...] * jnp.dot(mixed, w_ref[...], preferred_element_type=F32)
    x1_ref[...] = x1
    h2_ref[...] = (_rms(x1, g_ref[...]) * (1.0 + sc_ref[...]) + sh_ref[...]).astype(BF16)


def _outproj(x, m_ssm, m_att, w_out, gt1, sc2, sh2, g2, tm, rows_per_mod):
    n, d = x.shape
    r = gt1.shape[1]
    tpm = rows_per_mod // tm
    mod_spec = pl.BlockSpec((None, r, d), lambda i: (i // tpm, 0, 0))
    row = lambda w: pl.BlockSpec((tm, w), lambda i: (i, 0))
    return pl.pallas_call(
        _outproj_kernel,
        grid=(n // tm,),
        in_specs=[row(d), row(SSM_WIDTH), row(ATT_WIDTH), pl.BlockSpec((d, d), lambda i: (0, 0)),
                  mod_spec, mod_spec, mod_spec, pl.BlockSpec((1, d), lambda i: (0, 0))],
        out_specs=[row(d), row(d)],
        out_shape=[jax.ShapeDtypeStruct((n, d), F32), jax.ShapeDtypeStruct((n, d), BF16)],
        compiler_params=_cparams(("arbitrary",)),
        name="outproj",
    )(x, m_ssm, m_att, w_out, gt1, sc2, sh2, g2.reshape(1, d))


def _ffn_act(a_v, a_g, a1_v, a1_g, a2_v, a2_g, cwv_ref, cwg_ref, cbv_ref, cbg_ref):
    val = cbv_ref[...] + cwv_ref[2:3, :] * a_v + cwv_ref[0:1, :] * a2_v + cwv_ref[1:2, :] * a1_v
    gate = cbg_ref[...] + cwg_ref[2:3, :] * a_g + cwg_ref[0:1, :] * a2_g + cwg_ref[1:2, :] * a1_g
    return (gate * jax.nn.sigmoid(gate) * val).astype(BF16)


def _ffn_finish(j, contrib, x1_ref, gt_ref, gf_ref, y_ref):
    @pl.when(j == 0)
    def _():
        y_ref[...] = contrib

    @pl.when(j > 0)
    def _():
        y_ref[...] += contrib

    @pl.when(j == FFN_NF - 1)
    def _():
        y_ref[...] = _rms(x1_ref[...] + gt_ref[...] * y_ref[...], gf_ref[...])


def _ffn_seq_kernel(h_ref, x1_ref, gt_ref, wv_ref, wg_ref, cwv_ref, cwg_ref, cbv_ref, cbg_ref, wd_ref, gf_ref,
                    y_ref, tv_ref, tg_ref, sv_ref, sg_ref, hv_ref, hg_ref, *, tm, tpb, rs):
    i = pl.program_id(0)
    j = pl.program_id(1)

    @pl.when(i % tpb == 0)
    def _():
        hv_ref[j] = jnp.zeros((SUBLANES, FFN_TF), F32)
        hg_ref[j] = jnp.zeros((SUBLANES, FFN_TF), F32)

    @pl.when(j == 0)
    def _():
        y_ref[...] = jnp.zeros(y_ref.shape, F32)

    sv_ref[0:SUBLANES, :] = hv_ref[j]
    sg_ref[0:SUBLANES, :] = hg_ref[j]

    def up(k):
        hk = h_ref[k * rs:(k + 1) * rs, :]
        o = SUBLANES + k * rs
        sv_ref[o:o + rs, :] = jnp.dot(hk, wv_ref[...], preferred_element_type=F32)
        sg_ref[o:o + rs, :] = jnp.dot(hk, wg_ref[...], preferred_element_type=F32)

    def down(k):
        o = SUBLANES + k * rs
        act = _ffn_act(sv_ref[o:o + rs, :], sg_ref[o:o + rs, :], sv_ref[o - 1:o - 1 + rs, :],
                       sg_ref[o - 1:o - 1 + rs, :], sv_ref[o - 2:o - 2 + rs, :], sg_ref[o - 2:o - 2 + rs, :],
                       cwv_ref, cwg_ref, cbv_ref, cbg_ref)
        y_ref[k * rs:(k + 1) * rs, :] += jnp.dot(act, wd_ref[...], preferred_element_type=F32)

    up(0)
    for k in range(tm // rs):
        if k + 1 < tm // rs:
            up(k + 1)
        down(k)

    for s_ref, halo_ref, t_ref in ((sv_ref, hv_ref, tv_ref), (sg_ref, hg_ref, tg_ref)):
        halo_ref[j] = s_ref[tm:tm + SUBLANES, :]
        t_ref[...] = s_ref[tm + SUBLANES - 2:tm + SUBLANES, :]

    @pl.when(j == FFN_NF - 1)
    def _():
        y_ref[...] = _rms(x1_ref[...] + gt_ref[...] * y_ref[...], gf_ref[...])


def _ffn_step_kernel(h_ref, x1_ref, gt_ref, wv_ref, wg_ref, cwv_ref, cwg_ref, cbv_ref, cbg_ref, wd_ref, gf_ref,
                     p0v_ref, p0g_ref, p1v_ref, p1g_ref, y_ref, av_ref, ag_ref):
    j = pl.program_id(1)
    h = h_ref[...]
    a_v = jnp.dot(h, wv_ref[...], preferred_element_type=F32)
    a_g = jnp.dot(h, wg_ref[...], preferred_element_type=F32)
    av_ref[...] = a_v
    ag_ref[...] = a_g
    act = _ffn_act(a_v, a_g, p1v_ref[...], p1g_ref[...], p0v_ref[...], p0g_ref[...],
                   cwv_ref, cwg_ref, cbv_ref, cbg_ref)
    _ffn_finish(j, jnp.dot(act, wd_ref[...], preferred_element_type=F32), x1_ref, gt_ref, gf_ref, y_ref)


def _ffn_specs(d, tm, r, tpm):
    row_once = pl.BlockSpec((tm, d), lambda i, j: (i, 0), pipeline_mode=pl.Buffered(1))
    return [row_once, row_once, pl.BlockSpec((None, r, d), lambda i, j: (i // tpm, 0, 0)),
            pl.BlockSpec((d, FFN_TF), lambda i, j: (0, j)), pl.BlockSpec((d, FFN_TF), lambda i, j: (0, j + FFN_NF)),
            pl.BlockSpec((CONV_W, FFN_TF), lambda i, j: (0, j)),
            pl.BlockSpec((CONV_W, FFN_TF), lambda i, j: (0, j + FFN_NF)),
            pl.BlockSpec((1, FFN_TF), lambda i, j: (0, j)), pl.BlockSpec((1, FFN_TF), lambda i, j: (0, j + FFN_NF)),
            pl.BlockSpec((FFN_TF, d), lambda i, j: (j, 0)), pl.BlockSpec((1, d), lambda i, j: (0, 0))]


def _ffn_seq(h2, x1, gt2, w_up, conv_w, conv_b, w_down, g_final, tm, rows_per_mod):
    n, d = x1.shape
    tpb = rows_per_mod // tm
    nt = n // tm
    cb = conv_b.reshape(1, -1)
    y, tv, tg = pl.pallas_call(
        functools.partial(_ffn_seq_kernel, tm=tm, tpb=tpb, rs=min(FFN_RS, tm)),
        grid=(nt, FFN_NF),
        in_specs=_ffn_specs(d, tm, gt2.shape[1], tpb),
        out_specs=[pl.BlockSpec((tm, d), lambda i, j: (i, 0), pipeline_mode=pl.Buffered(1)),
                   pl.BlockSpec((None, CONV_W - 1, FFN_TF), lambda i, j: (i, 0, j)),
                   pl.BlockSpec((None, CONV_W - 1, FFN_TF), lambda i, j: (i, 0, j))],
        out_shape=[jax.ShapeDtypeStruct((n, d), F32),
                   jax.ShapeDtypeStruct((nt, CONV_W - 1, D_FF), F32),
                   jax.ShapeDtypeStruct((nt, CONV_W - 1, D_FF), F32)],
        scratch_shapes=[pltpu.VMEM((tm + SUBLANES, FFN_TF), F32), pltpu.VMEM((tm + SUBLANES, FFN_TF), F32),
                        pltpu.VMEM((FFN_NF, SUBLANES, FFN_TF), F32), pltpu.VMEM((FFN_NF, SUBLANES, FFN_TF), F32)],
        compiler_params=_cparams(("arbitrary", "arbitrary")),
        name="ffn_seq",
    )(h2, x1, gt2, w_up, w_up, conv_w, conv_w, cb, cb, w_down, g_final.reshape(1, d))
    tails = jnp.concatenate([tv, tg], axis=-1)
    return y, tails[tpb - 1::tpb]


def _ffn_step(h2, x1, gt2, w_up, conv_w, conv_b, w_down, g_final, conv_prev):
    n, d = x1.shape
    cb = conv_b.reshape(1, -1)
    prev_v = pl.BlockSpec((n, FFN_TF), lambda i, j: (0, j))
    prev_g = pl.BlockSpec((n, FFN_TF), lambda i, j: (0, j + FFN_NF))
    p0, p1 = conv_prev[:, 0], conv_prev[:, 1]
    y, a_v, a_g = pl.pallas_call(
        _ffn_step_kernel,
        grid=(1, FFN_NF),
        in_specs=_ffn_specs(d, n, gt2.shape[1], 1) + [prev_v, prev_g, prev_v, prev_g],
        out_specs=[pl.BlockSpec((n, d), lambda i, j: (0, 0)),
                   pl.BlockSpec((n, FFN_TF), lambda i, j: (0, j)), pl.BlockSpec((n, FFN_TF), lambda i, j: (0, j))],
        out_shape=[jax.ShapeDtypeStruct((n, d), F32),
                   jax.ShapeDtypeStruct((n, D_FF), F32), jax.ShapeDtypeStruct((n, D_FF), F32)],
        compiler_params=_cparams(("arbitrary", "arbitrary")),
        name="ffn_step",
    )(h2, x1, gt2, w_up, w_up, conv_w, conv_w, cb, cb, w_down, g_final.reshape(1, d), p0, p0, p1, p1)
    return y, jnp.stack([p1, jnp.concatenate([a_v, a_g], axis=-1)], axis=1)


def kernel(x_prompt, x_sample, cache_cmp_kv, cache_slc_kv, cache_win_kv, state_ssm_re, state_ssm_im, state_conv,
           page_table, c_prompt, c_sample, w_ada, b_ada, g_norm1, w_in, ssm_lam_re, ssm_lam_im, ssm_log_dt,
           ssm_b_re, ssm_b_im, ssm_c_re, ssm_c_im, ssm_d, w_cmp, pe_cmp, g_out_ssm, g_out_att, w_out, g_norm2,
           w_up, conv_w, conv_b, w_down, g_final):
    depth = w_in.shape[0]
    b, s, d = x_prompt.shape
    db, ds, _ = x_sample.shape
    assert depth == 1 and ds == 1 and d == D_MODEL, "kernel is written for one layer and one new token per sequence"
    assert s % 512 == 0
    tm = 512
    tm_in = 256
    tm_ffn = min(1024, s)
    gp = SSM_GROUPS * SSM_STATE
    kv_shape = (2, KV_GROUPS, HEAD_DIM)
    l = 0

    w_in_p = jnp.pad(w_in[l], ((0, 0), (0, IN_PAD - IN_WIDTH))).astype(BF16)
    w_out_b = w_out[l].astype(BF16)
    w_up_b = w_up[l].astype(BF16)
    w_down_b = w_down[l].astype(BF16)
    pwr, pwi, bbr, bbi = _s5_prep(ssm_lam_re[l], ssm_lam_im[l], ssm_log_dt[l], ssm_b_re[l], ssm_b_im[l])
    w1, w2, lvl_r, lvl_i, pw_r, pw_i = _s5_weights(pwr, pwi, bbr, bbi, ssm_c_re[l], ssm_c_im[l])
    wbd, pe8, wb = _cmp_weights(w_cmp[l], pe_cmp[l])

    n_c = b + db
    n_c_pad = -(-n_c // SUBLANES) * SUBLANES
    c_all = jnp.pad(jnp.concatenate([c_prompt, c_sample], axis=0), ((0, n_c_pad - n_c), (0, 0)))
    mod = _ada(c_all, w_ada[l], b_ada[l]).reshape(n_c_pad, 6, d)
    mod_p = [mod[:b, k].reshape(b, 1, d) for k in range(6)]
    mod_s = [mod[b:n_c, k].reshape(1, db, d) for k in range(6)]

    xp = x_prompt.reshape(b * s, d)
    za, kvc, kvs, kvw, graw = _inproj(xp, mod_p[1], mod_p[0], g_norm1[l], w_in_p, tm_in, s)
    m_ssm, st_re, st_im = _s5_prompt(za, b, s, w1, w2, lvl_r, lvl_i, pw_r, pw_i, ssm_d[l], g_out_ssm[l], 256)
    kc, vct = _cmpproj(kvc, b, s, wbd, pe8, wb)
    kvc_t, kvs_t, ks_b, vst, kw_b, vwt = _kvprep(kvc, kvs, kvw, b, s, 512)
    rows_major = lambda a: jnp.transpose(a, (0, 4, 1, 2, 3))[None]
    m_att = _nsa_prompt(za, graw, ks_b, vst, kw_b, vwt, kc, vct, g_out_att[l], b, s)
    x1, h2 = _outproj(xp, m_ssm, m_att, w_out_b, mod_p[2], mod_p[4], mod_p[3], g_norm2[l], tm, s)
    y_p, conv_p = _ffn_seq(h2, x1, mod_p[5], w_up_b, conv_w[l], conv_b[l], w_down_b, g_final, tm_ffn, s)
    wlen = min(WINDOW, s)
    win_p = kvw.reshape(b, s, *kv_shape)[:, s - wlen:]

    xs = x_sample.reshape(db, d)
    za_s, kvc_s, kvs_s, kvw_s, graw_s = _inproj(xs, mod_s[1], mod_s[0], g_norm1[l], w_in_p, db, db)
    m_ssm_s, st_re_s, st_im_s = _s5_sample(za_s, state_ssm_re[l].reshape(db, gp), state_ssm_im[l].reshape(db, gp),
                                           w1, w2, pw_r, pw_i, ssm_d[l], g_out_ssm[l])
    rows_minor = lambda c: jnp.transpose(c, (0, 2, 3, 4, 1))
    m_att_s, win_s = _nsa_sample(za_s, graw_s, kvs_s, kvw_s, rows_minor(cache_cmp_kv[l]),
                                 rows_minor(cache_slc_kv[l]), rows_minor(cache_win_kv[l]),
                                 page_table, wbd, pe8, wb, g_out_att[l])
    win_s = jnp.transpose(win_s, (0, 4, 1, 2, 3))
    x1_s, h2_s = _outproj(xs, m_ssm_s, m_att_s, w_out_b, mod_s[2], mod_s[4], mod_s[3], g_norm2[l], db, db)
    y_s, conv_s = _ffn_step(h2_s, x1_s, mod_s[5], w_up_b, conv_w[l], conv_b[l], w_down_b, g_final, state_conv[l])

    wbuf = cache_win_kv.shape[2]
    return (y_p.reshape(b, s, d), y_s.reshape(db, 1, d),
            rows_major(kvc_t), kvc_s.reshape(1, db, 1, *kv_shape),
            rows_major(kvs_t), kvs_s.reshape(1, db, 1, *kv_shape),
            win_p[None], win_s.reshape(1, db, wbuf, *kv_shape),
            st_re.reshape(1, b, SSM_GROUPS, SSM_STATE), st_im.reshape(1, b, SSM_GROUPS, SSM_STATE),
            st_re_s.reshape(1, db, SSM_GROUPS, SSM_STATE), st_im_s.reshape(1, db, SSM_GROUPS, SSM_STATE),
            conv_p[None], conv_s[None])
```

```python
import functools
import math

import jax
import jax.numpy as jnp
import numpy as np
from jax import lax
from jax.experimental import pallas as pl
from jax.experimental.pallas import tpu as pltpu

F32 = jnp.float32
BF16 = jnp.bfloat16

D_MODEL = 2048
SSM_WIDTH = D_MODEL // 2
ATT_WIDTH = D_MODEL - SSM_WIDTH
SSM_CH = 16
SSM_GROUPS = SSM_WIDTH // SSM_CH
SSM_STATE = 64
HEAD_DIM = 64
N_HEADS = ATT_WIDTH // HEAD_DIM
KV_GROUPS = 4
HEADS_PER_GROUP = N_HEADS // KV_GROUPS
KV_WIDTH = KV_GROUPS * HEAD_DIM
CMP_STRIDE = 16
CMP_BLOCK = 32
SLC_BLOCK = 64
N_SEL = 16
WINDOW = 512
PAGE_SIZE = 128
ATTN_SCALE = HEAD_DIM ** -0.5
NEG_INF = -1e30
FORCE_BONUS = 1e4
D_FF = 256 * ((8 * D_MODEL // 3 + 255) // 256)
CONV_W = 3
EPS = 1e-6
IN_WIDTH = 3 * SSM_WIDTH + 6 * KV_WIDTH + 3 * N_HEADS

LANES = 128
SUBLANES = 8
VMEM_LIMIT = 56 * 1024 * 1024

IN_TN = 512
IN_NA = 3 * SSM_WIDTH // IN_TN
IN_PAD = (IN_NA + 4) * IN_TN
GATE_PAD = LANES
SSM_LCH = 8 * SSM_STATE
SSM_NCH = SSM_GROUPS // 8
Q_TILE = 128
SEL_KT = 256
WIN_KT = 128
CMP_PG = 32
FFN_TF = 512
FFN_NF = D_FF // FFN_TF
FFN_RS = 256


def _cparams(sem):
    return pltpu.CompilerParams(dimension_semantics=sem, vmem_limit_bytes=VMEM_LIMIT)


def _rms(x, g):
    return x * lax.rsqrt(jnp.mean(x * x, axis=-1, keepdims=True) + EPS) * g


def _ada_kernel(c_ref, w_ref, b_ref, o_ref):
    c = c_ref[...]
    a = (c * jax.nn.sigmoid(c)).astype(BF16)
    o_ref[...] = jnp.dot(a, w_ref[...].astype(BF16), preferred_element_type=F32) + b_ref[...]


def _ada(c_all, w_ada, b_ada):
    r, d = c_all.shape
    n = w_ada.shape[1]
    tn = 1024
    return pl.pallas_call(
        _ada_kernel,
        grid=(n // tn,),
        in_specs=[pl.BlockSpec((r, d), lambda j: (0, 0)),
                  pl.BlockSpec((d, tn), lambda j: (0, j)),
                  pl.BlockSpec((1, tn), lambda j: (0, j))],
        out_specs=pl.BlockSpec((r, tn), lambda j: (0, j)),
        out_shape=jax.ShapeDtypeStruct((r, n), F32),
        compiler_params=_cparams(("arbitrary",)),
        name="ada",
    )(c_all, w_ada, b_ada.reshape(1, n))


def _inproj_kernel(x_ref, sc_ref, sh_ref, g_ref, w_ref, za_ref, kc_ref, ks_ref, kw_ref, gt_ref):
    h = (_rms(x_ref[...], g_ref[...]) * (1.0 + sc_ref[...]) + sh_ref[...]).astype(BF16)
    for j in range(IN_PAD // IN_TN):
        z = jnp.dot(h, w_ref[:, j * IN_TN:(j + 1) * IN_TN], preferred_element_type=F32)
        if j < IN_NA:
            za_ref[:, j * IN_TN:(j + 1) * IN_TN] = z
        elif j < IN_NA + 3:
            (kc_ref, ks_ref, kw_ref)[j - IN_NA][...] = z
        else:
            gt_ref[...] = z[:, :GATE_PAD]


def _inproj(x, sc, sh, g, w_pad, tm, rows_per_mod):
    n, d = x.shape
    r = sc.shape[1]
    tpm = rows_per_mod // tm
    mod_spec = pl.BlockSpec((None, r, d), lambda i: (i // tpm, 0, 0))
    row = lambda w: pl.BlockSpec((tm, w), lambda i: (i, 0))
    return pl.pallas_call(
        _inproj_kernel,
        grid=(n // tm,),
        in_specs=[row(d), mod_spec, mod_spec, pl.BlockSpec((1, d), lambda i: (0, 0)),
                  pl.BlockSpec((d, IN_PAD), lambda i: (0, 0), pipeline_mode=pl.Buffered(1))],
        out_specs=[row(3 * SSM_WIDTH), row(2 * KV_WIDTH), row(2 * KV_WIDTH), row(2 * KV_WIDTH), row(GATE_PAD)],
        out_shape=[jax.ShapeDtypeStruct((n, 3 * SSM_WIDTH), F32),
                   jax.ShapeDtypeStruct((n, 2 * KV_WIDTH), F32),
                   jax.ShapeDtypeStruct((n, 2 * KV_WIDTH), F32),
                   jax.ShapeDtypeStruct((n, 2 * KV_WIDTH), F32),
                   jax.ShapeDtypeStruct((n, GATE_PAD), F32)],
        compiler_params=_cparams(("arbitrary",)),
        name="inproj",
    )(x, sc, sh, g.reshape(1, d), w_pad)


def _s5_prep_kernel(lre_ref, lim_ref, ldt_ref, lrex_ref, limx_ref, bre_ref, bim_ref,
                    pwr_ref, pwi_ref, bbr_ref, bbi_ref):
    dt = jnp.exp(ldt_ref[...])

    def disc(lre, lim):
        mag = jnp.exp(lre * dt)
        ab_re = mag * jnp.cos(lim * dt)
        ab_im = mag * jnp.sin(lim * dt)
        den = lre * lre + lim * lim
        f_re = ((ab_re - 1.0) * lre + ab_im * lim) / den
        f_im = (ab_im * lre - (ab_re - 1.0) * lim) / den
        return ab_re, ab_im, f_re, f_im

    ab_re, ab_im, _, _ = disc(lre_ref[...], lim_ref[...])
    pr, pi = ab_re, ab_im
    pwr_ref[0] = pr
    pwi_ref[0] = pi
    for k in range(1, SUBLANES):
        pr, pi = pr * ab_re - pi * ab_im, pr * ab_im + pi * ab_re
        pwr_ref[k] = pr
        pwi_ref[k] = pi
    _, _, f_re, f_im = disc(lrex_ref[...], limx_ref[...])
    b_re, b_im = bre_ref[...], bim_ref[...]
    bbr_ref[...] = f_re * b_re - f_im * b_im
    bbi_ref[...] = f_re * b_im + f_im * b_re


def _s5_prep(lam_re, lam_im, log_dt, b_re, b_im):
    g, p = lam_re.shape
    ch = b_re.shape[-1]
    lrex = jnp.repeat(lam_re, ch, axis=1)
    limx = jnp.repeat(lam_im, ch, axis=1)
    full = lambda shape: pl.BlockSpec(shape, lambda: (0,) * len(shape))
    return pl.pallas_call(
        _s5_prep_kernel,
        in_specs=[full((g, p)), full((g, p)), full((g, 1)), full((g, p * ch)), full((g, p * ch)),
                  full((g, p * ch)), full((g, p * ch))],
        out_specs=[full((SUBLANES, g, p)), full((SUBLANES, g, p)), full((g, p * ch)), full((g, p * ch))],
        out_shape=[jax.ShapeDtypeStruct((SUBLANES, g, p), F32), jax.ShapeDtypeStruct((SUBLANES, g, p), F32),
                   jax.ShapeDtypeStruct((g, p * ch), F32), jax.ShapeDtypeStruct((g, p * ch), F32)],
        name="s5_prep",
    )(lam_re, lam_im, log_dt.reshape(g, 1), lrex, limx, b_re.reshape(g, p * ch), b_im.reshape(g, p * ch))


def _s5_weights(pwr, pwi, bbr, bbi, c_re, c_im):
    g, p, ch = SSM_GROUPS, SSM_STATE, SSM_CH
    eye = jnp.eye(8, dtype=F32)

    def w_in(bb):
        bb = bb.reshape(SSM_NCH, 8, p, ch)
        return jnp.einsum('jgpc,gh->jgchp', bb, eye).reshape(SSM_NCH, 8 * ch, 8 * p)

    def w_out(c):
        c = c.reshape(SSM_NCH, 8, ch, p)
        return jnp.einsum('jgcp,gh->jgphc', c, eye).reshape(SSM_NCH, 8 * p, 8 * ch)

    w1 = jnp.concatenate([w_in(bbr), w_in(bbi)], axis=-1).astype(BF16)
    w2 = jnp.concatenate([w_out(c_re), w_out(-c_im)], axis=1).astype(BF16)
    pw_r = pwr.reshape(SUBLANES, g * p)
    pw_i = pwi.reshape(SUBLANES, g * p)
    tau = jnp.arange(SUBLANES)[:, None]
    lvl_r = jnp.stack([jnp.where(tau >= d, pw_r[d - 1][None, :], 0.0) for d in (1, 2, 4)])
    lvl_i = jnp.stack([jnp.where(tau >= d, pw_i[d - 1][None, :], 0.0) for d in (1, 2, 4)])
    return w1, w2, lvl_r, lvl_i, pw_r, pw_i


def _s5_post(y, u, g_glu, d_skip, g_out):
    y = y + d_skip * u
    y = jax.nn.gelu(y) * jax.nn.sigmoid(g_glu)
    return _rms(y, g_out).astype(BF16)


def _s5_scan_kernel(u_ref, gg_ref, w1_ref, w2_ref, lr_ref, li_ref, pr_ref, pi_ref, d_ref, go_ref,
                    o_ref, sr_ref, si_ref, br_ref, bi_ref, y_ref, hr_ref, hi_ref, *, tt):
    t = pl.program_id(1)
    nrt = tt // SUBLANES

    @pl.when(t == 0)
    def _():
        hr_ref[...] = jnp.zeros_like(hr_ref)
        hi_ref[...] = jnp.zeros_like(hi_ref)

    def chunk(j, carry):
        lo = pl.multiple_of(j * SSM_LCH, SSM_LCH)
        uo = pl.multiple_of(j * LANES, LANES)
        ub = u_ref[:, pl.ds(uo, LANES)].astype(BF16)
        bu = jnp.dot(ub, w1_ref[j], preferred_element_type=F32)
        xr = bu[:, :SSM_LCH].reshape(nrt, SUBLANES, SSM_LCH)
        xi = bu[:, SSM_LCH:].reshape(nrt, SUBLANES, SSM_LCH)
        for lvl, d in enumerate((1, 2, 4)):
            ar = lr_ref[lvl, :, pl.ds(lo, SSM_LCH)]
            ai = li_ref[lvl, :, pl.ds(lo, SSM_LCH)]
            zr = pltpu.roll(xr, d, 1)
            zi = pltpu.roll(xi, d, 1)
            xr, xi = xr + ar * zr - ai * zi, xi + ar * zi + ai * zr
        br_ref[...] = xr.reshape(tt, SSM_LCH)
        bi_ref[...] = xi.reshape(tt, SSM_LCH)
        pr = pr_ref[:, pl.ds(lo, SSM_LCH)]
        pi = pi_ref[:, pl.ds(lo, SSM_LCH)]

        def tile(k, h):
            hr, hi = h
            r0 = pl.multiple_of(k * SUBLANES, SUBLANES)
            vr = br_ref[pl.ds(r0, SUBLANES), :] + pr * hr - pi * hi
            vi = bi_ref[pl.ds(r0, SUBLANES), :] + pr * hi + pi * hr
            br_ref[pl.ds(r0, SUBLANES), :] = vr
            bi_ref[pl.ds(r0, SUBLANES), :] = vi
            return vr[SUBLANES - 1:, :], vi[SUBLANES - 1:, :]

        hr, hi = lax.fori_loop(0, nrt, tile, (hr_ref[:, pl.ds(lo, SSM_LCH)], hi_ref[:, pl.ds(lo, SSM_LCH)]))
        hr_ref[:, pl.ds(lo, SSM_LCH)] = hr
        hi_ref[:, pl.ds(lo, SSM_LCH)] = hi
        hcat = jnp.concatenate([br_ref[...], bi_ref[...]], axis=1).astype(BF16)
        y_ref[:, pl.ds(uo, LANES)] = jnp.dot(hcat, w2_ref[j], preferred_element_type=F32)
        return carry

    lax.fori_loop(0, SSM_NCH, chunk, 0)
    o_ref[...] = _s5_post(y_ref[...], u_ref[...], gg_ref[...], d_ref[...], go_ref[...])
    sr_ref[...] = hr_ref[...]
    si_ref[...] = hi_ref[...]


def _s5_prompt(za, b, s, w1, w2, lvl_r, lvl_i, pw_r, pw_i, d_skip, g_out, tt):
    nt = s // tt
    gp = SSM_GROUPS * SSM_STATE
    const2 = lambda shape: pl.BlockSpec(shape, lambda bi, ti: (0,) * len(shape))
    st_spec = pl.BlockSpec((None, 1, gp), lambda bi, ti: (bi, 0, 0))
    return pl.pallas_call(
        functools.partial(_s5_scan_kernel, tt=tt),
        grid=(b, nt),
        in_specs=[pl.BlockSpec((tt, SSM_WIDTH), lambda bi, ti: (bi * nt + ti, 0)),
                  pl.BlockSpec((tt, SSM_WIDTH), lambda bi, ti: (bi * nt + ti, 1)),
                  const2(w1.shape), const2(w2.shape), const2(lvl_r.shape), const2(lvl_i.shape),
                  const2(pw_r.shape), const2(pw_i.shape), const2((1, SSM_WIDTH)), const2((1, SSM_WIDTH))],
        out_specs=[pl.BlockSpec((tt, SSM_WIDTH), lambda bi, ti: (bi * nt + ti, 0)), st_spec, st_spec],
        out_shape=[jax.ShapeDtypeStruct((b * s, SSM_WIDTH), BF16),
                   jax.ShapeDtypeStruct((b, 1, gp), F32), jax.ShapeDtypeStruct((b, 1, gp), F32)],
        scratch_shapes=[pltpu.VMEM((tt, SSM_LCH), F32), pltpu.VMEM((tt, SSM_LCH), F32),
                        pltpu.VMEM((tt, SSM_WIDTH), F32),
                        pltpu.VMEM((1, gp), F32), pltpu.VMEM((1, gp), F32)],
        compiler_params=_cparams(("arbitrary", "arbitrary")),
        name="s5_prompt",
    )(za, za, w1, w2, lvl_r, lvl_i, pw_r, pw_i, d_skip.reshape(1, -1), g_out.reshape(1, -1))


def _s5_step_kernel(u_ref, gg_ref, h0r_ref, h0i_ref, w1_ref, w2_ref, pr_ref, pi_ref, d_ref, go_ref,
                    o_ref, sr_ref, si_ref, y_ref):
    for j in range(SSM_NCH):
        lo, uo = j * SSM_LCH, j * LANES
        bu = jnp.dot(u_ref[:, uo:uo + LANES].astype(BF16), w1_ref[j], preferred_element_type=F32)
        ar = pr_ref[0:1, lo:lo + SSM_LCH]
        ai = pi_ref[0:1, lo:lo + SSM_LCH]
        h0r = h0r_ref[:, lo:lo + SSM_LCH]
        h0i = h0i_ref[:, lo:lo + SSM_LCH]
        hr = bu[:, :SSM_LCH] + (ar * h0r - ai * h0i)
        hi = bu[:, SSM_LCH:] + (ar * h0i + ai * h0r)
        sr_ref[:, lo:lo + SSM_LCH] = hr
        si_ref[:, lo:lo + SSM_LCH] = hi
        hcat = jnp.concatenate([hr, hi], axis=1).astype(BF16)
        y_ref[:, uo:uo + LANES] = jnp.dot(hcat, w2_ref[j], preferred_element_type=F32)
    o_ref[...] = _s5_post(y_ref[...], u_ref[...], gg_ref[...], d_ref[...], go_ref[...])


def _s5_sample(za, h0r, h0i, w1, w2, pw_r, pw_i, d_skip, g_out):
    n = za.shape[0]
    gp = SSM_GROUPS * SSM_STATE
    full = lambda shape: pl.BlockSpec(shape, lambda i: (0,) * len(shape))
    return pl.pallas_call(
        _s5_step_kernel,
        grid=(1,),
        in_specs=[pl.BlockSpec((n, SSM_WIDTH), lambda i: (0, 0)), pl.BlockSpec((n, SSM_WIDTH), lambda i: (0, 1)),
                  full((n, gp)), full((n, gp)), full(w1.shape), full(w2.shape),
                  full(pw_r.shape), full(pw_i.shape), full((1, SSM_WIDTH)), full((1, SSM_WIDTH))],
        out_specs=[full((n, SSM_WIDTH)), full((n, gp)), full((n, gp))],
        out_shape=[jax.ShapeDtypeStruct((n, SSM_WIDTH), BF16),
                   jax.ShapeDtypeStruct((n, gp), F32), jax.ShapeDtypeStruct((n, gp), F32)],
        scratch_shapes=[pltpu.VMEM((n, SSM_WIDTH), F32)],
        compiler_params=_cparams(("arbitrary",)),
        name="s5_sample",
    )(za, za, h0r, h0i, w1, w2, pw_r, pw_i, d_skip.reshape(1, -1), g_out.reshape(1, -1))


def _cmp_weights(w_cmp, pe_cmp):
    eye = jnp.eye(KV_GROUPS, dtype=F32)

    def bd(w):
        return jnp.einsum('sjde,gh->sjgdhe', w, eye).reshape(2, CMP_STRIDE, KV_WIDTH, KV_WIDTH)

    wbd = jnp.concatenate([bd(w_cmp[:, :CMP_STRIDE]), bd(w_cmp[:, CMP_STRIDE:])], axis=-1).astype(BF16)
    pe8 = jnp.zeros((2, SUBLANES, CMP_BLOCK * HEAD_DIM), F32).at[:, 0].set(pe_cmp.reshape(2, -1)).astype(BF16)
    wb = jnp.tile(w_cmp.reshape(2, CMP_BLOCK * HEAD_DIM, HEAD_DIM), (1, 1, KV_GROUPS)).astype(BF16)
    return wbd, pe8, wb


def _split3(x):
    a = x.astype(BF16)
    r = x - a.astype(F32)
    b = r.astype(BF16)
    c = (r - b.astype(F32)).astype(BF16)
    return a, b, c


def _dot_exact_left(e, x):
    a, b, c = _split3(x)
    return (jnp.dot(e, a, preferred_element_type=F32) + jnp.dot(e, b, preferred_element_type=F32)
            + jnp.dot(e, c, preferred_element_type=F32))


def _dot_exact_right(x, e):
    a, b, c = _split3(x)
    return (jnp.dot(a, e, preferred_element_type=F32) + jnp.dot(b, e, preferred_element_type=F32)
            + jnp.dot(c, e, preferred_element_type=F32))


def _rank(score, n_cand):
    n_tiles = score.shape[0] // SUBLANES
    tiles = [score[v * SUBLANES:(v + 1) * SUBLANES] for v in range(n_tiles)]
    ranks = [jnp.zeros(t.shape, F32) for t in tiles]
    row = lax.broadcasted_iota(jnp.int32, (SUBLANES, 1), 0)
    for i in range(n_cand):
        si = tiles[i // SUBLANES][i % SUBLANES:i % SUBLANES + 1]
        for v in range(n_tiles):
            ge = jnp.where(si >= tiles[v], 1.0, 0.0)
            gt = jnp.where(si > tiles[v], 1.0, 0.0)
            if i < v * SUBLANES:
                beats = ge
            elif i >= (v + 1) * SUBLANES:
                beats = gt
            else:
                beats = jnp.where(row > i % SUBLANES, ge, gt)
            ranks[v] = ranks[v] + beats
    return jnp.concatenate(ranks, axis=0)


def _col_reduce(x, op, final):
    slabs = [x[i * SUBLANES:(i + 1) * SUBLANES] for i in range(x.shape[0] // SUBLANES)]
    while len(slabs) > 1:
        pairs = [op(slabs[i], slabs[i + 1]) for i in range(0, len(slabs) - 1, 2)]
        slabs = pairs + ([slabs[-1]] if len(slabs) % 2 else [])
    return final(slabs[0], axis=0, keepdims=True)


def _softmax_stage(g, s, bias, pv_prev, m_ref, l_ref, acc_ref, p_ref, rows):
    if bias is not None:
        s = s + jnp.concatenate([bias] * HEADS_PER_GROUP, axis=1)
    m_old = m_ref[g]
    m_new = jnp.maximum(m_old, _col_reduce(s, jnp.maximum, jnp.max))
    alpha = jnp.exp(m_old - m_new)
    p = jnp.exp(s - m_new)
    l_ref[g] = alpha * l_ref[g] + _col_reduce(p, jnp.add, jnp.sum)
    acc_ref[g] = alpha * (acc_ref[g] + pv_prev)
    m_ref[g] = m_new
    p_ref[g, rows, :] = p.astype(BF16)


def _cmpproj_kernel(x_ref, wbd_ref, pe_ref, wb_ref, kc_ref, vct_ref):
    nch = x_ref.shape[0]
    outs = []
    for s in range(2):
        acc = jnp.zeros((nch, 2 * KV_WIDTH), F32)
        for j in range(CMP_STRIDE):
            c0 = j * 2 * KV_WIDTH + s * KV_WIDTH
            acc = acc + jnp.dot(x_ref[:, c0:c0 + KV_WIDTH].astype(BF16), wbd_ref[s, j],
                                preferred_element_type=F32)
        bias = jnp.dot(pe_ref[s], wb_ref[s], preferred_element_type=F32)[0:1]
        outs.append(acc[:, :KV_WIDTH] + pltpu.roll(acc[:, KV_WIDTH:], nch - 1, 0) + bias)
    kc_ref[...] = outs[0].astype(BF16)
    vct_ref[...] = jnp.transpose(outs[1]).reshape(KV_GROUPS, HEAD_DIM, nch).astype(BF16)


def _cmpproj(kvc, b, s, wbd, pe8, wb):
    nch = s // CMP_STRIDE
    x = kvc.reshape(b, nch, CMP_STRIDE * 2 * KV_WIDTH)
    const = lambda shape: pl.BlockSpec(shape, lambda bi: (0,) * len(shape))
    return pl.pallas_call(
        _cmpproj_kernel,
        grid=(b,),
        in_specs=[pl.BlockSpec((None, nch, CMP_STRIDE * 2 * KV_WIDTH), lambda bi: (bi, 0, 0)),
                  const(wbd.shape), const(pe8.shape), const(wb.shape)],
        out_specs=[pl.BlockSpec((None, nch, KV_WIDTH), lambda bi: (bi, 0, 0)),
                   pl.BlockSpec((None, KV_GROUPS, HEAD_DIM, nch), lambda bi: (bi, 0, 0, 0))],
        out_shape=[jax.ShapeDtypeStruct((b, nch, KV_WIDTH), BF16),
                   jax.ShapeDtypeStruct((b, KV_GROUPS, HEAD_DIM, nch), BF16)],
        compiler_params=_cparams(("arbitrary",)),
        name="cmpproj",
    )(x, wbd, pe8, wb)


def _kvprep_kernel(kc_ref, ks_ref, kw_ref, kct_ref, kst_ref, ksa_ref, vsto_ref, kwo_ref, vwto_ref):
    tk = ks_ref.shape[0]
    kct_ref[...] = jnp.transpose(kc_ref[...]).reshape(2, KV_GROUPS, HEAD_DIM, tk)
    xs = ks_ref[...]
    xst = jnp.transpose(xs).reshape(2, KV_GROUPS, HEAD_DIM, tk)
    kst_ref[...] = xst
    vsto_ref[...] = xst[1].astype(BF16)
    row = lax.broadcasted_iota(jnp.int32, (tk, HEAD_DIM), 0)
    lane = lax.broadcasted_iota(jnp.int32, (tk, HEAD_DIM), 1)
    onehot = jnp.where(lane == (row // SLC_BLOCK) % (SEL_KT // SLC_BLOCK), 1.0, 0.0)
    ksa_ref[...] = jnp.concatenate(
        [piece for g in range(KV_GROUPS) for piece in (xs[:, g * HEAD_DIM:(g + 1) * HEAD_DIM], onehot)],
        axis=1).astype(BF16)
    xw = kw_ref[...]
    kwo_ref[...] = xw[:, :KV_WIDTH].astype(BF16)
    vwto_ref[...] = jnp.transpose(xw[:, KV_WIDTH:]).reshape(KV_GROUPS, HEAD_DIM, tk).astype(BF16)


def _kvprep(kvc, kvs, kvw, b, s, tk):
    nt = s // tk
    in_spec = pl.BlockSpec((tk, 2 * KV_WIDTH), lambda bi, ti: (bi * nt + ti, 0))
    t_spec = pl.BlockSpec((None, 2, KV_GROUPS, HEAD_DIM, tk), lambda bi, ti: (bi, 0, 0, 0, ti))
    t_shape = jax.ShapeDtypeStruct((b, 2, KV_GROUPS, HEAD_DIM, s), F32)
    vt_spec = pl.BlockSpec((None, KV_GROUPS, HEAD_DIM, tk), lambda bi, ti: (bi, 0, 0, ti))
    vt_shape = jax.ShapeDtypeStruct((b, KV_GROUPS, HEAD_DIM, s), BF16)
    return pl.pallas_call(
        _kvprep_kernel,
        grid=(b, nt),
        in_specs=[in_spec, in_spec, in_spec],
        out_specs=[t_spec, t_spec,
                   pl.BlockSpec((None, tk, KV_GROUPS * LANES), lambda bi, ti: (bi, ti, 0)), vt_spec,
                   pl.BlockSpec((None, tk, KV_WIDTH), lambda bi, ti: (bi, ti, 0)), vt_spec],
        out_shape=[t_shape, t_shape, jax.ShapeDtypeStruct((b, s, KV_GROUPS * LANES), BF16), vt_shape,
                   jax.ShapeDtypeStruct((b, s, KV_WIDTH), BF16), vt_shape],
        compiler_params=_cparams(("arbitrary", "arbitrary")),
        name="kvprep",
    )(kvc, kvs, kvw)


def _nsa_prompt_kernel(q_ref, gt_ref, ks_ref, vst_ref, kw_ref, vwt_ref, kc_ref, vct_ref, esum_ref, go_ref,
                       o_ref, yt_ref, qst_ref, selb_ref, oc_ref, os_ref, m_ref, l_ref, acc_ref,
                       sa_ref, sb_ref, pa_ref, pb_ref, rhs_ref, *, n_blk, n_sel):
    qi = pl.program_id(1)
    q0 = qi * Q_TILE
    ncb = kc_ref.shape[0]
    bpt = SEL_KT // SLC_BLOCK
    tcol = lax.broadcasted_iota(jnp.int32, (1, Q_TILE), 1)
    tpos = q0 + tcol
    tpos4 = jnp.concatenate([tpos] * HEADS_PER_GROUP, axis=1)
    cend = lax.broadcasted_iota(jnp.int32, (ncb, 1), 0) * CMP_STRIDE + (CMP_BLOCK - 1)
    blk = lax.broadcasted_iota(jnp.int32, (n_blk, 1), 0)
    cur = tpos // SLC_BLOCK
    valid = blk <= cur
    forced = (blk == 0) | (blk == cur) | (blk == cur - 1)

    def keys(ref, k0, n, g):
        lane0 = (g // 2) * LANES
        return ref[pl.ds(k0, n), lane0:lane0 + LANES][:, (g % 2) * HEAD_DIM:(g % 2 + 1) * HEAD_DIM]

    def reset():
        m_ref[...] = jnp.full(m_ref.shape, NEG_INF, F32)
        l_ref[...] = jnp.zeros(l_ref.shape, F32)
        acc_ref[...] = jnp.zeros(acc_ref.shape, F32)

    for g in range(KV_GROUPS):
        qt = jnp.transpose(q_ref[:, g * 256:(g + 1) * 256] * ATTN_SCALE)
        qst = jnp.concatenate([qt[r * HEAD_DIM:(r + 1) * HEAD_DIM] for r in range(HEADS_PER_GROUP)],
                              axis=1).astype(BF16)
        qst_ref[g] = qst
        rhs_ref[g, 0:HEAD_DIM, :] = qst
        rhs_ref[g, HEAD_DIM:, :] = jnp.zeros((LANES - HEAD_DIM, HEADS_PER_GROUP * Q_TILE), BF16)
        sc = jnp.dot(keys(kc_ref, 0, ncb, g), qst, preferred_element_type=F32)
        mcf = jnp.where(cend <= tpos4, 1.0, 0.0)
        sc = jnp.where(mcf > 0.5, sc, NEG_INF)
        p = jnp.exp(sc - jnp.max(sc, axis=0, keepdims=True)) * mcf
        p = p / jnp.maximum(jnp.sum(p, axis=0, keepdims=True), 1e-30)
        oc_ref[g] = jnp.dot(vct_ref[g], p.astype(BF16), preferred_element_type=F32)
        imp = p[:, 0:Q_TILE]
        for r in range(1, HEADS_PER_GROUP):
            imp = imp + p[:, r * Q_TILE:(r + 1) * Q_TILE]
        impb = _dot_exact_left(esum_ref[...], imp)
        score = jnp.where(valid, impb + jnp.where(forced, FORCE_BONUS, 0.0), NEG_INF)
        selb_ref[g] = jnp.where(_rank(score, n_blk) < n_sel, 0.0, NEG_INF)

    bufs = ((sa_ref, pa_ref), (sb_ref, pb_ref))
    srows = slice(0, WIN_KT)
    krow = lax.broadcasted_iota(jnp.int32, (WIN_KT, 1), 0)
    zeros_pv = [0.0] * KV_GROUPS

    def static_tiles(tiles, pv):
        def scores(t, s_ref):
            for g in range(KV_GROUPS):
                s_ref[g, srows, :] = jnp.dot(tiles[t][0](g), qst_ref[g], preferred_element_type=F32)

        def pvs(t, p_ref):
            return [jnp.dot(tiles[t][1](g), p_ref[g, srows, :], preferred_element_type=F32)
                    for g in range(KV_GROUPS)]

        scores(0, sa_ref)
        for t in range(len(tiles)):
            (s_cur, p_cur), (s_nxt, p_prv) = bufs[t % 2], bufs[(t + 1) % 2]
            if t + 1 < len(tiles):
                scores(t + 1, s_nxt)
            if t > 0:
                pv = pvs(t - 1, p_prv)
            for g in range(KV_GROUPS):
                _softmax_stage(g, s_cur[g, srows, :], tiles[t][2](g), pv[g], m_ref, l_ref, acc_ref, p_cur, srows)
        return pvs(len(tiles) - 1, bufs[(len(tiles) - 1) % 2][1])

    n_full = qi // (SEL_KT // Q_TILE)
    last_full = jnp.maximum(n_full - 1, 0)

    def sel_scores(kt, s_ref):
        kc = jnp.clip(kt, 0, last_full)
        k0 = pl.multiple_of(kc * SEL_KT, SEL_KT)
        live = kt < n_full
        pad = jnp.zeros((2 * SUBLANES - bpt, HEADS_PER_GROUP * Q_TILE), F32)
        for g in range(KV_GROUPS):
            rows = jnp.where(live, selb_ref[g, pl.ds(kc * bpt, bpt), :], NEG_INF)
            rows = jnp.concatenate([jnp.concatenate([rows] * HEADS_PER_GROUP, axis=1), pad], axis=0)
            rhs_ref[g, HEAD_DIM:HEAD_DIM + 2 * SUBLANES, :] = rows.astype(BF16)
            s_ref[g] = jnp.dot(ks_ref[pl.ds(k0, SEL_KT), g * LANES:(g + 1) * LANES], rhs_ref[g],
                               preferred_element_type=F32)

    def sel_pv(kt, p_ref):
        k0 = pl.multiple_of(jnp.clip(kt, 0, last_full) * SEL_KT, SEL_KT)
        return [jnp.dot(vst_ref[g, :, pl.ds(k0, SEL_KT)], p_ref[g], preferred_element_type=F32)
                for g in range(KV_GROUPS)]

    def sel_step(kt, s_cur, s_nxt, p_cur, p_prv):
        sel_scores(kt + 1, s_nxt)
        pv = sel_pv(kt - 1, p_prv)
        for g in range(KV_GROUPS):
            _softmax_stage(g, s_cur[g], None, pv[g], m_ref, l_ref, acc_ref, p_cur, slice(0, SEL_KT))

    def sel_pair(i, carry):
        sel_step(2 * i, sa_ref, sb_ref, pa_ref, pb_ref)
        sel_step(2 * i + 1, sb_ref, sa_ref, pb_ref, pa_ref)
        return carry

    reset()
    pb_ref[...] = jnp.zeros(pb_ref.shape, BF16)
    sel_scores(0, sa_ref)
    n_pairs = (n_full + 1) // 2
    lax.fori_loop(0, n_pairs, sel_pair, 0)
    pv = sel_pv(2 * n_pairs - 1, pb_ref)

    def block_bias(g, blk0):
        rows = selb_ref[g, pl.ds(blk0, WIN_KT // SLC_BLOCK), :]
        return jnp.concatenate([jnp.broadcast_to(rows[i:i + 1], (SLC_BLOCK, Q_TILE))
                                for i in range(WIN_KT // SLC_BLOCK)], axis=0)

    def sel_tail_tile(k0, blk0, extra):
        return (lambda g: ks_ref[pl.ds(k0, WIN_KT), g * LANES:(g + 1) * LANES][:, :HEAD_DIM],
                lambda g: vst_ref[g, :, pl.ds(k0, WIN_KT)],
                lambda g: block_bias(g, blk0) + extra)

    bpq = Q_TILE // SLC_BLOCK
    odd = jnp.where(qi % (SEL_KT // Q_TILE) == 1, 0.0, NEG_INF)
    prev0 = pl.multiple_of(jnp.maximum(qi - 1, 0) * Q_TILE, Q_TILE)
    pv = static_tiles([sel_tail_tile(pl.multiple_of(q0, Q_TILE), qi * bpq, jnp.where(krow <= tcol, 0.0, NEG_INF)),
                       sel_tail_tile(prev0, jnp.maximum(qi - 1, 0) * bpq, odd)], pv)
    for g in range(KV_GROUPS):
        os_ref[g] = (acc_ref[g] + pv[g]) / jnp.maximum(l_ref[g], 1e-30)

    reset()
    n_back = WINDOW // WIN_KT

    def win_tile(t):
        k0 = pl.multiple_of(jnp.maximum(qi - t, 0) * WIN_KT, WIN_KT)
        if t == 0:
            bias = jnp.where(krow <= tcol, 0.0, NEG_INF)
        else:
            off = jnp.where(qi - t >= 0, 0.0, NEG_INF)
            bias = jnp.where(krow > tcol, off, NEG_INF) if t == n_back else jnp.zeros((WIN_KT, Q_TILE), F32) + off
        return (lambda g: keys(kw_ref, k0, WIN_KT, g), lambda g: vwt_ref[g, :, pl.ds(k0, WIN_KT)], lambda g: bias)

    pv = static_tiles([win_tile(t) for t in range(n_back + 1)], zeros_pv)

    gates = jax.nn.sigmoid(jnp.transpose(gt_ref[...]))
    for g in range(KV_GROUPS):
        o_c, o_s = oc_ref[g], os_ref[g]
        o_w = (acc_ref[g] + pv[g]) / jnp.maximum(l_ref[g], 1e-30)
        for r in range(HEADS_PER_GROUP):
            cs = slice(r * Q_TILE, (r + 1) * Q_TILE)
            gi = (g * HEADS_PER_GROUP + r) * 3
            y = (gates[gi:gi + 1] * o_c[:, cs] + gates[gi + 1:gi + 2] * o_s[:, cs]
                 + gates[gi + 2:gi + 3] * o_w[:, cs])
            row0 = (g * HEADS_PER_GROUP + r) * HEAD_DIM
            yt_ref[row0:row0 + HEAD_DIM, :] = y

    o_ref[...] = _rms(jnp.transpose(yt_ref[...]), go_ref[...]).astype(BF16)


def _esum_matrix(n_blk, n_rows, row_of_block0):
    r = SLC_BLOCK // CMP_STRIDE
    j = np.arange(n_blk)[:, None]
    i = np.arange(n_rows)[None, :] - row_of_block0
    e = (i >= r * j - 1) & (i <= r * j + r - 1) & (i >= 0)
    return jnp.asarray(e, BF16)


def _nsa_prompt(za, gates, ks, vst, kw, vwt, kc, vct, g_out, b, s):
    assert WINDOW % WIN_KT == 0 and WIN_KT == Q_TILE and s % SEL_KT == 0
    nqt = s // Q_TILE
    nq = HEADS_PER_GROUP * Q_TILE
    n_blk = s // SLC_BLOCK
    n_sel = min(N_SEL, n_blk)
    ncb = s // CMP_STRIDE
    esum = _esum_matrix(n_blk, ncb, 0) * jnp.asarray(np.arange(ncb)[None, :] < ncb - 1, BF16)
    qcol = 2 * SSM_WIDTH // ATT_WIDTH
    per_b3 = lambda shape: pl.BlockSpec((None,) + shape, lambda bi, qi: (bi, 0, 0))
    per_b4 = lambda shape: pl.BlockSpec((None,) + shape, lambda bi, qi: (bi, 0, 0, 0))
    return pl.pallas_call(
        functools.partial(_nsa_prompt_kernel, n_blk=n_blk, n_sel=n_sel),
        grid=(b, nqt),
        in_specs=[pl.BlockSpec((Q_TILE, ATT_WIDTH), lambda bi, qi: (bi * nqt + qi, qcol)),
                  pl.BlockSpec((Q_TILE, GATE_PAD), lambda bi, qi: (bi * nqt + qi, 0)),
                  per_b3((s, KV_GROUPS * LANES)), per_b4((KV_GROUPS, HEAD_DIM, s)),
                  per_b3((s, KV_WIDTH)), per_b4((KV_GROUPS, HEAD_DIM, s)),
                  per_b3((ncb, KV_WIDTH)), per_b4((KV_GROUPS, HEAD_DIM, ncb)),
                  pl.BlockSpec((n_blk, ncb), lambda bi, qi: (0, 0)),
                  pl.BlockSpec((1, ATT_WIDTH), lambda bi, qi: (0, 0))],
        out_specs=pl.BlockSpec((Q_TILE, ATT_WIDTH), lambda bi, qi: (bi * nqt + qi, 0)),
        out_shape=jax.ShapeDtypeStruct((b * s, ATT_WIDTH), BF16),
        scratch_shapes=[pltpu.VMEM((ATT_WIDTH, Q_TILE), F32),
                        pltpu.VMEM((KV_GROUPS, HEAD_DIM, nq), BF16),
                        pltpu.VMEM((KV_GROUPS, n_blk, Q_TILE), F32),
                        pltpu.VMEM((KV_GROUPS, HEAD_DIM, nq), F32), pltpu.VMEM((KV_GROUPS, HEAD_DIM, nq), F32),
                        pltpu.VMEM((KV_GROUPS, 1, nq), F32), pltpu.VMEM((KV_GROUPS, 1, nq), F32),
                        pltpu.VMEM((KV_GROUPS, HEAD_DIM, nq), F32),
                        pltpu.VMEM((KV_GROUPS, SEL_KT, nq), F32), pltpu.VMEM((KV_GROUPS, SEL_KT, nq), F32),
                        pltpu.VMEM((KV_GROUPS, SEL_KT, nq), BF16), pltpu.VMEM((KV_GROUPS, SEL_KT, nq), BF16),
                        pltpu.VMEM((KV_GROUPS, LANES, nq), BF16)],
        compiler_params=_cparams(("arbitrary", "arbitrary")),
        name="nsa_prompt",
    )(za, gates, ks, vst, kw, vwt, kc, vct, esum, g_out.reshape(1, -1))


def _cmp_pages_kernel(pt_ref, *refs, pg):
    x_refs = refs[:pg]
    wbd_ref, pe_ref, wb_ref, kc_ref, vc_ref, xk_ref, xv_ref, carry_ref = refs[pg:]
    rows = pg * (PAGE_SIZE // CMP_STRIDE)
    h = pl.program_id(1)

    @pl.when(h == 0)
    def _():
        carry_ref[...] = jnp.zeros_like(carry_ref)

    cpp = PAGE_SIZE // CMP_STRIDE
    pitch = rows + SUBLANES
    def regroup(s, xs_ref, pages):
        for k in pages:
            for gp in range(KV_GROUPS // 2):
                t = jnp.transpose(x_refs[k][s, 2 * gp:2 * gp + 2].reshape(2 * HEAD_DIM, PAGE_SIZE))
                for n in range(cpp):
                    xs_ref[gp, pl.ds(k * cpp + n, CMP_STRIDE, stride=pitch), :] = (
                        t[n * CMP_STRIDE:(n + 1) * CMP_STRIDE])

    regroup(0, xk_ref, range(pg))
    ppj = -(-pg // CMP_STRIDE)
    row = lax.broadcasted_iota(jnp.int32, (rows, 1), 0)
    for s, out_ref, xs_ref in ((0, kc_ref, xk_ref), (1, vc_ref, xv_ref)):
        acc = jnp.zeros((rows, 2 * KV_WIDTH), F32)
        for j in range(CMP_STRIDE):
            xs = jnp.concatenate([xs_ref[gp, j * pitch:j * pitch + rows, :]
                                  for gp in range(KV_GROUPS // 2)], axis=1).astype(BF16)
            acc = acc + jnp.dot(xs, wbd_ref[s, j], preferred_element_type=F32)
            if s == 0:
                regroup(1, xv_ref, range(min(j * ppj, pg), min((j + 1) * ppj, pg)))
        bias = jnp.dot(pe_ref[s], wb_ref[s], preferred_element_type=F32)[0:1]
        lo = acc[:, :KV_WIDTH]
        prev = jnp.where(row == 0, carry_ref[s, SUBLANES - 1:SUBLANES, :], pltpu.roll(lo, 1, 0))
        out_ref[...] = (prev + acc[:, KV_WIDTH:] + bias).astype(BF16)
        carry_ref[s] = lo[rows - SUBLANES:, :]


def _cmp_pages(cache_cmp, page_table, wbd, pe8, wb):
    db, n_pages = page_table.shape
    cpp = PAGE_SIZE // CMP_STRIDE
    pg = min(CMP_PG, n_pages)
    nh = n_pages // pg
    page_shape = cache_cmp.shape[1:]
    page_specs = [pl.BlockSpec((None,) + page_shape, functools.partial(
        lambda bi, hi, pt, k: (pt[bi, hi * pg + k], 0, 0, 0, 0), k=k)) for k in range(pg)]
    const = lambda shape: pl.BlockSpec(shape, lambda bi, hi, pt: (0,) * len(shape))
    out_spec = pl.BlockSpec((None, pg * cpp, KV_WIDTH), lambda bi, hi, pt: (bi, hi, 0))
    out_shape = jax.ShapeDtypeStruct((db, n_pages * cpp, KV_WIDTH), BF16)
    return pl.pallas_call(
        functools.partial(_cmp_pages_kernel, pg=pg),
        grid_spec=pltpu.PrefetchScalarGridSpec(
            num_scalar_prefetch=1,
            grid=(db, nh),
            in_specs=page_specs + [const(wbd.shape), const(pe8.shape), const(wb.shape)],
            out_specs=[out_spec, out_spec],
            scratch_shapes=[pltpu.VMEM((KV_GROUPS // 2, CMP_STRIDE * (pg * cpp + SUBLANES), LANES), F32),
                            pltpu.VMEM((KV_GROUPS // 2, CMP_STRIDE * (pg * cpp + SUBLANES), LANES), F32),
                            pltpu.VMEM((2, SUBLANES, KV_WIDTH), F32)]),
        out_shape=[out_shape, out_shape],
        compiler_params=_cparams(("arbitrary", "arbitrary")),
        name="cmp_pages",
    )(page_table, *([cache_cmp] * pg), wbd, pe8, wb)


def _query_blockdiag(q_ref):
    qt = jnp.transpose(q_ref[...] * ATTN_SCALE)
    tiled = jnp.concatenate([qt] * KV_GROUPS, axis=0)
    rowg = lax.broadcasted_iota(jnp.int32, (KV_WIDTH, 1), 0) // HEAD_DIM
    colg = lax.broadcasted_iota(jnp.int32, (1, LANES), 1) // HEADS_PER_GROUP
    return jnp.where(rowg == colg, tiled, 0.0).astype(BF16)


def _diag_heads(o):
    rowg = lax.broadcasted_iota(jnp.int32, (N_HEADS, 1), 0) // HEADS_PER_GROUP
    out = jnp.zeros((N_HEADS, HEAD_DIM), F32)
    for g in range(KV_GROUPS):
        out = out + jnp.where(rowg == g, o[:N_HEADS, g * HEAD_DIM:(g + 1) * HEAD_DIM], 0.0)
    return out


def _smp_cmp_kernel(q_ref, kc_ref, vc_ref, esum_ref, gsum_ref, oc_ref, imp_ref, *, qpos):
    qbd = _query_blockdiag(q_ref)
    nr = kc_ref.shape[0]
    s = jnp.dot(kc_ref[...], qbd, preferred_element_type=F32)
    row = lax.broadcasted_iota(jnp.int32, (nr, 1), 0)
    mf = jnp.where(row >= 1, jnp.where((row - 1) * CMP_STRIDE + CMP_BLOCK - 1 <= qpos, 1.0, 0.0), 0.0)
    s = jnp.where(mf > 0.5, s, NEG_INF)
    p = jnp.exp(s - jnp.max(s, axis=0, keepdims=True)) * mf
    p = p / jnp.maximum(jnp.sum(p, axis=0, keepdims=True), 1e-30)
    o = jnp.dot(jnp.transpose(p).astype(BF16), vc_ref[...], preferred_element_type=F32)
    oc_ref[...] = _diag_heads(o)
    impg = _dot_exact_right(p, gsum_ref[...])
    imp_ref[...] = _dot_exact_left(esum_ref[...], impg)


def _smp_cmp(q_pad, kc, vc, n_blk_pad, qpos):
    db, nr, _ = kc.shape
    n_blk = (qpos + SLC_BLOCK) // SLC_BLOCK
    esum = _esum_matrix(n_blk_pad, nr, 1) * jnp.asarray(np.arange(n_blk_pad)[:, None] < n_blk, BF16)
    gs = (np.arange(LANES)[:, None] // HEADS_PER_GROUP == np.arange(LANES)[None, :]) & (
        np.arange(LANES)[:, None] < N_HEADS)
    gsum = jnp.asarray(gs, BF16)
    per_b = lambda shape: pl.BlockSpec((None,) + shape, lambda bi: (bi, 0, 0))
    const = lambda shape: pl.BlockSpec(shape, lambda bi: (0,) * len(shape))
    return pl.pallas_call(
        functools.partial(_smp_cmp_kernel, qpos=qpos),
        grid=(db,),
        in_specs=[per_b((LANES, HEAD_DIM)), per_b((nr, KV_WIDTH)), per_b((nr, KV_WIDTH)),
                  const((n_blk_pad, nr)), const((LANES, LANES))],
        out_specs=[per_b((N_HEADS, HEAD_DIM)), per_b((n_blk_pad, LANES))],
        out_shape=[jax.ShapeDtypeStruct((db, N_HEADS, HEAD_DIM), F32),
                   jax.ShapeDtypeStruct((db, n_blk_pad, LANES), F32)],
        compiler_params=_cparams(("arbitrary",)),
        name="smp_cmp",
    )(q_pad, kc, vc, esum, gsum)


def _smp_topk_kernel(imp_ref, idx_ref, *, n_blk, n_sel):
    nbp = imp_ref.shape[0]
    blk = lax.broadcasted_iota(jnp.int32, (nbp, 1), 0)
    cur = n_blk - 1
    valid = blk <= cur
    forced = (blk == 0) | (blk == cur) | (blk == cur - 1)
    score = jnp.where(valid, imp_ref[...] + jnp.where(forced, FORCE_BONUS, 0.0), NEG_INF)
    rank = _rank(score, nbp)
    blkf = blk.astype(F32)
    for k in range(n_sel):
        pick = jnp.sum(jnp.where(rank == float(k), blkf, 0.0), axis=0, keepdims=True)
        idx_ref[k:k + 1, :] = pick.astype(jnp.int32)


def _smp_topk(score_t, n_blk, n_sel):
    nbp, w = score_t.shape
    full = lambda shape: pl.BlockSpec(shape, lambda: (0,) * len(shape))
    return pl.pallas_call(
        functools.partial(_smp_topk_kernel, n_blk=n_blk, n_sel=n_sel),
        in_specs=[full((nbp, w))],
        out_specs=full((n_sel, w)),
        out_shape=jax.ShapeDtypeStruct((n_sel, w), jnp.int32),
        name="smp_topk",
    )(score_t)


def _smp_attn_kernel(page_ref, half_ref, isnew_ref, *refs, n_slots, n_sel, past, wbuf):
    blk_refs = refs[:n_slots]
    (q_ref, ksn_ref, win_ref, kwn_ref, oc_ref, gt_ref, go_ref, o_ref, wout_ref) = refs[n_slots:]
    b = pl.program_id(0)
    lane_half = lax.broadcasted_iota(jnp.int32, (1, PAGE_SIZE), 1) // SLC_BLOCK
    row8 = lax.broadcasted_iota(jnp.int32, (SUBLANES, 1), 0)
    wlane = lax.broadcasted_iota(jnp.int32, (1, wbuf), 1)
    kposw = past - wbuf + wlane
    wbias = jnp.where((kposw > past - WINDOW) & (kposw >= 0), 0.0, NEG_INF)
    nt = (((1,), (1,)), ((), ()))

    def bf(x):
        return x.astype(BF16).astype(F32)

    ys = []
    for g in range(KV_GROUPS):
        q = q_ref[g] * ATTN_SCALE
        qb = q.astype(BF16)
        kn = bf(ksn_ref[:, g * HEAD_DIM:(g + 1) * HEAD_DIM])
        vn = bf(ksn_ref[:, KV_WIDTH + g * HEAD_DIM:KV_WIDTH + (g + 1) * HEAD_DIM])
        kwn = bf(kwn_ref[:, g * HEAD_DIM:(g + 1) * HEAD_DIM])
        vwn = bf(kwn_ref[:, KV_WIDTH + g * HEAD_DIM:KV_WIDTH + (g + 1) * HEAD_DIM])

        slots = range(g * n_sel, (g + 1) * n_sel)
        kt = jnp.concatenate([blk_refs[k][0] for k in slots], axis=1).astype(BF16)
        vt = jnp.concatenate([blk_refs[k][1] for k in slots], axis=1).astype(BF16)
        keeps = [jnp.where(isnew_ref[b, k] == 0, 1.0, 0.0) for k in slots]
        mf = jnp.concatenate([jnp.where(lane_half == half_ref[b, k], keep, 0.0)
                              for k, keep in zip(slots, keeps)], axis=1)
        has_new = 1.0 - functools.reduce(jnp.minimum, keeps)
        s = jnp.where(mf > 0.5, jnp.dot(qb, kt, preferred_element_type=F32), NEG_INF)
        s_n = jnp.where(has_new > 0.5, jnp.sum(bf(q) * kn, axis=1, keepdims=True), NEG_INF)
        m = jnp.maximum(jnp.max(s, axis=1, keepdims=True), s_n)
        p = jnp.exp(s - m) * mf
        p_n = jnp.exp(s_n - m) * has_new
        l = jnp.sum(p, axis=1, keepdims=True) + p_n
        o_s = (lax.dot_general(p.astype(BF16), vt, nt, preferred_element_type=F32) + bf(p_n) * vn)
        o_s = o_s / jnp.maximum(l, 1e-30)

        s = jnp.dot(qb, win_ref[0, g].astype(BF16), preferred_element_type=F32) + wbias
        s_n = jnp.sum(bf(q) * kwn, axis=1, keepdims=True)
        m = jnp.maximum(jnp.max(s, axis=1, keepdims=True), s_n)
        p = jnp.exp(s - m)
        p_n = jnp.exp(s_n - m)
        l = jnp.sum(p, axis=1, keepdims=True) + p_n
        o_w = (lax.dot_general(p.astype(BF16), win_ref[1, g].astype(BF16), nt, preferred_element_type=F32)
               + bf(p_n) * vwn)
        o_w = o_w / l

        gates = jax.nn.sigmoid(gt_ref[g])
        ys.append(gates[:, 0:1] * oc_ref[g] + gates[:, 1:2] * o_s + gates[:, 2:3] * o_w)

    real = row8 < HEADS_PER_GROUP
    ssq = functools.reduce(lambda a, c: a + c, [jnp.sum(jnp.where(real, y * y, 0.0), axis=1, keepdims=True)
                                                 for y in ys])
    scale = lax.rsqrt(jnp.sum(ssq, axis=0, keepdims=True) / ATT_WIDTH + EPS)
    for g in range(KV_GROUPS):
        o_ref[g] = (ys[g] * scale * go_ref[g]).astype(BF16)

    new_col = jnp.transpose(jnp.broadcast_to(kwn_ref[...], (SUBLANES, 2 * KV_WIDTH)))[:, 0:1]
    for s2 in range(2):
        for g in range(KV_GROUPS):
            c0 = (s2 * KV_GROUPS + g) * HEAD_DIM
            wout_ref[s2, g] = jnp.where(wlane == wbuf - 1, new_col[c0:c0 + HEAD_DIM],
                                        pltpu.roll(win_ref[s2, g], wbuf - 1, 1))


def _smp_attn(page, half, isnew, cache_slc, q_g, kvs_new, cache_win, kvw_new, o_c, gates, g_out, past):
    db, n_slots = page.shape
    n_sel = n_slots // KV_GROUPS
    wbuf = cache_win.shape[-1]
    width = 2 * KV_WIDTH
    blk_specs = [pl.BlockSpec((None, 2, None, HEAD_DIM, PAGE_SIZE), functools.partial(
        lambda bi, pg, hf, nw, k: (pg[bi, k], 0, k // n_sel, 0, 0), k=k)) for k in range(n_slots)]
    per_b3 = lambda shape: pl.BlockSpec((None,) + shape, lambda bi, pg, hf, nw: (bi, 0, 0))
    per_b4 = lambda shape: pl.BlockSpec((None,) + shape, lambda bi, pg, hf, nw: (bi, 0, 0, 0))
    per_b5 = lambda shape: pl.BlockSpec((None,) + shape, lambda bi, pg, hf, nw: (bi, 0, 0, 0, 0))
    win_shape = (2, KV_GROUPS, HEAD_DIM, wbuf)
    head_shape = (KV_GROUPS, SUBLANES, HEAD_DIM)
    return pl.pallas_call(
        functools.partial(_smp_attn_kernel, n_slots=n_slots, n_sel=n_sel, past=past, wbuf=wbuf),
        grid_spec=pltpu.PrefetchScalarGridSpec(
            num_scalar_prefetch=3,
            grid=(db,),
            in_specs=blk_specs + [per_b4(head_shape), per_b3((1, width)), per_b5(win_shape), per_b3((1, width)),
                                  per_b4(head_shape), per_b4((KV_GROUPS, SUBLANES, 3)),
                                  pl.BlockSpec(head_shape, lambda bi, pg, hf, nw: (0, 0, 0))],
            out_specs=[per_b4(head_shape), per_b5(win_shape)]),
        out_shape=[jax.ShapeDtypeStruct((db,) + head_shape, BF16),
                   jax.ShapeDtypeStruct((db,) + win_shape, F32)],
        compiler_params=_cparams(("arbitrary",)),
        name="smp_attn",
    )(page, half, isnew, *([cache_slc] * n_slots), q_g, kvs_new.reshape(db, 1, width), cache_win,
      kvw_new.reshape(db, 1, width), o_c, gates, g_out)


def _nsa_sample(za, gates_raw, kvs_new, kvw_new, cache_cmp, cache_slc, cache_win, page_table, wbd, pe8, wb,
                g_out):
    db, n_pages = page_table.shape
    past = n_pages * PAGE_SIZE
    n_blk = (past + SLC_BLOCK) // SLC_BLOCK
    n_sel = min(N_SEL, n_blk)
    n_blk_pad = -(-n_blk // SUBLANES) * SUBLANES
    q = za[:, 2 * SSM_WIDTH:].reshape(db, N_HEADS, HEAD_DIM)
    q_pad = jnp.pad(q, ((0, 0), (0, LANES - N_HEADS), (0, 0)))
    kc, vc = _cmp_pages(cache_cmp, page_table, wbd, pe8, wb)
    o_c, imp = _smp_cmp(q_pad, kc, vc, n_blk_pad, past)
    score_t = jnp.transpose(imp[:, :, :KV_GROUPS], (1, 0, 2)).reshape(n_blk_pad, db * KV_GROUPS)
    lane_pad = -(-db * KV_GROUPS // LANES) * LANES
    score_t = jnp.pad(score_t, ((0, 0), (0, lane_pad - db * KV_GROUPS)))
    idx = _smp_topk(score_t, n_blk, n_sel)[:, :db * KV_GROUPS]
    idx = jnp.transpose(idx.reshape(n_sel, db, KV_GROUPS), (1, 2, 0))
    n_past_blk = past // SLC_BLOCK
    per_page = PAGE_SIZE // SLC_BLOCK
    jp = jnp.minimum(idx, n_past_blk - 1).reshape(db, KV_GROUPS * n_sel)
    page = jnp.take_along_axis(page_table, jp // per_page, axis=1).astype(jnp.int32)
    half = (jp % per_page).astype(jnp.int32)
    isnew = (idx >= n_past_blk).reshape(db, KV_GROUPS * n_sel).astype(jnp.int32)
    pad_heads = lambda a: jnp.pad(a.reshape(a.shape[0], KV_GROUPS, HEADS_PER_GROUP, a.shape[-1]),
                                  ((0, 0), (0, 0), (0, SUBLANES - HEADS_PER_GROUP), (0, 0)))
    gates = pad_heads(gates_raw[:, :3 * N_HEADS].reshape(db, N_HEADS, 3))
    g_out_g = pad_heads(g_out.reshape(1, N_HEADS, HEAD_DIM))[0]
    y, win_new = _smp_attn(page, half, isnew, cache_slc, pad_heads(q), kvs_new, cache_win, kvw_new,
                           pad_heads(o_c), gates, g_out_g, past)
    return y[:, :, :HEADS_PER_GROUP].reshape(db, ATT_WIDTH), win_new


def _outproj_kernel(x_ref, ms_ref, ma_ref, w_ref, gt_ref, sc_ref, sh_ref, g_ref, x1_ref, h2_ref):
    mixed = jnp.concatenate([ms_ref[...], ma_ref[...]], axis=1)
    x1 = x_ref[...] + gt_ref[...] * jnp.dot(mixed, w_ref[...], preferred_element_type=F32)
    x1_ref[...] = x1
    h2_ref[...] = (_rms(x1, g_ref[...]) * (1.0 + sc_ref[...]) + sh_ref[...]).astype(BF16)


def _outproj(x, m_ssm, m_att, w_out, gt1, sc2, sh2, g2, tm, rows_per_mod):
    n, d = x.shape
    r = gt1.shape[1]
    tpm = rows_per_mod // tm
    mod_spec = pl.BlockSpec((None, r, d), lambda i: (i // tpm, 0, 0))
    row = lambda w: pl.BlockSpec((tm, w), lambda i: (i, 0))
    return pl.pallas_call(
        _outproj_kernel,
        grid=(n // tm,),
        in_specs=[row(d), row(SSM_WIDTH), row(ATT_WIDTH), pl.BlockSpec((d, d), lambda i: (0, 0)),
                  mod_spec, mod_spec, mod_spec, pl.BlockSpec((1, d), lambda i: (0, 0))],
        out_specs=[row(d), row(d)],
        out_shape=[jax.ShapeDtypeStruct((n, d), F32), jax.ShapeDtypeStruct((n, d), BF16)],
        compiler_params=_cparams(("arbitrary",)),
        name="outproj",
    )(x, m_ssm, m_att, w_out, gt1, sc2, sh2, g2.reshape(1, d))


def _ffn_act(a_v, a_g, a1_v, a1_g, a2_v, a2_g, cwv_ref, cwg_ref, cbv_ref, cbg_ref):
    val = cbv_ref[...] + cwv_ref[2:3, :] * a_v + cwv_ref[0:1, :] * a2_v + cwv_ref[1:2, :] * a1_v
    gate = cbg_ref[...] + cwg_ref[2:3, :] * a_g + cwg_ref[0:1, :] * a2_g + cwg_ref[1:2, :] * a1_g
    return (gate * jax.nn.sigmoid(gate) * val).astype(BF16)


def _ffn_finish(j, contrib, x1_ref, gt_ref, gf_ref, y_ref):
    @pl.when(j == 0)
    def _():
        y_ref[...] = contrib

    @pl.when(j > 0)
    def _():
        y_ref[...] += contrib

    @pl.when(j == FFN_NF - 1)
    def _():
        y_ref[...] = _rms(x1_ref[...] + gt_ref[...] * y_ref[...], gf_ref[...])


def _ffn_seq_kernel(h_ref, x1_ref, gt_ref, wv_ref, wg_ref, cwv_ref, cwg_ref, cbv_ref, cbg_ref, wd_ref, gf_ref,
                    y_ref, tv_ref, tg_ref, sv_ref, sg_ref, hv_ref, hg_ref, *, tm, tpb, rs):
    i = pl.program_id(0)
    j = pl.program_id(1)

    @pl.when(i % tpb == 0)
    def _():
        hv_ref[j] = jnp.zeros((SUBLANES, FFN_TF), F32)
        hg_ref[j] = jnp.zeros((SUBLANES, FFN_TF), F32)

    @pl.when(j == 0)
    def _():
        y_ref[...] = jnp.zeros(y_ref.shape, F32)

    sv_ref[0:SUBLANES, :] = hv_ref[j]
    sg_ref[0:SUBLANES, :] = hg_ref[j]

    def up(k):
        hk = h_ref[k * rs:(k + 1) * rs, :]
        o = SUBLANES + k * rs
        sv_ref[o:o + rs, :] = jnp.dot(hk, wv_ref[...], preferred_element_type=F32)
        sg_ref[o:o + rs, :] = jnp.dot(hk, wg_ref[...], preferred_element_type=F32)

    def down(k):
        o = SUBLANES + k * rs
        act = _ffn_act(sv_ref[o:o + rs, :], sg_ref[o:o + rs, :], sv_ref[o - 1:o - 1 + rs, :],
                       sg_ref[o - 1:o - 1 + rs, :], sv_ref[o - 2:o - 2 + rs, :], sg_ref[o - 2:o - 2 + rs, :],
                       cwv_ref, cwg_ref, cbv_ref, cbg_ref)
        y_ref[k * rs:(k + 1) * rs, :] += jnp.dot(act, wd_ref[...], preferred_element_type=F32)

    up(0)
    for k in range(tm // rs):
        if k + 1 < tm // rs:
            up(k + 1)
        down(k)

    for s_ref, halo_ref, t_ref in ((sv_ref, hv_ref, tv_ref), (sg_ref, hg_ref, tg_ref)):
        halo_ref[j] = s_ref[tm:tm + SUBLANES, :]
        t_ref[...] = s_ref[tm + SUBLANES - 2:tm + SUBLANES, :]

    @pl.when(j == FFN_NF - 1)
    def _():
        y_ref[...] = _rms(x1_ref[...] + gt_ref[...] * y_ref[...], gf_ref[...])


def _ffn_step_kernel(h_ref, x1_ref, gt_ref, wv_ref, wg_ref, cwv_ref, cwg_ref, cbv_ref, cbg_ref, wd_ref, gf_ref,
                     p0v_ref, p0g_ref, p1v_ref, p1g_ref, y_ref, av_ref, ag_ref):
    j = pl.program_id(1)
    h = h_ref[...]
    a_v = jnp.dot(h, wv_ref[...], preferred_element_type=F32)
    a_g = jnp.dot(h, wg_ref[...], preferred_element_type=F32)
    av_ref[...] = a_v
    ag_ref[...] = a_g
    act = _ffn_act(a_v, a_g, p1v_ref[...], p1g_ref[...], p0v_ref[...], p0g_ref[...],
                   cwv_ref, cwg_ref, cbv_ref, cbg_ref)
    _ffn_finish(j, jnp.dot(act, wd_ref[...], preferred_element_type=F32), x1_ref, gt_ref, gf_ref, y_ref)


def _ffn_specs(d, tm, r, tpm):
    row_once = pl.BlockSpec((tm, d), lambda i, j: (i, 0), pipeline_mode=pl.Buffered(1))
    return [row_once, row_once, pl.BlockSpec((None, r, d), lambda i, j: (i // tpm, 0, 0)),
            pl.BlockSpec((None, d, FFN_TF), lambda i, j: (j, 0, 0)),
            pl.BlockSpec((None, d, FFN_TF), lambda i, j: (j + FFN_NF, 0, 0)),
            pl.BlockSpec((CONV_W, FFN_TF), lambda i, j: (0, j)),
            pl.BlockSpec((CONV_W, FFN_TF), lambda i, j: (0, j + FFN_NF)),
            pl.BlockSpec((1, FFN_TF), lambda i, j: (0, j)), pl.BlockSpec((1, FFN_TF), lambda i, j: (0, j + FFN_NF)),
            pl.BlockSpec((FFN_TF, d), lambda i, j: (j, 0)), pl.BlockSpec((1, d), lambda i, j: (0, 0))]


def _ffn_seq(h2, x1, gt2, w_up, conv_w, conv_b, w_down, g_final, tm, rows_per_mod):
    n, d = x1.shape
    tpb = rows_per_mod // tm
    nt = n // tm
    cb = conv_b.reshape(1, -1)
    y, tv, tg = pl.pallas_call(
        functools.partial(_ffn_seq_kernel, tm=tm, tpb=tpb, rs=min(FFN_RS, tm)),
        grid=(nt, FFN_NF),
        in_specs=_ffn_specs(d, tm, gt2.shape[1], tpb),
        out_specs=[pl.BlockSpec((tm, d), lambda i, j: (i, 0), pipeline_mode=pl.Buffered(1)),
                   pl.BlockSpec((None, CONV_W - 1, FFN_TF), lambda i, j: (i, 0, j)),
                   pl.BlockSpec((None, CONV_W - 1, FFN_TF), lambda i, j: (i, 0, j))],
        out_shape=[jax.ShapeDtypeStruct((n, d), F32),
                   jax.ShapeDtypeStruct((nt, CONV_W - 1, D_FF), F32),
                   jax.ShapeDtypeStruct((nt, CONV_W - 1, D_FF), F32)],
        scratch_shapes=[pltpu.VMEM((tm + SUBLANES, FFN_TF), F32), pltpu.VMEM((tm + SUBLANES, FFN_TF), F32),
                        pltpu.VMEM((FFN_NF, SUBLANES, FFN_TF), F32), pltpu.VMEM((FFN_NF, SUBLANES, FFN_TF), F32)],
        compiler_params=_cparams(("arbitrary", "arbitrary")),
        name="ffn_seq",
    )(h2, x1, gt2, w_up, w_up, conv_w, conv_w, cb, cb, w_down, g_final.reshape(1, d))
    tails = jnp.concatenate([tv, tg], axis=-1)
    return y, tails[tpb - 1::tpb]


def _ffn_step(h2, x1, gt2, w_up, conv_w, conv_b, w_down, g_final, conv_prev):
    n, d = x1.shape
    cb = conv_b.reshape(1, -1)
    prev_v = pl.BlockSpec((n, FFN_TF), lambda i, j: (0, j))
    prev_g = pl.BlockSpec((n, FFN_TF), lambda i, j: (0, j + FFN_NF))
    p0, p1 = conv_prev[:, 0], conv_prev[:, 1]
    y, a_v, a_g = pl.pallas_call(
        _ffn_step_kernel,
        grid=(1, FFN_NF),
        in_specs=_ffn_specs(d, n, gt2.shape[1], 1) + [prev_v, prev_g, prev_v, prev_g],
        out_specs=[pl.BlockSpec((n, d), lambda i, j: (0, 0)),
                   pl.BlockSpec((n, FFN_TF), lambda i, j: (0, j)), pl.BlockSpec((n, FFN_TF), lambda i, j: (0, j))],
        out_shape=[jax.ShapeDtypeStruct((n, d), F32),
                   jax.ShapeDtypeStruct((n, D_FF), F32), jax.ShapeDtypeStruct((n, D_FF), F32)],
        compiler_params=_cparams(("arbitrary", "arbitrary")),
        name="ffn_step",
    )(h2, x1, gt2, w_up, w_up, conv_w, conv_w, cb, cb, w_down, g_final.reshape(1, d), p0, p0, p1, p1)
    return y, jnp.stack([p1, jnp.concatenate([a_v, a_g], axis=-1)], axis=1)


def kernel(x_prompt, x_sample, cache_cmp_kv, cache_slc_kv, cache_win_kv, state_ssm_re, state_ssm_im, state_conv,
           page_table, c_prompt, c_sample, w_ada, b_ada, g_norm1, w_in, ssm_lam_re, ssm_lam_im, ssm_log_dt,
           ssm_b_re, ssm_b_im, ssm_c_re, ssm_c_im, ssm_d, w_cmp, pe_cmp, g_out_ssm, g_out_att, w_out, g_norm2,
           w_up, conv_w, conv_b, w_down, g_final):
    depth = w_in.shape[0]
    b, s, d = x_prompt.shape
    db, ds, _ = x_sample.shape
    assert depth == 1 and ds == 1 and d == D_MODEL, "kernel is written for one layer and one new token per sequence"
    assert s % 512 == 0
    tm = 512
    tm_in = 256
    tm_ffn = min(1024, s)
    gp = SSM_GROUPS * SSM_STATE
    kv_shape = (2, KV_GROUPS, HEAD_DIM)
    l = 0

    w_in_p = jnp.pad(w_in[l], ((0, 0), (0, IN_PAD - IN_WIDTH))).astype(BF16)
    w_out_b = w_out[l].astype(BF16)
    w_up_b = jnp.transpose(w_up[l].reshape(d, 2 * FFN_NF, FFN_TF), (1, 0, 2)).astype(BF16)
    w_down_b = w_down[l].astype(BF16)
    pwr, pwi, bbr, bbi = _s5_prep(ssm_lam_re[l], ssm_lam_im[l], ssm_log_dt[l], ssm_b_re[l], ssm_b_im[l])
    w1, w2, lvl_r, lvl_i, pw_r, pw_i = _s5_weights(pwr, pwi, bbr, bbi, ssm_c_re[l], ssm_c_im[l])
    wbd, pe8, wb = _cmp_weights(w_cmp[l], pe_cmp[l])

    n_c = b + db
    n_c_pad = -(-n_c // SUBLANES) * SUBLANES
    c_all = jnp.pad(jnp.concatenate([c_prompt, c_sample], axis=0), ((0, n_c_pad - n_c), (0, 0)))
    mod = _ada(c_all, w_ada[l], b_ada[l]).reshape(n_c_pad, 6, d)
    mod_p = [mod[:b, k].reshape(b, 1, d) for k in range(6)]
    mod_s = [mod[b:n_c, k].reshape(1, db, d) for k in range(6)]

    xp = x_prompt.reshape(b * s, d)
    za, kvc, kvs, kvw, graw = _inproj(xp, mod_p[1], mod_p[0], g_norm1[l], w_in_p, tm_in, s)
    m_ssm, st_re, st_im = _s5_prompt(za, b, s, w1, w2, lvl_r, lvl_i, pw_r, pw_i, ssm_d[l], g_out_ssm[l], 256)
    kc, vct = _cmpproj(kvc, b, s, wbd, pe8, wb)
    kvc_t, kvs_t, ks_b, vst, kw_b, vwt = _kvprep(kvc, kvs, kvw, b, s, 512)
    rows_major = lambda a: jnp.transpose(a, (0, 4, 1, 2, 3))[None]
    m_att = _nsa_prompt(za, graw, ks_b, vst, kw_b, vwt, kc, vct, g_out_att[l], b, s)
    x1, h2 = _outproj(xp, m_ssm, m_att, w_out_b, mod_p[2], mod_p[4], mod_p[3], g_norm2[l], tm, s)
    y_p, conv_p = _ffn_seq(h2, x1, mod_p[5], w_up_b, conv_w[l], conv_b[l], w_down_b, g_final, tm_ffn, s)
    wlen = min(WINDOW, s)
    win_p = kvw.reshape(b, s, 2 * KV_WIDTH)[:, s - wlen:].reshape(b, wlen, *kv_shape)

    xs = x_sample.reshape(db, d)
    za_s, kvc_s, kvs_s, kvw_s, graw_s = _inproj(xs, mod_s[1], mod_s[0], g_norm1[l], w_in_p, db, db)
    m_ssm_s, st_re_s, st_im_s = _s5_sample(za_s, state_ssm_re[l].reshape(db, gp), state_ssm_im[l].reshape(db, gp),
                                           w1, w2, pw_r, pw_i, ssm_d[l], g_out_ssm[l])
    rows_minor = lambda c: jnp.transpose(c, (0, 2, 3, 4, 1))
    m_att_s, win_s = _nsa_sample(za_s, graw_s, kvs_s, kvw_s, rows_minor(cache_cmp_kv[l]),
                                 rows_minor(cache_slc_kv[l]), rows_minor(cache_win_kv[l]),
                                 page_table, wbd, pe8, wb, g_out_att[l])
    win_s = jnp.transpose(win_s, (0, 4, 1, 2, 3))
    x1_s, h2_s = _outproj(xs, m_ssm_s, m_att_s, w_out_b, mod_s[2], mod_s[4], mod_s[3], g_norm2[l], db, db)
    y_s, conv_s = _ffn_step(h2_s, x1_s, mod_s[5], w_up_b, conv_w[l], conv_b[l], w_down_b, g_final, state_conv[l])

    wbuf = cache_win_kv.shape[2]
    return (y_p.reshape(b, s, d), y_s.reshape(db, 1, d),
            rows_major(kvc_t), kvc_s.reshape(1, db, 1, *kv_shape),
            rows_major(kvs_t), kvs_s.reshape(1, db, 1, *kv_shape),
            win_p[None], win_s.reshape(1, db, wbuf, *kv_shape),
            st_re.reshape(1, b, SSM_GROUPS, SSM_STATE), st_im.reshape(1, b, SSM_GROUPS, SSM_STATE),
            st_re_s.reshape(1, db, SSM_GROUPS, SSM_STATE), st_im_s.reshape(1, db, SSM_GROUPS, SSM_STATE),
            conv_p[None], conv_s[None])
```

```python
import functools
import math

import jax
import jax.numpy as jnp
import numpy as np
from jax import lax
from jax.experimental import pallas as pl
from jax.experimental.pallas import tpu as pltpu

F32 = jnp.float32
BF16 = jnp.bfloat16

D_MODEL = 2048
SSM_WIDTH = D_MODEL // 2
ATT_WIDTH = D_MODEL - SSM_WIDTH
SSM_CH = 16
SSM_GROUPS = SSM_WIDTH // SSM_CH
SSM_STATE = 64
HEAD_DIM = 64
N_HEADS = ATT_WIDTH // HEAD_DIM
KV_GROUPS = 4
HEADS_PER_GROUP = N_HEADS // KV_GROUPS
KV_WIDTH = KV_GROUPS * HEAD_DIM
CMP_STRIDE = 16
CMP_BLOCK = 32
SLC_BLOCK = 64
N_SEL = 16
WINDOW = 512
PAGE_SIZE = 128
ATTN_SCALE = HEAD_DIM ** -0.5
NEG_INF = -1e30
LOG2E = math.log2(math.e)
FORCE_BONUS = 1e4
D_FF = 256 * ((8 * D_MODEL // 3 + 255) // 256)
CONV_W = 3
EPS = 1e-6
IN_WIDTH = 3 * SSM_WIDTH + 6 * KV_WIDTH + 3 * N_HEADS

LANES = 128
SUBLANES = 8
VMEM_LIMIT = 56 * 1024 * 1024

IN_TN = 512
IN_NA = 3 * SSM_WIDTH // IN_TN
IN_PAD = (IN_NA + 4) * IN_TN
GATE_PAD = LANES
SSM_LCH = 8 * SSM_STATE
SSM_NCH = SSM_GROUPS // 8
Q_TILE = 128
SEL_KT = 256
WIN_KT = 128
CMP_PG = 32
FFN_TF = 512
FFN_NF = D_FF // FFN_TF
FFN_RS = 256


def _cparams(sem):
    return pltpu.CompilerParams(dimension_semantics=sem, vmem_limit_bytes=VMEM_LIMIT)


def _rms(x, g):
    return x * lax.rsqrt(jnp.mean(x * x, axis=-1, keepdims=True) + EPS) * g


def _ada_kernel(c_ref, w_ref, b_ref, o_ref):
    c = c_ref[...]
    a = (c * jax.nn.sigmoid(c)).astype(BF16)
    o_ref[...] = jnp.dot(a, w_ref[...].astype(BF16), preferred_element_type=F32) + b_ref[...]


def _ada(c_all, w_ada, b_ada):
    r, d = c_all.shape
    n = w_ada.shape[1]
    tn = 1024
    return pl.pallas_call(
        _ada_kernel,
        grid=(n // tn,),
        in_specs=[pl.BlockSpec((r, d), lambda j: (0, 0)),
                  pl.BlockSpec((d, tn), lambda j: (0, j)),
                  pl.BlockSpec((1, tn), lambda j: (0, j))],
        out_specs=pl.BlockSpec((r, tn), lambda j: (0, j)),
        out_shape=jax.ShapeDtypeStruct((r, n), F32),
        compiler_params=_cparams(("arbitrary",)),
        name="ada",
    )(c_all, w_ada, b_ada.reshape(1, n))


def _inproj_kernel(x_ref, sc_ref, sh_ref, g_ref, w_ref, za_ref, kc_ref, ks_ref, kw_ref, gt_ref):
    h = (_rms(x_ref[...], g_ref[...]) * (1.0 + sc_ref[...]) + sh_ref[...]).astype(BF16)
    for j in range(IN_PAD // IN_TN):
        z = jnp.dot(h, w_ref[:, j * IN_TN:(j + 1) * IN_TN], preferred_element_type=F32)
        if j < IN_NA:
            za_ref[:, j * IN_TN:(j + 1) * IN_TN] = z
        elif j < IN_NA + 3:
            (kc_ref, ks_ref, kw_ref)[j - IN_NA][...] = z
        else:
            gt_ref[...] = z[:, :GATE_PAD]


def _inproj(x, sc, sh, g, w_pad, tm, rows_per_mod):
    n, d = x.shape
    r = sc.shape[1]
    tpm = rows_per_mod // tm
    mod_spec = pl.BlockSpec((None, r, d), lambda i: (i // tpm, 0, 0))
    row = lambda w: pl.BlockSpec((tm, w), lambda i: (i, 0))
    return pl.pallas_call(
        _inproj_kernel,
        grid=(n // tm,),
        in_specs=[row(d), mod_spec, mod_spec, pl.BlockSpec((1, d), lambda i: (0, 0)),
                  pl.BlockSpec((d, IN_PAD), lambda i: (0, 0), pipeline_mode=pl.Buffered(1))],
        out_specs=[row(3 * SSM_WIDTH), row(2 * KV_WIDTH), row(2 * KV_WIDTH), row(2 * KV_WIDTH), row(GATE_PAD)],
        out_shape=[jax.ShapeDtypeStruct((n, 3 * SSM_WIDTH), F32),
                   jax.ShapeDtypeStruct((n, 2 * KV_WIDTH), F32),
                   jax.ShapeDtypeStruct((n, 2 * KV_WIDTH), F32),
                   jax.ShapeDtypeStruct((n, 2 * KV_WIDTH), F32),
                   jax.ShapeDtypeStruct((n, GATE_PAD), F32)],
        compiler_params=_cparams(("arbitrary",)),
        name="inproj",
    )(x, sc, sh, g.reshape(1, d), w_pad)


def _s5_prep_kernel(lre_ref, lim_ref, ldt_ref, lrex_ref, limx_ref, bre_ref, bim_ref,
                    pwr_ref, pwi_ref, bbr_ref, bbi_ref):
    dt = jnp.exp(ldt_ref[...])

    def disc(lre, lim):
        mag = jnp.exp(lre * dt)
        ab_re = mag * jnp.cos(lim * dt)
        ab_im = mag * jnp.sin(lim * dt)
        den = lre * lre + lim * lim
        f_re = ((ab_re - 1.0) * lre + ab_im * lim) / den
        f_im = (ab_im * lre - (ab_re - 1.0) * lim) / den
        return ab_re, ab_im, f_re, f_im

    ab_re, ab_im, _, _ = disc(lre_ref[...], lim_ref[...])
    pr, pi = ab_re, ab_im
    pwr_ref[0] = pr
    pwi_ref[0] = pi
    for k in range(1, SUBLANES):
        pr, pi = pr * ab_re - pi * ab_im, pr * ab_im + pi * ab_re
        pwr_ref[k] = pr
        pwi_ref[k] = pi
    _, _, f_re, f_im = disc(lrex_ref[...], limx_ref[...])
    b_re, b_im = bre_ref[...], bim_ref[...]
    bbr_ref[...] = f_re * b_re - f_im * b_im
    bbi_ref[...] = f_re * b_im + f_im * b_re


def _s5_prep(lam_re, lam_im, log_dt, b_re, b_im):
    g, p = lam_re.shape
    ch = b_re.shape[-1]
    lrex = jnp.repeat(lam_re, ch, axis=1)
    limx = jnp.repeat(lam_im, ch, axis=1)
    full = lambda shape: pl.BlockSpec(shape, lambda: (0,) * len(shape))
    return pl.pallas_call(
        _s5_prep_kernel,
        in_specs=[full((g, p)), full((g, p)), full((g, 1)), full((g, p * ch)), full((g, p * ch)),
                  full((g, p * ch)), full((g, p * ch))],
        out_specs=[full((SUBLANES, g, p)), full((SUBLANES, g, p)), full((g, p * ch)), full((g, p * ch))],
        out_shape=[jax.ShapeDtypeStruct((SUBLANES, g, p), F32), jax.ShapeDtypeStruct((SUBLANES, g, p), F32),
                   jax.ShapeDtypeStruct((g, p * ch), F32), jax.ShapeDtypeStruct((g, p * ch), F32)],
        name="s5_prep",
    )(lam_re, lam_im, log_dt.reshape(g, 1), lrex, limx, b_re.reshape(g, p * ch), b_im.reshape(g, p * ch))


def _s5_weights(pwr, pwi, bbr, bbi, c_re, c_im):
    g, p, ch = SSM_GROUPS, SSM_STATE, SSM_CH
    eye = jnp.eye(8, dtype=F32)

    def w_in(bb):
        bb = bb.reshape(SSM_NCH, 8, p, ch)
        return jnp.einsum('jgpc,gh->jgchp', bb, eye).reshape(SSM_NCH, 8 * ch, 8 * p)

    def w_out(c):
        c = c.reshape(SSM_NCH, 8, ch, p)
        return jnp.einsum('jgcp,gh->jgphc', c, eye).reshape(SSM_NCH, 8 * p, 8 * ch)

    w1 = jnp.concatenate([w_in(bbr), w_in(bbi)], axis=-1).astype(BF16)
    w2 = jnp.concatenate([w_out(c_re), w_out(-c_im)], axis=1).astype(BF16)
    pw_r = pwr.reshape(SUBLANES, g * p)
    pw_i = pwi.reshape(SUBLANES, g * p)
    tau = jnp.arange(SUBLANES)[:, None]
    lvl_r = jnp.stack([jnp.where(tau >= d, pw_r[d - 1][None, :], 0.0) for d in (1, 2, 4)])
    lvl_i = jnp.stack([jnp.where(tau >= d, pw_i[d - 1][None, :], 0.0) for d in (1, 2, 4)])
    return w1, w2, lvl_r, lvl_i, pw_r, pw_i


def _s5_post(y, u, g_glu, d_skip, g_out):
    y = y + d_skip * u
    y = jax.nn.gelu(y) * jax.nn.sigmoid(g_glu)
    return _rms(y, g_out).astype(BF16)


def _s5_scan_kernel(u_ref, gg_ref, w1_ref, w2_ref, lr_ref, li_ref, pr_ref, pi_ref, d_ref, go_ref,
                    o_ref, sr_ref, si_ref, br_ref, bi_ref, y_ref, hr_ref, hi_ref, *, tt):
    t = pl.program_id(1)
    nrt = tt // SUBLANES

    @pl.when(t == 0)
    def _():
        hr_ref[...] = jnp.zeros_like(hr_ref)
        hi_ref[...] = jnp.zeros_like(hi_ref)

    def chunk(j, carry):
        lo = pl.multiple_of(j * SSM_LCH, SSM_LCH)
        uo = pl.multiple_of(j * LANES, LANES)
        ub = u_ref[:, pl.ds(uo, LANES)].astype(BF16)
        bu = jnp.dot(ub, w1_ref[j], preferred_element_type=F32)
        xr = bu[:, :SSM_LCH].reshape(nrt, SUBLANES, SSM_LCH)
        xi = bu[:, SSM_LCH:].reshape(nrt, SUBLANES, SSM_LCH)
        for lvl, d in enumerate((1, 2, 4)):
            ar = lr_ref[lvl, :, pl.ds(lo, SSM_LCH)]
            ai = li_ref[lvl, :, pl.ds(lo, SSM_LCH)]
            zr = pltpu.roll(xr, d, 1)
            zi = pltpu.roll(xi, d, 1)
            xr, xi = xr + ar * zr - ai * zi, xi + ar * zi + ai * zr
        br_ref[...] = xr.reshape(tt, SSM_LCH)
        bi_ref[...] = xi.reshape(tt, SSM_LCH)
        pr = pr_ref[:, pl.ds(lo, SSM_LCH)]
        pi = pi_ref[:, pl.ds(lo, SSM_LCH)]

        def tile(k, h):
            hr, hi = h
            r0 = pl.multiple_of(k * SUBLANES, SUBLANES)
            vr = br_ref[pl.ds(r0, SUBLANES), :] + pr * hr - pi * hi
            vi = bi_ref[pl.ds(r0, SUBLANES), :] + pr * hi + pi * hr
            br_ref[pl.ds(r0, SUBLANES), :] = vr
            bi_ref[pl.ds(r0, SUBLANES), :] = vi
            return vr[SUBLANES - 1:, :], vi[SUBLANES - 1:, :]

        hr, hi = lax.fori_loop(0, nrt, tile, (hr_ref[:, pl.ds(lo, SSM_LCH)], hi_ref[:, pl.ds(lo, SSM_LCH)]))
        hr_ref[:, pl.ds(lo, SSM_LCH)] = hr
        hi_ref[:, pl.ds(lo, SSM_LCH)] = hi
        hcat = jnp.concatenate([br_ref[...], bi_ref[...]], axis=1).astype(BF16)
        y_ref[:, pl.ds(uo, LANES)] = jnp.dot(hcat, w2_ref[j], preferred_element_type=F32)
        return carry

    lax.fori_loop(0, SSM_NCH, chunk, 0)
    o_ref[...] = _s5_post(y_ref[...], u_ref[...], gg_ref[...], d_ref[...], go_ref[...])
    sr_ref[...] = hr_ref[...]
    si_ref[...] = hi_ref[...]


def _s5_prompt(za, b, s, w1, w2, lvl_r, lvl_i, pw_r, pw_i, d_skip, g_out, tt):
    nt = s // tt
    gp = SSM_GROUPS * SSM_STATE
    const2 = lambda shape: pl.BlockSpec(shape, lambda bi, ti: (0,) * len(shape))
    st_spec = pl.BlockSpec((None, 1, gp), lambda bi, ti: (bi, 0, 0))
    return pl.pallas_call(
        functools.partial(_s5_scan_kernel, tt=tt),
        grid=(b, nt),
        in_specs=[pl.BlockSpec((tt, SSM_WIDTH), lambda bi, ti: (bi * nt + ti, 0)),
                  pl.BlockSpec((tt, SSM_WIDTH), lambda bi, ti: (bi * nt + ti, 1)),
                  const2(w1.shape), const2(w2.shape), const2(lvl_r.shape), const2(lvl_i.shape),
                  const2(pw_r.shape), const2(pw_i.shape), const2((1, SSM_WIDTH)), const2((1, SSM_WIDTH))],
        out_specs=[pl.BlockSpec((tt, SSM_WIDTH), lambda bi, ti: (bi * nt + ti, 0)), st_spec, st_spec],
        out_shape=[jax.ShapeDtypeStruct((b * s, SSM_WIDTH), BF16),
                   jax.ShapeDtypeStruct((b, 1, gp), F32), jax.ShapeDtypeStruct((b, 1, gp), F32)],
        scratch_shapes=[pltpu.VMEM((tt, SSM_LCH), F32), pltpu.VMEM((tt, SSM_LCH), F32),
                        pltpu.VMEM((tt, SSM_WIDTH), F32),
                        pltpu.VMEM((1, gp), F32), pltpu.VMEM((1, gp), F32)],
        compiler_params=_cparams(("arbitrary", "arbitrary")),
        name="s5_prompt",
    )(za, za, w1, w2, lvl_r, lvl_i, pw_r, pw_i, d_skip.reshape(1, -1), g_out.reshape(1, -1))


def _s5_step_kernel(u_ref, gg_ref, h0r_ref, h0i_ref, w1_ref, w2_ref, pr_ref, pi_ref, d_ref, go_ref,
                    o_ref, sr_ref, si_ref, y_ref):
    for j in range(SSM_NCH):
        lo, uo = j * SSM_LCH, j * LANES
        bu = jnp.dot(u_ref[:, uo:uo + LANES].astype(BF16), w1_ref[j], preferred_element_type=F32)
        ar = pr_ref[0:1, lo:lo + SSM_LCH]
        ai = pi_ref[0:1, lo:lo + SSM_LCH]
        h0r = h0r_ref[:, lo:lo + SSM_LCH]
        h0i = h0i_ref[:, lo:lo + SSM_LCH]
        hr = bu[:, :SSM_LCH] + (ar * h0r - ai * h0i)
        hi = bu[:, SSM_LCH:] + (ar * h0i + ai * h0r)
        sr_ref[:, lo:lo + SSM_LCH] = hr
        si_ref[:, lo:lo + SSM_LCH] = hi
        hcat = jnp.concatenate([hr, hi], axis=1).astype(BF16)
        y_ref[:, uo:uo + LANES] = jnp.dot(hcat, w2_ref[j], preferred_element_type=F32)
    o_ref[...] = _s5_post(y_ref[...], u_ref[...], gg_ref[...], d_ref[...], go_ref[...])


def _s5_sample(za, h0r, h0i, w1, w2, pw_r, pw_i, d_skip, g_out):
    n = za.shape[0]
    gp = SSM_GROUPS * SSM_STATE
    full = lambda shape: pl.BlockSpec(shape, lambda i: (0,) * len(shape))
    return pl.pallas_call(
        _s5_step_kernel,
        grid=(1,),
        in_specs=[pl.BlockSpec((n, SSM_WIDTH), lambda i: (0, 0)), pl.BlockSpec((n, SSM_WIDTH), lambda i: (0, 1)),
                  full((n, gp)), full((n, gp)), full(w1.shape), full(w2.shape),
                  full(pw_r.shape), full(pw_i.shape), full((1, SSM_WIDTH)), full((1, SSM_WIDTH))],
        out_specs=[full((n, SSM_WIDTH)), full((n, gp)), full((n, gp))],
        out_shape=[jax.ShapeDtypeStruct((n, SSM_WIDTH), BF16),
                   jax.ShapeDtypeStruct((n, gp), F32), jax.ShapeDtypeStruct((n, gp), F32)],
        scratch_shapes=[pltpu.VMEM((n, SSM_WIDTH), F32)],
        compiler_params=_cparams(("arbitrary",)),
        name="s5_sample",
    )(za, za, h0r, h0i, w1, w2, pw_r, pw_i, d_skip.reshape(1, -1), g_out.reshape(1, -1))


def _cmp_weights(w_cmp, pe_cmp):
    eye = jnp.eye(KV_GROUPS, dtype=F32)

    def bd(w):
        return jnp.einsum('sjde,gh->sjgdhe', w, eye).reshape(2, CMP_STRIDE, KV_WIDTH, KV_WIDTH)

    wbd = jnp.concatenate([bd(w_cmp[:, :CMP_STRIDE]), bd(w_cmp[:, CMP_STRIDE:])], axis=-1).astype(BF16)
    pe8 = jnp.zeros((2, SUBLANES, CMP_BLOCK * HEAD_DIM), F32).at[:, 0].set(pe_cmp.reshape(2, -1)).astype(BF16)
    wb = jnp.tile(w_cmp.reshape(2, CMP_BLOCK * HEAD_DIM, HEAD_DIM), (1, 1, KV_GROUPS)).astype(BF16)
    return wbd, pe8, wb


def _split3(x):
    a = x.astype(BF16)
    r = x - a.astype(F32)
    b = r.astype(BF16)
    c = (r - b.astype(F32)).astype(BF16)
    return a, b, c


def _dot_exact_left(e, x):
    a, b, c = _split3(x)
    return (jnp.dot(e, a, preferred_element_type=F32) + jnp.dot(e, b, preferred_element_type=F32)
            + jnp.dot(e, c, preferred_element_type=F32))


def _dot_exact_right(x, e):
    a, b, c = _split3(x)
    return (jnp.dot(a, e, preferred_element_type=F32) + jnp.dot(b, e, preferred_element_type=F32)
            + jnp.dot(c, e, preferred_element_type=F32))


def _rank(score, n_cand):
    n_tiles = score.shape[0] // SUBLANES
    tiles = [score[v * SUBLANES:(v + 1) * SUBLANES] for v in range(n_tiles)]
    ranks = [jnp.zeros(t.shape, F32) for t in tiles]
    row = lax.broadcasted_iota(jnp.int32, (SUBLANES, 1), 0)
    for i in range(n_cand):
        si = tiles[i // SUBLANES][i % SUBLANES:i % SUBLANES + 1]
        for v in range(n_tiles):
            ge = jnp.where(si >= tiles[v], 1.0, 0.0)
            gt = jnp.where(si > tiles[v], 1.0, 0.0)
            if i < v * SUBLANES:
                beats = ge
            elif i >= (v + 1) * SUBLANES:
                beats = gt
            else:
                beats = jnp.where(row > i % SUBLANES, ge, gt)
            ranks[v] = ranks[v] + beats
    return jnp.concatenate(ranks, axis=0)


def _col_reduce(x, op, final):
    slabs = [x[i * SUBLANES:(i + 1) * SUBLANES] for i in range(x.shape[0] // SUBLANES)]
    while len(slabs) > 1:
        pairs = [op(slabs[i], slabs[i + 1]) for i in range(0, len(slabs) - 1, 2)]
        slabs = pairs + ([slabs[-1]] if len(slabs) % 2 else [])
    return final(slabs[0], axis=0, keepdims=True)


def _softmax_stage(g, s, bias, pv_prev, m_ref, l_ref, acc_ref, p_ref, rows):
    n_tok = s.shape[1] // HEADS_PER_GROUP
    hp = HEADS_PER_GROUP // 2
    for h in range(2):
        cs = slice(h * hp * n_tok, (h + 1) * hp * n_tok)
        sh = s[:, cs]
        if bias is not None:
            sh = sh + jnp.concatenate([bias] * hp, axis=1)
        m_old = m_ref[g, :, cs]
        m_new = jnp.maximum(m_old, _col_reduce(sh, jnp.maximum, jnp.max))
        alpha = jnp.exp2(m_old - m_new)
        p = jnp.exp2(sh - m_new)
        l_ref[g, :, cs] = alpha * l_ref[g, :, cs] + _col_reduce(p, jnp.add, jnp.sum)
        pv_h = pv_prev if isinstance(pv_prev, float) else pv_prev[:, cs]
        acc_ref[g, :, cs] = alpha * (acc_ref[g, :, cs] + pv_h)
        m_ref[g, :, cs] = m_new
        p_ref[g, rows, cs] = p.astype(BF16)


def _cmpproj_kernel(x_ref, wbd_ref, pe_ref, wb_ref, kc_ref, vct_ref):
    nch = x_ref.shape[0]
    outs = []
    for s in range(2):
        acc = jnp.zeros((nch, 2 * KV_WIDTH), F32)
        for j in range(CMP_STRIDE):
            c0 = j * 2 * KV_WIDTH + s * KV_WIDTH
            acc = acc + jnp.dot(x_ref[:, c0:c0 + KV_WIDTH].astype(BF16), wbd_ref[s, j],
                                preferred_element_type=F32)
        bias = jnp.dot(pe_ref[s], wb_ref[s], preferred_element_type=F32)[0:1]
        outs.append(acc[:, :KV_WIDTH] + pltpu.roll(acc[:, KV_WIDTH:], nch - 1, 0) + bias)
    kc_ref[...] = outs[0].astype(BF16)
    vct_ref[...] = jnp.transpose(outs[1]).reshape(KV_GROUPS, HEAD_DIM, nch).astype(BF16)


def _cmpproj(kvc, b, s, wbd, pe8, wb):
    nch = s // CMP_STRIDE
    x = kvc.reshape(b, nch, CMP_STRIDE * 2 * KV_WIDTH)
    const = lambda shape: pl.BlockSpec(shape, lambda bi: (0,) * len(shape))
    return pl.pallas_call(
        _cmpproj_kernel,
        grid=(b,),
        in_specs=[pl.BlockSpec((None, nch, CMP_STRIDE * 2 * KV_WIDTH), lambda bi: (bi, 0, 0)),
                  const(wbd.shape), const(pe8.shape), const(wb.shape)],
        out_specs=[pl.BlockSpec((None, nch, KV_WIDTH), lambda bi: (bi, 0, 0)),
                   pl.BlockSpec((None, KV_GROUPS, HEAD_DIM, nch), lambda bi: (bi, 0, 0, 0))],
        out_shape=[jax.ShapeDtypeStruct((b, nch, KV_WIDTH), BF16),
                   jax.ShapeDtypeStruct((b, KV_GROUPS, HEAD_DIM, nch), BF16)],
        compiler_params=_cparams(("arbitrary",)),
        name="cmpproj",
    )(x, wbd, pe8, wb)


def _kvprep_kernel(kc_ref, ks_ref, kw_ref, kct_ref, kst_ref, ksa_ref, vsto_ref, kwo_ref, vwto_ref):
    tk = ks_ref.shape[0]
    kct_ref[...] = jnp.transpose(kc_ref[...]).reshape(2, KV_GROUPS, HEAD_DIM, tk)
    xs = ks_ref[...]
    xst = jnp.transpose(xs).reshape(2, KV_GROUPS, HEAD_DIM, tk)
    kst_ref[...] = xst
    vsto_ref[...] = xst[1].astype(BF16)
    row = lax.broadcasted_iota(jnp.int32, (tk, HEAD_DIM), 0)
    lane = lax.broadcasted_iota(jnp.int32, (tk, HEAD_DIM), 1)
    onehot = jnp.where(lane == (row // SLC_BLOCK) % (SEL_KT // SLC_BLOCK), 1.0, 0.0)
    ksa_ref[...] = jnp.concatenate(
        [piece for g in range(KV_GROUPS) for piece in (xs[:, g * HEAD_DIM:(g + 1) * HEAD_DIM], onehot)],
        axis=1).astype(BF16)
    xw = kw_ref[...]
    kwo_ref[...] = xw[:, :KV_WIDTH].astype(BF16)
    vwto_ref[...] = jnp.transpose(xw[:, KV_WIDTH:]).reshape(KV_GROUPS, HEAD_DIM, tk).astype(BF16)


def _kvprep(kvc, kvs, kvw, b, s, tk):
    nt = s // tk
    in_spec = pl.BlockSpec((tk, 2 * KV_WIDTH), lambda bi, ti: (bi * nt + ti, 0))
    t_spec = pl.BlockSpec((None, 2, KV_GROUPS, HEAD_DIM, tk), lambda bi, ti: (bi, 0, 0, 0, ti))
    t_shape = jax.ShapeDtypeStruct((b, 2, KV_GROUPS, HEAD_DIM, s), F32)
    vt_spec = pl.BlockSpec((None, KV_GROUPS, HEAD_DIM, tk), lambda bi, ti: (bi, 0, 0, ti))
    vt_shape = jax.ShapeDtypeStruct((b, KV_GROUPS, HEAD_DIM, s), BF16)
    return pl.pallas_call(
        _kvprep_kernel,
        grid=(b, nt),
        in_specs=[in_spec, in_spec, in_spec],
        out_specs=[t_spec, t_spec,
                   pl.BlockSpec((None, tk, KV_GROUPS * LANES), lambda bi, ti: (bi, ti, 0)), vt_spec,
                   pl.BlockSpec((None, tk, KV_WIDTH), lambda bi, ti: (bi, ti, 0)), vt_spec],
        out_shape=[t_shape, t_shape, jax.ShapeDtypeStruct((b, s, KV_GROUPS * LANES), BF16), vt_shape,
                   jax.ShapeDtypeStruct((b, s, KV_WIDTH), BF16), vt_shape],
        compiler_params=_cparams(("arbitrary", "arbitrary")),
        name="kvprep",
    )(kvc, kvs, kvw)


def _nsa_prompt_kernel(q_ref, gt_ref, ks_ref, vst_ref, kw_ref, vwt_ref, kc_ref, vct_ref, esum_ref, go_ref,
                       o_ref, yt_ref, qst_ref, selb_ref, oc_ref, os_ref, m_ref, l_ref, acc_ref,
                       sa_ref, sb_ref, pa_ref, pb_ref, rhs_ref, *, n_blk, n_sel):
    qi = pl.program_id(1)
    q0 = qi * Q_TILE
    ncb = kc_ref.shape[0]
    bpt = SEL_KT // SLC_BLOCK
    tcol = lax.broadcasted_iota(jnp.int32, (1, Q_TILE), 1)
    tpos = q0 + tcol
    tpos4 = jnp.concatenate([tpos] * HEADS_PER_GROUP, axis=1)
    cend = lax.broadcasted_iota(jnp.int32, (ncb, 1), 0) * CMP_STRIDE + (CMP_BLOCK - 1)
    blk = lax.broadcasted_iota(jnp.int32, (n_blk, 1), 0)
    cur = tpos // SLC_BLOCK
    valid = blk <= cur
    forced = (blk == 0) | (blk == cur) | (blk == cur - 1)

    def keys(ref, k0, n, g):
        lane0 = (g // 2) * LANES
        return ref[pl.ds(k0, n), lane0:lane0 + LANES][:, (g % 2) * HEAD_DIM:(g % 2 + 1) * HEAD_DIM]

    def reset():
        m_ref[...] = jnp.full(m_ref.shape, NEG_INF, F32)
        l_ref[...] = jnp.zeros(l_ref.shape, F32)
        acc_ref[...] = jnp.zeros(acc_ref.shape, F32)

    for g in range(KV_GROUPS):
        qt = jnp.transpose(q_ref[:, g * 256:(g + 1) * 256] * ATTN_SCALE)
        qf = jnp.concatenate([qt[r * HEAD_DIM:(r + 1) * HEAD_DIM] for r in range(HEADS_PER_GROUP)], axis=1)
        qst = qf.astype(BF16)
        qs2 = (qf * LOG2E).astype(BF16)
        qst_ref[g] = qs2
        rhs_ref[g, 0:HEAD_DIM, :] = qs2
        rhs_ref[g, HEAD_DIM:, :] = jnp.zeros((LANES - HEAD_DIM, HEADS_PER_GROUP * Q_TILE), BF16)
        sc = jnp.dot(keys(kc_ref, 0, ncb, g), qst, preferred_element_type=F32)
        sc = jnp.where(cend <= tpos4, sc, NEG_INF)
        mx = jnp.max(sc, axis=0, keepdims=True)
        e = jnp.exp(sc - mx)
        den = jnp.sum(e, axis=0, keepdims=True)
        p = e * jnp.where(mx > 0.5 * NEG_INF, 1.0 / den, 0.0)
        oc_ref[g] = jnp.dot(vct_ref[g], p.astype(BF16), preferred_element_type=F32)
        imp = p[:, 0:Q_TILE]
        for r in range(1, HEADS_PER_GROUP):
            imp = imp + p[:, r * Q_TILE:(r + 1) * Q_TILE]
        impb = _dot_exact_left(esum_ref[...], imp)
        score = jnp.where(valid, impb + jnp.where(forced, FORCE_BONUS, 0.0), NEG_INF)
        selb_ref[g] = jnp.where(_rank(score, n_blk) < n_sel, 0.0, NEG_INF)

    bufs = ((sa_ref, pa_ref), (sb_ref, pb_ref))
    srows = slice(0, WIN_KT)
    krow = lax.broadcasted_iota(jnp.int32, (WIN_KT, 1), 0)
    zeros_pv = [0.0] * KV_GROUPS

    def static_tiles(tiles, pv):
        def scores(t, s_ref):
            for g in range(KV_GROUPS):
                s_ref[g, srows, :] = jnp.dot(tiles[t][0](g), qst_ref[g], preferred_element_type=F32)

        def pvs(t, p_ref):
            return [jnp.dot(tiles[t][1](g), p_ref[g, srows, :], preferred_element_type=F32)
                    for g in range(KV_GROUPS)]

        scores(0, sa_ref)
        for t in range(len(tiles)):
            (s_cur, p_cur), (s_nxt, p_prv) = bufs[t % 2], bufs[(t + 1) % 2]
            if t + 1 < len(tiles):
                scores(t + 1, s_nxt)
            if t > 0:
                pv = pvs(t - 1, p_prv)
            for g in range(KV_GROUPS):
                _softmax_stage(g, s_cur[g, srows, :], tiles[t][2](g), pv[g], m_ref, l_ref, acc_ref, p_cur, srows)
        return pvs(len(tiles) - 1, bufs[(len(tiles) - 1) % 2][1])

    n_full = qi // (SEL_KT // Q_TILE)
    last_full = jnp.maximum(n_full - 1, 0)

    def sel_scores(kt, s_ref):
        kc = jnp.clip(kt, 0, last_full)
        k0 = pl.multiple_of(kc * SEL_KT, SEL_KT)
        live = kt < n_full
        pad = jnp.zeros((2 * SUBLANES - bpt, HEADS_PER_GROUP * Q_TILE), F32)
        for g in range(KV_GROUPS):
            rows = jnp.where(live, selb_ref[g, pl.ds(kc * bpt, bpt), :], NEG_INF)
            rows = jnp.concatenate([jnp.concatenate([rows] * HEADS_PER_GROUP, axis=1), pad], axis=0)
            rhs_ref[g, HEAD_DIM:HEAD_DIM + 2 * SUBLANES, :] = rows.astype(BF16)
            s_ref[g] = jnp.dot(ks_ref[pl.ds(k0, SEL_KT), g * LANES:(g + 1) * LANES], rhs_ref[g],
                               preferred_element_type=F32)

    def sel_pv(kt, p_ref):
        k0 = pl.multiple_of(jnp.clip(kt, 0, last_full) * SEL_KT, SEL_KT)
        return [jnp.dot(vst_ref[g, :, pl.ds(k0, SEL_KT)], p_ref[g], preferred_element_type=F32)
                for g in range(KV_GROUPS)]

    def sel_step(kt, s_cur, s_nxt, p_cur, p_prv):
        sel_scores(kt + 1, s_nxt)
        pv = sel_pv(kt - 1, p_prv)
        for g in range(KV_GROUPS):
            _softmax_stage(g, s_cur[g], None, pv[g], m_ref, l_ref, acc_ref, p_cur, slice(0, SEL_KT))

    def sel_pair(i, carry):
        sel_step(2 * i, sa_ref, sb_ref, pa_ref, pb_ref)
        sel_step(2 * i + 1, sb_ref, sa_ref, pb_ref, pa_ref)
        return carry

    reset()
    pb_ref[...] = jnp.zeros(pb_ref.shape, BF16)
    sel_scores(0, sa_ref)
    n_pairs = (n_full + 1) // 2
    lax.fori_loop(0, n_pairs, sel_pair, 0)
    pv = sel_pv(2 * n_pairs - 1, pb_ref)

    def block_bias(g, blk0):
        rows = selb_ref[g, pl.ds(blk0, WIN_KT // SLC_BLOCK), :]
        return jnp.concatenate([jnp.broadcast_to(rows[i:i + 1], (SLC_BLOCK, Q_TILE))
                                for i in range(WIN_KT // SLC_BLOCK)], axis=0)

    def sel_tail_tile(k0, blk0, extra):
        return (lambda g: ks_ref[pl.ds(k0, WIN_KT), g * LANES:(g + 1) * LANES][:, :HEAD_DIM],
                lambda g: vst_ref[g, :, pl.ds(k0, WIN_KT)],
                lambda g: block_bias(g, blk0) + extra)

    bpq = Q_TILE // SLC_BLOCK
    odd = jnp.where(qi % (SEL_KT // Q_TILE) == 1, 0.0, NEG_INF)
    prev0 = pl.multiple_of(jnp.maximum(qi - 1, 0) * Q_TILE, Q_TILE)
    pv = static_tiles([sel_tail_tile(pl.multiple_of(q0, Q_TILE), qi * bpq, jnp.where(krow <= tcol, 0.0, NEG_INF)),
                       sel_tail_tile(prev0, jnp.maximum(qi - 1, 0) * bpq, odd)], pv)
    for g in range(KV_GROUPS):
        os_ref[g] = (acc_ref[g] + pv[g]) / jnp.maximum(l_ref[g], 1e-30)

    reset()
    n_back = WINDOW // WIN_KT

    def win_tile(t):
        k0 = pl.multiple_of(jnp.maximum(qi - t, 0) * WIN_KT, WIN_KT)
        if t == 0:
            bias = jnp.where(krow <= tcol, 0.0, NEG_INF)
        else:
            off = jnp.where(qi - t >= 0, 0.0, NEG_INF)
            bias = jnp.where(krow > tcol, off, NEG_INF) if t == n_back else jnp.zeros((WIN_KT, Q_TILE), F32) + off
        return (lambda g: keys(kw_ref, k0, WIN_KT, g), lambda g: vwt_ref[g, :, pl.ds(k0, WIN_KT)], lambda g: bias)

    pv = static_tiles([win_tile(t) for t in range(n_back + 1)], zeros_pv)

    gates = jax.nn.sigmoid(jnp.transpose(gt_ref[...]))
    for g in range(KV_GROUPS):
        o_c, o_s = oc_ref[g], os_ref[g]
        o_w = (acc_ref[g] + pv[g]) / jnp.maximum(l_ref[g], 1e-30)
        for r in range(HEADS_PER_GROUP):
            cs = slice(r * Q_TILE, (r + 1) * Q_TILE)
            gi = (g * HEADS_PER_GROUP + r) * 3
            y = (gates[gi:gi + 1] * o_c[:, cs] + gates[gi + 1:gi + 2] * o_s[:, cs]
                 + gates[gi + 2:gi + 3] * o_w[:, cs])
            row0 = (g * HEADS_PER_GROUP + r) * HEAD_DIM
            yt_ref[row0:row0 + HEAD_DIM, :] = y

    o_ref[...] = _rms(jnp.transpose(yt_ref[...]), go_ref[...]).astype(BF16)


def _esum_matrix(n_blk, n_rows, row_of_block0):
    r = SLC_BLOCK // CMP_STRIDE
    j = np.arange(n_blk)[:, None]
    i = np.arange(n_rows)[None, :] - row_of_block0
    e = (i >= r * j - 1) & (i <= r * j + r - 1) & (i >= 0)
    return jnp.asarray(e, BF16)


def _nsa_prompt(za, gates, ks, vst, kw, vwt, kc, vct, g_out, b, s):
    assert WINDOW % WIN_KT == 0 and WIN_KT == Q_TILE and s % SEL_KT == 0
    nqt = s // Q_TILE
    nq = HEADS_PER_GROUP * Q_TILE
    n_blk = s // SLC_BLOCK
    n_sel = min(N_SEL, n_blk)
    ncb = s // CMP_STRIDE
    esum = _esum_matrix(n_blk, ncb, 0) * jnp.asarray(np.arange(ncb)[None, :] < ncb - 1, BF16)
    qcol = 2 * SSM_WIDTH // ATT_WIDTH
    per_b3 = lambda shape: pl.BlockSpec((None,) + shape, lambda bi, qi: (bi, 0, 0))
    per_b4 = lambda shape: pl.BlockSpec((None,) + shape, lambda bi, qi: (bi, 0, 0, 0))
    return pl.pallas_call(
        functools.partial(_nsa_prompt_kernel, n_blk=n_blk, n_sel=n_sel),
        grid=(b, nqt),
        in_specs=[pl.BlockSpec((Q_TILE, ATT_WIDTH), lambda bi, qi: (bi * nqt + qi, qcol)),
                  pl.BlockSpec((Q_TILE, GATE_PAD), lambda bi, qi: (bi * nqt + qi, 0)),
                  per_b3((s, KV_GROUPS * LANES)), per_b4((KV_GROUPS, HEAD_DIM, s)),
                  per_b3((s, KV_WIDTH)), per_b4((KV_GROUPS, HEAD_DIM, s)),
                  per_b3((ncb, KV_WIDTH)), per_b4((KV_GROUPS, HEAD_DIM, ncb)),
                  pl.BlockSpec((n_blk, ncb), lambda bi, qi: (0, 0)),
                  pl.BlockSpec((1, ATT_WIDTH), lambda bi, qi: (0, 0))],
        out_specs=pl.BlockSpec((Q_TILE, ATT_WIDTH), lambda bi, qi: (bi * nqt + qi, 0)),
        out_shape=jax.ShapeDtypeStruct((b * s, ATT_WIDTH), BF16),
        scratch_shapes=[pltpu.VMEM((ATT_WIDTH, Q_TILE), F32),
                        pltpu.VMEM((KV_GROUPS, HEAD_DIM, nq), BF16),
                        pltpu.VMEM((KV_GROUPS, n_blk, Q_TILE), F32),
                        pltpu.VMEM((KV_GROUPS, HEAD_DIM, nq), F32), pltpu.VMEM((KV_GROUPS, HEAD_DIM, nq), F32),
                        pltpu.VMEM((KV_GROUPS, 1, nq), F32), pltpu.VMEM((KV_GROUPS, 1, nq), F32),
                        pltpu.VMEM((KV_GROUPS, HEAD_DIM, nq), F32),
                        pltpu.VMEM((KV_GROUPS, SEL_KT, nq), F32), pltpu.VMEM((KV_GROUPS, SEL_KT, nq), F32),
                        pltpu.VMEM((KV_GROUPS, SEL_KT, nq), BF16), pltpu.VMEM((KV_GROUPS, SEL_KT, nq), BF16),
                        pltpu.VMEM((KV_GROUPS, LANES, nq), BF16)],
        compiler_params=_cparams(("arbitrary", "arbitrary")),
        name="nsa_prompt",
    )(za, gates, ks, vst, kw, vwt, kc, vct, esum, g_out.reshape(1, -1))


def _cmp_pages_kernel(pt_ref, *refs, pg):
    x_refs = refs[:pg]
    wbd_ref, pe_ref, wb_ref, kc_ref, vc_ref, xk_ref, xv_ref, carry_ref = refs[pg:]
    rows = pg * (PAGE_SIZE // CMP_STRIDE)
    h = pl.program_id(1)

    @pl.when(h == 0)
    def _():
        carry_ref[...] = jnp.zeros_like(carry_ref)

    cpp = PAGE_SIZE // CMP_STRIDE
    pitch = rows + SUBLANES
    def regroup(s, xs_ref, pages):
        for k in pages:
            for gp in range(KV_GROUPS // 2):
                t = jnp.transpose(x_refs[k][s, 2 * gp:2 * gp + 2].reshape(2 * HEAD_DIM, PAGE_SIZE))
                for n in range(cpp):
                    xs_ref[gp, pl.ds(k * cpp + n, CMP_STRIDE, stride=pitch), :] = (
                        t[n * CMP_STRIDE:(n + 1) * CMP_STRIDE])

    regroup(0, xk_ref, range(pg))
    ppj = -(-pg // CMP_STRIDE)
    row = lax.broadcasted_iota(jnp.int32, (rows, 1), 0)
    for s, out_ref, xs_ref in ((0, kc_ref, xk_ref), (1, vc_ref, xv_ref)):
        acc = jnp.zeros((rows, 2 * KV_WIDTH), F32)
        for j in range(CMP_STRIDE):
            xs = jnp.concatenate([xs_ref[gp, j * pitch:j * pitch + rows, :]
                                  for gp in range(KV_GROUPS // 2)], axis=1).astype(BF16)
            acc = acc + jnp.dot(xs, wbd_ref[s, j], preferred_element_type=F32)
            if s == 0:
                regroup(1, xv_ref, range(min(j * ppj, pg), min((j + 1) * ppj, pg)))
        bias = jnp.dot(pe_ref[s], wb_ref[s], preferred_element_type=F32)[0:1]
        lo = acc[:, :KV_WIDTH]
        prev = jnp.where(row == 0, carry_ref[s, SUBLANES - 1:SUBLANES, :], pltpu.roll(lo, 1, 0))
        out_ref[...] = (prev + acc[:, KV_WIDTH:] + bias).astype(BF16)
        carry_ref[s] = lo[rows - SUBLANES:, :]


def _cmp_pages(cache_cmp, page_table, wbd, pe8, wb):
    db, n_pages = page_table.shape
    cpp = PAGE_SIZE // CMP_STRIDE
    pg = min(CMP_PG, n_pages)
    nh = n_pages // pg
    page_shape = cache_cmp.shape[1:]
    page_specs = [pl.BlockSpec((None,) + page_shape, functools.partial(
        lambda bi, hi, pt, k: (pt[bi, hi * pg + k], 0, 0, 0, 0), k=k)) for k in range(pg)]
    const = lambda shape: pl.BlockSpec(shape, lambda bi, hi, pt: (0,) * len(shape))
    out_spec = pl.BlockSpec((None, pg * cpp, KV_WIDTH), lambda bi, hi, pt: (bi, hi, 0))
    out_shape = jax.ShapeDtypeStruct((db, n_pages * cpp, KV_WIDTH), BF16)
    return pl.pallas_call(
        functools.partial(_cmp_pages_kernel, pg=pg),
        grid_spec=pltpu.PrefetchScalarGridSpec(
            num_scalar_prefetch=1,
            grid=(db, nh),
            in_specs=page_specs + [const(wbd.shape), const(pe8.shape), const(wb.shape)],
            out_specs=[out_spec, out_spec],
            scratch_shapes=[pltpu.VMEM((KV_GROUPS // 2, CMP_STRIDE * (pg * cpp + SUBLANES), LANES), F32),
                            pltpu.VMEM((KV_GROUPS // 2, CMP_STRIDE * (pg * cpp + SUBLANES), LANES), F32),
                            pltpu.VMEM((2, SUBLANES, KV_WIDTH), F32)]),
        out_shape=[out_shape, out_shape],
        compiler_params=_cparams(("arbitrary", "arbitrary")),
        name="cmp_pages",
    )(page_table, *([cache_cmp] * pg), wbd, pe8, wb)


def _query_blockdiag(q_ref):
    qt = jnp.transpose(q_ref[...] * ATTN_SCALE)
    tiled = jnp.concatenate([qt] * KV_GROUPS, axis=0)
    rowg = lax.broadcasted_iota(jnp.int32, (KV_WIDTH, 1), 0) // HEAD_DIM
    colg = lax.broadcasted_iota(jnp.int32, (1, LANES), 1) // HEADS_PER_GROUP
    return jnp.where(rowg == colg, tiled, 0.0).astype(BF16)


def _diag_heads(o):
    rowg = lax.broadcasted_iota(jnp.int32, (N_HEADS, 1), 0) // HEADS_PER_GROUP
    out = jnp.zeros((N_HEADS, HEAD_DIM), F32)
    for g in range(KV_GROUPS):
        out = out + jnp.where(rowg == g, o[:N_HEADS, g * HEAD_DIM:(g + 1) * HEAD_DIM], 0.0)
    return out


def _smp_cmp_kernel(q_ref, kc_ref, vc_ref, esum_ref, gsum_ref, oc_ref, imp_ref, *, qpos):
    qbd = _query_blockdiag(q_ref)
    nr = kc_ref.shape[0]
    s = jnp.dot(kc_ref[...], qbd, preferred_element_type=F32)
    row = lax.broadcasted_iota(jnp.int32, (nr, 1), 0)
    mf = jnp.where(row >= 1, jnp.where((row - 1) * CMP_STRIDE + CMP_BLOCK - 1 <= qpos, 1.0, 0.0), 0.0)
    s = jnp.where(mf > 0.5, s, NEG_INF)
    p = jnp.exp(s - jnp.max(s, axis=0, keepdims=True)) * mf
    p = p / jnp.maximum(jnp.sum(p, axis=0, keepdims=True), 1e-30)
    o = jnp.dot(jnp.transpose(p).astype(BF16), vc_ref[...], preferred_element_type=F32)
    oc_ref[...] = _diag_heads(o)
    impg = _dot_exact_right(p, gsum_ref[...])
    imp_ref[...] = _dot_exact_left(esum_ref[...], impg)


def _smp_cmp(q_pad, kc, vc, n_blk_pad, qpos):
    db, nr, _ = kc.shape
    n_blk = (qpos + SLC_BLOCK) // SLC_BLOCK
    esum = _esum_matrix(n_blk_pad, nr, 1) * jnp.asarray(np.arange(n_blk_pad)[:, None] < n_blk, BF16)
    gs = (np.arange(LANES)[:, None] // HEADS_PER_GROUP == np.arange(LANES)[None, :]) & (
        np.arange(LANES)[:, None] < N_HEADS)
    gsum = jnp.asarray(gs, BF16)
    per_b = lambda shape: pl.BlockSpec((None,) + shape, lambda bi: (bi, 0, 0))
    const = lambda shape: pl.BlockSpec(shape, lambda bi: (0,) * len(shape))
    return pl.pallas_call(
        functools.partial(_smp_cmp_kernel, qpos=qpos),
        grid=(db,),
        in_specs=[per_b((LANES, HEAD_DIM)), per_b((nr, KV_WIDTH)), per_b((nr, KV_WIDTH)),
                  const((n_blk_pad, nr)), const((LANES, LANES))],
        out_specs=[per_b((N_HEADS, HEAD_DIM)), per_b((n_blk_pad, LANES))],
        out_shape=[jax.ShapeDtypeStruct((db, N_HEADS, HEAD_DIM), F32),
                   jax.ShapeDtypeStruct((db, n_blk_pad, LANES), F32)],
        compiler_params=_cparams(("arbitrary",)),
        name="smp_cmp",
    )(q_pad, kc, vc, esum, gsum)


def _smp_topk_kernel(imp_ref, idx_ref, *, n_blk, n_sel):
    nbp = imp_ref.shape[0]
    blk = lax.broadcasted_iota(jnp.int32, (nbp, 1), 0)
    cur = n_blk - 1
    valid = blk <= cur
    forced = (blk == 0) | (blk == cur) | (blk == cur - 1)
    score = jnp.where(valid, imp_ref[...] + jnp.where(forced, FORCE_BONUS, 0.0), NEG_INF)
    rank = _rank(score, nbp)
    blkf = blk.astype(F32)
    for k in range(n_sel):
        pick = jnp.sum(jnp.where(rank == float(k), blkf, 0.0), axis=0, keepdims=True)
        idx_ref[k:k + 1, :] = pick.astype(jnp.int32)


def _smp_topk(score_t, n_blk, n_sel):
    nbp, w = score_t.shape
    full = lambda shape: pl.BlockSpec(shape, lambda: (0,) * len(shape))
    return pl.pallas_call(
        functools.partial(_smp_topk_kernel, n_blk=n_blk, n_sel=n_sel),
        in_specs=[full((nbp, w))],
        out_specs=full((n_sel, w)),
        out_shape=jax.ShapeDtypeStruct((n_sel, w), jnp.int32),
        name="smp_topk",
    )(score_t)


def _smp_attn_kernel(page_ref, half_ref, isnew_ref, *refs, n_slots, n_sel, past, wbuf):
    blk_refs = refs[:n_slots]
    (q_ref, ksn_ref, win_ref, kwn_ref, oc_ref, gt_ref, go_ref, o_ref, wout_ref) = refs[n_slots:]
    b = pl.program_id(0)
    lane_half = lax.broadcasted_iota(jnp.int32, (1, PAGE_SIZE), 1) // SLC_BLOCK
    row8 = lax.broadcasted_iota(jnp.int32, (SUBLANES, 1), 0)
    wlane = lax.broadcasted_iota(jnp.int32, (1, wbuf), 1)
    kposw = past - wbuf + wlane
    wbias = jnp.where((kposw > past - WINDOW) & (kposw >= 0), 0.0, NEG_INF)
    nt = (((1,), (1,)), ((), ()))

    def bf(x):
        return x.astype(BF16).astype(F32)

    ys = []
    for g in range(KV_GROUPS):
        q = q_ref[g] * ATTN_SCALE
        qb = q.astype(BF16)
        kn = bf(ksn_ref[:, g * HEAD_DIM:(g + 1) * HEAD_DIM])
        vn = bf(ksn_ref[:, KV_WIDTH + g * HEAD_DIM:KV_WIDTH + (g + 1) * HEAD_DIM])
        kwn = bf(kwn_ref[:, g * HEAD_DIM:(g + 1) * HEAD_DIM])
        vwn = bf(kwn_ref[:, KV_WIDTH + g * HEAD_DIM:KV_WIDTH + (g + 1) * HEAD_DIM])

        slots = range(g * n_sel, (g + 1) * n_sel)
        kt = jnp.concatenate([blk_refs[k][0] for k in slots], axis=1).astype(BF16)
        vt = jnp.concatenate([blk_refs[k][1] for k in slots], axis=1).astype(BF16)
        keeps = [jnp.where(isnew_ref[b, k] == 0, 1.0, 0.0) for k in slots]
        mf = jnp.concatenate([jnp.where(lane_half == half_ref[b, k], keep, 0.0)
                              for k, keep in zip(slots, keeps)], axis=1)
        has_new = 1.0 - functools.reduce(jnp.minimum, keeps)
        s = jnp.where(mf > 0.5, jnp.dot(qb, kt, preferred_element_type=F32), NEG_INF)
        s_n = jnp.where(has_new > 0.5, jnp.sum(bf(q) * kn, axis=1, keepdims=True), NEG_INF)
        m = jnp.maximum(jnp.max(s, axis=1, keepdims=True), s_n)
        p = jnp.exp(s - m) * mf
        p_n = jnp.exp(s_n - m) * has_new
        l = jnp.sum(p, axis=1, keepdims=True) + p_n
        o_s = (lax.dot_general(p.astype(BF16), vt, nt, preferred_element_type=F32) + bf(p_n) * vn)
        o_s = o_s / jnp.maximum(l, 1e-30)

        s = jnp.dot(qb, win_ref[0, g].astype(BF16), preferred_element_type=F32) + wbias
        s_n = jnp.sum(bf(q) * kwn, axis=1, keepdims=True)
        m = jnp.maximum(jnp.max(s, axis=1, keepdims=True), s_n)
        p = jnp.exp(s - m)
        p_n = jnp.exp(s_n - m)
        l = jnp.sum(p, axis=1, keepdims=True) + p_n
        o_w = (lax.dot_general(p.astype(BF16), win_ref[1, g].astype(BF16), nt, preferred_element_type=F32)
               + bf(p_n) * vwn)
        o_w = o_w / l

        gates = jax.nn.sigmoid(gt_ref[g])
        ys.append(gates[:, 0:1] * oc_ref[g] + gates[:, 1:2] * o_s + gates[:, 2:3] * o_w)

    real = row8 < HEADS_PER_GROUP
    ssq = functools.reduce(lambda a, c: a + c, [jnp.sum(jnp.where(real, y * y, 0.0), axis=1, keepdims=True)
                                                 for y in ys])
    scale = lax.rsqrt(jnp.sum(ssq, axis=0, keepdims=True) / ATT_WIDTH + EPS)
    for g in range(KV_GROUPS):
        o_ref[g] = (ys[g] * scale * go_ref[g]).astype(BF16)

    new_col = jnp.transpose(jnp.broadcast_to(kwn_ref[...], (SUBLANES, 2 * KV_WIDTH)))[:, 0:1]
    for s2 in range(2):
        for g in range(KV_GROUPS):
            c0 = (s2 * KV_GROUPS + g) * HEAD_DIM
            wout_ref[s2, g] = jnp.where(wlane == wbuf - 1, new_col[c0:c0 + HEAD_DIM],
                                        pltpu.roll(win_ref[s2, g], wbuf - 1, 1))


def _smp_attn(page, half, isnew, cache_slc, q_g, kvs_new, cache_win, kvw_new, o_c, gates, g_out, past):
    db, n_slots = page.shape
    n_sel = n_slots // KV_GROUPS
    wbuf = cache_win.shape[-1]
    width = 2 * KV_WIDTH
    blk_specs = [pl.BlockSpec((None, 2, None, HEAD_DIM, PAGE_SIZE), functools.partial(
        lambda bi, pg, hf, nw, k: (pg[bi, k], 0, k // n_sel, 0, 0), k=k)) for k in range(n_slots)]
    per_b3 = lambda shape: pl.BlockSpec((None,) + shape, lambda bi, pg, hf, nw: (bi, 0, 0))
    per_b4 = lambda shape: pl.BlockSpec((None,) + shape, lambda bi, pg, hf, nw: (bi, 0, 0, 0))
    per_b5 = lambda shape: pl.BlockSpec((None,) + shape, lambda bi, pg, hf, nw: (bi, 0, 0, 0, 0))
    win_shape = (2, KV_GROUPS, HEAD_DIM, wbuf)
    head_shape = (KV_GROUPS, SUBLANES, HEAD_DIM)
    return pl.pallas_call(
        functools.partial(_smp_attn_kernel, n_slots=n_slots, n_sel=n_sel, past=past, wbuf=wbuf),
        grid_spec=pltpu.PrefetchScalarGridSpec(
            num_scalar_prefetch=3,
            grid=(db,),
            in_specs=blk_specs + [per_b4(head_shape), per_b3((1, width)), per_b5(win_shape), per_b3((1, width)),
                                  per_b4(head_shape), per_b4((KV_GROUPS, SUBLANES, 3)),
                                  pl.BlockSpec(head_shape, lambda bi, pg, hf, nw: (0, 0, 0))],
            out_specs=[per_b4(head_shape), per_b5(win_shape)]),
        out_shape=[jax.ShapeDtypeStruct((db,) + head_shape, BF16),
                   jax.ShapeDtypeStruct((db,) + win_shape, F32)],
        compiler_params=_cparams(("arbitrary",)),
        name="smp_attn",
    )(page, half, isnew, *([cache_slc] * n_slots), q_g, kvs_new.reshape(db, 1, width), cache_win,
      kvw_new.reshape(db, 1, width), o_c, gates, g_out)


def _nsa_sample(za, gates_raw, kvs_new, kvw_new, cache_cmp, cache_slc, cache_win, page_table, wbd, pe8, wb,
                g_out):
    db, n_pages = page_table.shape
    past = n_pages * PAGE_SIZE
    n_blk = (past + SLC_BLOCK) // SLC_BLOCK
    n_sel = min(N_SEL, n_blk)
    n_blk_pad = -(-n_blk // SUBLANES) * SUBLANES
    q = za[:, 2 * SSM_WIDTH:].reshape(db, N_HEADS, HEAD_DIM)
    q_pad = jnp.pad(q, ((0, 0), (0, LANES - N_HEADS), (0, 0)))
    kc, vc = _cmp_pages(cache_cmp, page_table, wbd, pe8, wb)
    o_c, imp = _smp_cmp(q_pad, kc, vc, n_blk_pad, past)
    score_t = jnp.transpose(imp[:, :, :KV_GROUPS], (1, 0, 2)).reshape(n_blk_pad, db * KV_GROUPS)
    lane_pad = -(-db * KV_GROUPS // LANES) * LANES
    score_t = jnp.pad(score_t, ((0, 0), (0, lane_pad - db * KV_GROUPS)))
    idx = _smp_topk(score_t, n_blk, n_sel)[:, :db * KV_GROUPS]
    idx = jnp.transpose(idx.reshape(n_sel, db, KV_GROUPS), (1, 2, 0))
    n_past_blk = past // SLC_BLOCK
    per_page = PAGE_SIZE // SLC_BLOCK
    jp = jnp.minimum(idx, n_past_blk - 1).reshape(db, KV_GROUPS * n_sel)
    page = jnp.take_along_axis(page_table, jp // per_page, axis=1).astype(jnp.int32)
    half = (jp % per_page).astype(jnp.int32)
    isnew = (idx >= n_past_blk).reshape(db, KV_GROUPS * n_sel).astype(jnp.int32)
    pad_heads = lambda a: jnp.pad(a.reshape(a.shape[0], KV_GROUPS, HEADS_PER_GROUP, a.shape[-1]),
                                  ((0, 0), (0, 0), (0, SUBLANES - HEADS_PER_GROUP), (0, 0)))
    gates = pad_heads(gates_raw[:, :3 * N_HEADS].reshape(db, N_HEADS, 3))
    g_out_g = pad_heads(g_out.reshape(1, N_HEADS, HEAD_DIM))[0]
    y, win_new = _smp_attn(page, half, isnew, cache_slc, pad_heads(q), kvs_new, cache_win, kvw_new,
                           pad_heads(o_c), gates, g_out_g, past)
    return y[:, :, :HEADS_PER_GROUP].reshape(db, ATT_WIDTH), win_new


def _outproj_kernel(x_ref, ms_ref, ma_ref, w_ref, gt_ref, sc_ref, sh_ref, g_ref, x1_ref, h2_ref):
    mixed = jnp.concatenate([ms_ref[...], ma_ref[...]], axis=1)
    x1 = x_ref[...] + gt_ref[...] * jnp.dot(mixed, w_ref[...], preferred_element_type=F32)
    x1_ref[...] = x1
    h2_ref[...] = (_rms(x1, g_ref[...]) * (1.0 + sc_ref[...]) + sh_ref[...]).astype(BF16)


def _outproj(x, m_ssm, m_att, w_out, gt1, sc2, sh2, g2, tm, rows_per_mod):
    n, d = x.shape
    r = gt1.shape[1]
    tpm = rows_per_mod // tm
    mod_spec = pl.BlockSpec((None, r, d), lambda i: (i // tpm, 0, 0))
    row = lambda w: pl.BlockSpec((tm, w), lambda i: (i, 0))
    return pl.pallas_call(
        _outproj_kernel,
        grid=(n // tm,),
        in_specs=[row(d), row(SSM_WIDTH), row(ATT_WIDTH), pl.BlockSpec((d, d), lambda i: (0, 0)),
                  mod_spec, mod_spec, mod_spec, pl.BlockSpec((1, d), lambda i: (0, 0))],
        out_specs=[row(d), row(d)],
        out_shape=[jax.ShapeDtypeStruct((n, d), F32), jax.ShapeDtypeStruct((n, d), BF16)],
        compiler_params=_cparams(("arbitrary",)),
        name="outproj",
    )(x, m_ssm, m_att, w_out, gt1, sc2, sh2, g2.reshape(1, d))


def _ffn_act(a_v, a_g, a1_v, a1_g, a2_v, a2_g, cwv_ref, cwg_ref, cbv_ref, cbg_ref):
    val = cbv_ref[...] + cwv_ref[2:3, :] * a_v + cwv_ref[0:1, :] * a2_v + cwv_ref[1:2, :] * a1_v
    gate = cbg_ref[...] + cwg_ref[2:3, :] * a_g + cwg_ref[0:1, :] * a2_g + cwg_ref[1:2, :] * a1_g
    return (gate * jax.nn.sigmoid(gate) * val).astype(BF16)


def _ffn_finish(j, contrib, x1_ref, gt_ref, gf_ref, y_ref):
    @pl.when(j == 0)
    def _():
        y_ref[...] = contrib

    @pl.when(j > 0)
    def _():
        y_ref[...] += contrib

    @pl.when(j == FFN_NF - 1)
    def _():
        y_ref[...] = _rms(x1_ref[...] + gt_ref[...] * y_ref[...], gf_ref[...])


def _ffn_seq_kernel(h_ref, x1_ref, gt_ref, wv_ref, wg_ref, cwv_ref, cwg_ref, cbv_ref, cbg_ref, wd_ref, gf_ref,
                    y_ref, tv_ref, tg_ref, sv_ref, sg_ref, hv_ref, hg_ref, *, tm, tpb, rs):
    i = pl.program_id(0)
    j = pl.program_id(1)

    @pl.when(i % tpb == 0)
    def _():
        hv_ref[j] = jnp.zeros((SUBLANES, FFN_TF), F32)
        hg_ref[j] = jnp.zeros((SUBLANES, FFN_TF), F32)

    @pl.when(j == 0)
    def _():
        y_ref[...] = jnp.zeros(y_ref.shape, F32)

    sv_ref[0:SUBLANES, :] = hv_ref[j]
    sg_ref[0:SUBLANES, :] = hg_ref[j]

    def up(k):
        hk = h_ref[k * rs:(k + 1) * rs, :]
        o = SUBLANES + k * rs
        sv_ref[o:o + rs, :] = jnp.dot(hk, wv_ref[...], preferred_element_type=F32)
        sg_ref[o:o + rs, :] = jnp.dot(hk, wg_ref[...], preferred_element_type=F32)

    def down(k):
        o = SUBLANES + k * rs
        act = _ffn_act(sv_ref[o:o + rs, :], sg_ref[o:o + rs, :], sv_ref[o - 1:o - 1 + rs, :],
                       sg_ref[o - 1:o - 1 + rs, :], sv_ref[o - 2:o - 2 + rs, :], sg_ref[o - 2:o - 2 + rs, :],
                       cwv_ref, cwg_ref, cbv_ref, cbg_ref)
        y_ref[k * rs:(k + 1) * rs, :] += jnp.dot(act, wd_ref[...], preferred_element_type=F32)

    up(0)
    for k in range(tm // rs):
        if k + 1 < tm // rs:
            up(k + 1)
        down(k)

    for s_ref, halo_ref, t_ref in ((sv_ref, hv_ref, tv_ref), (sg_ref, hg_ref, tg_ref)):
        halo_ref[j] = s_ref[tm:tm + SUBLANES, :]
        t_ref[...] = s_ref[tm + SUBLANES - 2:tm + SUBLANES, :]

    @pl.when(j == FFN_NF - 1)
    def _():
        y_ref[...] = _rms(x1_ref[...] + gt_ref[...] * y_ref[...], gf_ref[...])


def _ffn_step_kernel(h_ref, x1_ref, gt_ref, wv_ref, wg_ref, cwv_ref, cwg_ref, cbv_ref, cbg_ref, wd_ref, gf_ref,
                     p0v_ref, p0g_ref, p1v_ref, p1g_ref, y_ref, av_ref, ag_ref):
    j = pl.program_id(1)
    h = h_ref[...]
    a_v = jnp.dot(h, wv_ref[...], preferred_element_type=F32)
    a_g = jnp.dot(h, wg_ref[...], preferred_element_type=F32)
    av_ref[...] = a_v
    ag_ref[...] = a_g
    act = _ffn_act(a_v, a_g, p1v_ref[...], p1g_ref[...], p0v_ref[...], p0g_ref[...],
                   cwv_ref, cwg_ref, cbv_ref, cbg_ref)
    _ffn_finish(j, jnp.dot(act, wd_ref[...], preferred_element_type=F32), x1_ref, gt_ref, gf_ref, y_ref)


def _ffn_specs(d, tm, r, tpm):
    row_once = pl.BlockSpec((tm, d), lambda i, j: (i, 0), pipeline_mode=pl.Buffered(1))
    return [row_once, row_once, pl.BlockSpec((None, r, d), lambda i, j: (i // tpm, 0, 0)),
            pl.BlockSpec((d, FFN_TF), lambda i, j: (0, j)), pl.BlockSpec((d, FFN_TF), lambda i, j: (0, j + FFN_NF)),
            pl.BlockSpec((CONV_W, FFN_TF), lambda i, j: (0, j)),
            pl.BlockSpec((CONV_W, FFN_TF), lambda i, j: (0, j + FFN_NF)),
            pl.BlockSpec((1, FFN_TF), lambda i, j: (0, j)), pl.BlockSpec((1, FFN_TF), lambda i, j: (0, j + FFN_NF)),
            pl.BlockSpec((FFN_TF, d), lambda i, j: (j, 0)), pl.BlockSpec((1, d), lambda i, j: (0, 0))]


def _ffn_seq(h2, x1, gt2, w_up, conv_w, conv_b, w_down, g_final, tm, rows_per_mod):
    n, d = x1.shape
    tpb = rows_per_mod // tm
    nt = n // tm
    cb = conv_b.reshape(1, -1)
    y, tv, tg = pl.pallas_call(
        functools.partial(_ffn_seq_kernel, tm=tm, tpb=tpb, rs=min(FFN_RS, tm)),
        grid=(nt, FFN_NF),
        in_specs=_ffn_specs(d, tm, gt2.shape[1], tpb),
        out_specs=[pl.BlockSpec((tm, d), lambda i, j: (i, 0), pipeline_mode=pl.Buffered(1)),
                   pl.BlockSpec((None, CONV_W - 1, FFN_TF), lambda i, j: (i, 0, j)),
                   pl.BlockSpec((None, CONV_W - 1, FFN_TF), lambda i, j: (i, 0, j))],
        out_shape=[jax.ShapeDtypeStruct((n, d), F32),
                   jax.ShapeDtypeStruct((nt, CONV_W - 1, D_FF), F32),
                   jax.ShapeDtypeStruct((nt, CONV_W - 1, D_FF), F32)],
        scratch_shapes=[pltpu.VMEM((tm + SUBLANES, FFN_TF), F32), pltpu.VMEM((tm + SUBLANES, FFN_TF), F32),
                        pltpu.VMEM((FFN_NF, SUBLANES, FFN_TF), F32), pltpu.VMEM((FFN_NF, SUBLANES, FFN_TF), F32)],
        compiler_params=_cparams(("arbitrary", "arbitrary")),
        name="ffn_seq",
    )(h2, x1, gt2, w_up, w_up, conv_w, conv_w, cb, cb, w_down, g_final.reshape(1, d))
    tails = jnp.concatenate([tv, tg], axis=-1)
    return y, tails[tpb - 1::tpb]


def _ffn_step(h2, x1, gt2, w_up, conv_w, conv_b, w_down, g_final, conv_prev):
    n, d = x1.shape
    cb = conv_b.reshape(1, -1)
    prev_v = pl.BlockSpec((n, FFN_TF), lambda i, j: (0, j))
    prev_g = pl.BlockSpec((n, FFN_TF), lambda i, j: (0, j + FFN_NF))
    p0, p1 = conv_prev[:, 0], conv_prev[:, 1]
    y, a_v, a_g = pl.pallas_call(
        _ffn_step_kernel,
        grid=(1, FFN_NF),
        in_specs=_ffn_specs(d, n, gt2.shape[1], 1) + [prev_v, prev_g, prev_v, prev_g],
        out_specs=[pl.BlockSpec((n, d), lambda i, j: (0, 0)),
                   pl.BlockSpec((n, FFN_TF), lambda i, j: (0, j)), pl.BlockSpec((n, FFN_TF), lambda i, j: (0, j))],
        out_shape=[jax.ShapeDtypeStruct((n, d), F32),
                   jax.ShapeDtypeStruct((n, D_FF), F32), jax.ShapeDtypeStruct((n, D_FF), F32)],
        compiler_params=_cparams(("arbitrary", "arbitrary")),
        name="ffn_step",
    )(h2, x1, gt2, w_up, w_up, conv_w, conv_w, cb, cb, w_down, g_final.reshape(1, d), p0, p0, p1, p1)
    return y, jnp.stack([p1, jnp.concatenate([a_v, a_g], axis=-1)], axis=1)


def kernel(x_prompt, x_sample, cache_cmp_kv, cache_slc_kv, cache_win_kv, state_ssm_re, state_ssm_im, state_conv,
           page_table, c_prompt, c_sample, w_ada, b_ada, g_norm1, w_in, ssm_lam_re, ssm_lam_im, ssm_log_dt,
           ssm_b_re, ssm_b_im, ssm_c_re, ssm_c_im, ssm_d, w_cmp, pe_cmp, g_out_ssm, g_out_att, w_out, g_norm2,
           w_up, conv_w, conv_b, w_down, g_final):
    depth = w_in.shape[0]
    b, s, d = x_prompt.shape
    db, ds, _ = x_sample.shape
    assert depth == 1 and ds == 1 and d == D_MODEL, "kernel is written for one layer and one new token per sequence"
    assert s % 512 == 0
    tm = 512
    tm_in = 256
    tm_ffn = min(1024, s)
    gp = SSM_GROUPS * SSM_STATE
    kv_shape = (2, KV_GROUPS, HEAD_DIM)
    l = 0

    w_in_p = jnp.pad(w_in[l], ((0, 0), (0, IN_PAD - IN_WIDTH))).astype(BF16)
    w_out_b = w_out[l].astype(BF16)
    w_up_b = w_up[l].astype(BF16)
    w_down_b = w_down[l].astype(BF16)
    pwr, pwi, bbr, bbi = _s5_prep(ssm_lam_re[l], ssm_lam_im[l], ssm_log_dt[l], ssm_b_re[l], ssm_b_im[l])
    w1, w2, lvl_r, lvl_i, pw_r, pw_i = _s5_weights(pwr, pwi, bbr, bbi, ssm_c_re[l], ssm_c_im[l])
    wbd, pe8, wb = _cmp_weights(w_cmp[l], pe_cmp[l])

    n_c = b + db
    n_c_pad = -(-n_c // SUBLANES) * SUBLANES
    c_all = jnp.pad(jnp.concatenate([c_prompt, c_sample], axis=0), ((0, n_c_pad - n_c), (0, 0)))
    mod = _ada(c_all, w_ada[l], b_ada[l]).reshape(n_c_pad, 6, d)
    mod_p = [mod[:b, k].reshape(b, 1, d) for k in range(6)]
    mod_s = [mod[b:n_c, k].reshape(1, db, d) for k in range(6)]

    xp = x_prompt.reshape(b * s, d)
    za, kvc, kvs, kvw, graw = _inproj(xp, mod_p[1], mod_p[0], g_norm1[l], w_in_p, tm_in, s)
    m_ssm, st_re, st_im = _s5_prompt(za, b, s, w1, w2, lvl_r, lvl_i, pw_r, pw_i, ssm_d[l], g_out_ssm[l], 256)
    kc, vct = _cmpproj(kvc, b, s, wbd, pe8, wb)
    kvc_t, kvs_t, ks_b, vst, kw_b, vwt = _kvprep(kvc, kvs, kvw, b, s, 512)
    rows_major = lambda a: jnp.transpose(a, (0, 4, 1, 2, 3))[None]
    m_att = _nsa_prompt(za, graw, ks_b, vst, kw_b, vwt, kc, vct, g_out_att[l], b, s)
    x1, h2 = _outproj(xp, m_ssm, m_att, w_out_b, mod_p[2], mod_p[4], mod_p[3], g_norm2[l], tm, s)
    y_p, conv_p = _ffn_seq(h2, x1, mod_p[5], w_up_b, conv_w[l], conv_b[l], w_down_b, g_final, tm_ffn, s)
    wlen = min(WINDOW, s)
    win_p = kvw.reshape(b, s, 2 * KV_WIDTH)[:, s - wlen:].reshape(b, wlen, *kv_shape)

    xs = x_sample.reshape(db, d)
    za_s, kvc_s, kvs_s, kvw_s, graw_s = _inproj(xs, mod_s[1], mod_s[0], g_norm1[l], w_in_p, db, db)
    m_ssm_s, st_re_s, st_im_s = _s5_sample(za_s, state_ssm_re[l].reshape(db, gp), state_ssm_im[l].reshape(db, gp),
                                           w1, w2, pw_r, pw_i, ssm_d[l], g_out_ssm[l])
    rows_minor = lambda c: jnp.transpose(c, (0, 2, 3, 4, 1))
    m_att_s, win_s = _nsa_sample(za_s, graw_s, kvs_s, kvw_s, rows_minor(cache_cmp_kv[l]),
                                 rows_minor(cache_slc_kv[l]), rows_minor(cache_win_kv[l]),
                                 page_table, wbd, pe8, wb, g_out_att[l])
    win_s = jnp.transpose(win_s, (0, 4, 1, 2, 3))
    x1_s, h2_s = _outproj(xs, m_ssm_s, m_att_s, w_out_b, mod_s[2], mod_s[4], mod_s[3], g_norm2[l], db, db)
    y_s, conv_s = _ffn_step(h2_s, x1_s, mod_s[5], w_up_b, conv_w[l], conv_b[l], w_down_b, g_final, state_conv[l])

    wbuf = cache_win_kv.shape[2]
    return (y_p.reshape(b, s, d), y_s.reshape(db, 1, d),
            rows_major(kvc_t), kvc_s.reshape(1, db, 1, *kv_shape),
            rows_major(kvs_t), kvs_s.reshape(1, db, 1, *kv_shape),
            win_p[None], win_s.reshape(1, db, wbuf, *kv_shape),
            st_re.reshape(1, b, SSM_GROUPS, SSM_STATE), st_im.reshape(1, b, SSM_GROUPS, SSM_STATE),
            st_re_s.reshape(1, db, SSM_GROUPS, SSM_STATE), st_im_s.reshape(1, db, SSM_GROUPS, SSM_STATE),
            conv_p[None], conv_s[None])
```

```python
import functools
import math

import jax
import jax.numpy as jnp
import numpy as np
from jax import lax
from jax.experimental import pallas as pl
from jax.experimental.pallas import tpu as pltpu

F32 = jnp.float32
BF16 = jnp.bfloat16

D_MODEL = 2048
SSM_WIDTH = D_MODEL // 2
ATT_WIDTH = D_MODEL - SSM_WIDTH
SSM_CH = 16
SSM_GROUPS = SSM_WIDTH // SSM_CH
SSM_STATE = 64
HEAD_DIM = 64
N_HEADS = ATT_WIDTH // HEAD_DIM
KV_GROUPS = 4
HEADS_PER_GROUP = N_HEADS // KV_GROUPS
KV_WIDTH = KV_GROUPS * HEAD_DIM
CMP_STRIDE = 16
CMP_BLOCK = 32
SLC_BLOCK = 64
N_SEL = 16
WINDOW = 512
PAGE_SIZE = 128
ATTN_SCALE = HEAD_DIM ** -0.5
NEG_INF = -1e30
LOG2E = math.log2(math.e)
FORCE_BONUS = 1e4
D_FF = 256 * ((8 * D_MODEL // 3 + 255) // 256)
CONV_W = 3
EPS = 1e-6
IN_WIDTH = 3 * SSM_WIDTH + 6 * KV_WIDTH + 3 * N_HEADS

LANES = 128
SUBLANES = 8
VMEM_LIMIT = 56 * 1024 * 1024

IN_TN = 512
IN_NA = 3 * SSM_WIDTH // IN_TN
IN_PAD = (IN_NA + 4) * IN_TN
GATE_PAD = LANES
SSM_LCH = 8 * SSM_STATE
SSM_NCH = SSM_GROUPS // 8
Q_TILE = 128
SEL_KT = 256
WIN_KT = 128
CMP_TAIL = SLC_BLOCK // CMP_STRIDE
CMP_PG = 32
FFN_TF = 512
FFN_NF = D_FF // FFN_TF
FFN_RS = 256


def _cparams(sem):
    return pltpu.CompilerParams(dimension_semantics=sem, vmem_limit_bytes=VMEM_LIMIT)


def _rms(x, g):
    return x * lax.rsqrt(jnp.mean(x * x, axis=-1, keepdims=True) + EPS) * g


def _ada_kernel(c_ref, w_ref, b_ref, o_ref):
    c = c_ref[...]
    a = (c * jax.nn.sigmoid(c)).astype(BF16)
    o_ref[...] = jnp.dot(a, w_ref[...].astype(BF16), preferred_element_type=F32) + b_ref[...]


def _ada(c_all, w_ada, b_ada):
    r, d = c_all.shape
    n = w_ada.shape[1]
    tn = 1024
    return pl.pallas_call(
        _ada_kernel,
        grid=(n // tn,),
        in_specs=[pl.BlockSpec((r, d), lambda j: (0, 0)),
                  pl.BlockSpec((d, tn), lambda j: (0, j)),
                  pl.BlockSpec((1, tn), lambda j: (0, j))],
        out_specs=pl.BlockSpec((r, tn), lambda j: (0, j)),
        out_shape=jax.ShapeDtypeStruct((r, n), F32),
        compiler_params=_cparams(("arbitrary",)),
        name="ada",
    )(c_all, w_ada, b_ada.reshape(1, n))


def _inproj_kernel(x_ref, sc_ref, sh_ref, g_ref, w_ref, za_ref, kc_ref, ks_ref, kw_ref, gt_ref):
    h = (_rms(x_ref[...], g_ref[...]) * (1.0 + sc_ref[...]) + sh_ref[...]).astype(BF16)
    for j in range(IN_PAD // IN_TN):
        z = jnp.dot(h, w_ref[:, j * IN_TN:(j + 1) * IN_TN], preferred_element_type=F32)
        if j < IN_NA:
            za_ref[:, j * IN_TN:(j + 1) * IN_TN] = z
        elif j < IN_NA + 3:
            (kc_ref, ks_ref, kw_ref)[j - IN_NA][...] = z
        else:
            gt_ref[...] = z[:, :GATE_PAD]


def _inproj(x, sc, sh, g, w_pad, tm, rows_per_mod):
    n, d = x.shape
    r = sc.shape[1]
    tpm = rows_per_mod // tm
    mod_spec = pl.BlockSpec((None, r, d), lambda i: (i // tpm, 0, 0))
    row = lambda w: pl.BlockSpec((tm, w), lambda i: (i, 0))
    return pl.pallas_call(
        _inproj_kernel,
        grid=(n // tm,),
        in_specs=[row(d), mod_spec, mod_spec, pl.BlockSpec((1, d), lambda i: (0, 0)),
                  pl.BlockSpec((d, IN_PAD), lambda i: (0, 0), pipeline_mode=pl.Buffered(1))],
        out_specs=[row(3 * SSM_WIDTH), row(2 * KV_WIDTH), row(2 * KV_WIDTH), row(2 * KV_WIDTH), row(GATE_PAD)],
        out_shape=[jax.ShapeDtypeStruct((n, 3 * SSM_WIDTH), F32),
                   jax.ShapeDtypeStruct((n, 2 * KV_WIDTH), F32),
                   jax.ShapeDtypeStruct((n, 2 * KV_WIDTH), F32),
                   jax.ShapeDtypeStruct((n, 2 * KV_WIDTH), F32),
                   jax.ShapeDtypeStruct((n, GATE_PAD), F32)],
        compiler_params=_cparams(("arbitrary",)),
        name="inproj",
    )(x, sc, sh, g.reshape(1, d), w_pad)


def _s5_prep_kernel(lre_ref, lim_ref, ldt_ref, lrex_ref, limx_ref, bre_ref, bim_ref,
                    pwr_ref, pwi_ref, bbr_ref, bbi_ref):
    dt = jnp.exp(ldt_ref[...])

    def disc(lre, lim):
        mag = jnp.exp(lre * dt)
        ab_re = mag * jnp.cos(lim * dt)
        ab_im = mag * jnp.sin(lim * dt)
        den = lre * lre + lim * lim
        f_re = ((ab_re - 1.0) * lre + ab_im * lim) / den
        f_im = (ab_im * lre - (ab_re - 1.0) * lim) / den
        return ab_re, ab_im, f_re, f_im

    ab_re, ab_im, _, _ = disc(lre_ref[...], lim_ref[...])
    pr, pi = ab_re, ab_im
    pwr_ref[0] = pr
    pwi_ref[0] = pi
    for k in range(1, SUBLANES):
        pr, pi = pr * ab_re - pi * ab_im, pr * ab_im + pi * ab_re
        pwr_ref[k] = pr
        pwi_ref[k] = pi
    _, _, f_re, f_im = disc(lrex_ref[...], limx_ref[...])
    b_re, b_im = bre_ref[...], bim_ref[...]
    bbr_ref[...] = f_re * b_re - f_im * b_im
    bbi_ref[...] = f_re * b_im + f_im * b_re


def _s5_prep(lam_re, lam_im, log_dt, b_re, b_im):
    g, p = lam_re.shape
    ch = b_re.shape[-1]
    lrex = jnp.repeat(lam_re, ch, axis=1)
    limx = jnp.repeat(lam_im, ch, axis=1)
    full = lambda shape: pl.BlockSpec(shape, lambda: (0,) * len(shape))
    return pl.pallas_call(
        _s5_prep_kernel,
        in_specs=[full((g, p)), full((g, p)), full((g, 1)), full((g, p * ch)), full((g, p * ch)),
                  full((g, p * ch)), full((g, p * ch))],
        out_specs=[full((SUBLANES, g, p)), full((SUBLANES, g, p)), full((g, p * ch)), full((g, p * ch))],
        out_shape=[jax.ShapeDtypeStruct((SUBLANES, g, p), F32), jax.ShapeDtypeStruct((SUBLANES, g, p), F32),
                   jax.ShapeDtypeStruct((g, p * ch), F32), jax.ShapeDtypeStruct((g, p * ch), F32)],
        name="s5_prep",
    )(lam_re, lam_im, log_dt.reshape(g, 1), lrex, limx, b_re.reshape(g, p * ch), b_im.reshape(g, p * ch))


def _s5_weights(pwr, pwi, bbr, bbi, c_re, c_im):
    g, p, ch = SSM_GROUPS, SSM_STATE, SSM_CH
    eye = jnp.eye(8, dtype=F32)

    def w_in(bb):
        bb = bb.reshape(SSM_NCH, 8, p, ch)
        return jnp.einsum('jgpc,gh->jgchp', bb, eye).reshape(SSM_NCH, 8 * ch, 8 * p)

    def w_out(c):
        c = c.reshape(SSM_NCH, 8, ch, p)
        return jnp.einsum('jgcp,gh->jgphc', c, eye).reshape(SSM_NCH, 8 * p, 8 * ch)

    w1 = jnp.concatenate([w_in(bbr), w_in(bbi)], axis=-1).astype(BF16)
    w2 = jnp.concatenate([w_out(c_re), w_out(-c_im)], axis=1).astype(BF16)
    pw_r = pwr.reshape(SUBLANES, g * p)
    pw_i = pwi.reshape(SUBLANES, g * p)
    tau = jnp.arange(SUBLANES)[:, None]
    lvl_r = jnp.stack([jnp.where(tau >= d, pw_r[d - 1][None, :], 0.0) for d in (1, 2, 4)])
    lvl_i = jnp.stack([jnp.where(tau >= d, pw_i[d - 1][None, :], 0.0) for d in (1, 2, 4)])
    return w1, w2, lvl_r, lvl_i, pw_r, pw_i


def _s5_post(y, u, g_glu, d_skip, g_out):
    y = y + d_skip * u
    y = jax.nn.gelu(y) * jax.nn.sigmoid(g_glu)
    return _rms(y, g_out).astype(BF16)


def _s5_scan_kernel(u_ref, gg_ref, w1_ref, w2_ref, lr_ref, li_ref, pr_ref, pi_ref, d_ref, go_ref,
                    o_ref, sr_ref, si_ref, br_ref, bi_ref, y_ref, hr_ref, hi_ref, *, tt):
    t = pl.program_id(1)
    nrt = tt // SUBLANES

    @pl.when(t == 0)
    def _():
        hr_ref[...] = jnp.zeros_like(hr_ref)
        hi_ref[...] = jnp.zeros_like(hi_ref)

    def chunk(j, carry):
        lo = pl.multiple_of(j * SSM_LCH, SSM_LCH)
        uo = pl.multiple_of(j * LANES, LANES)
        ub = u_ref[:, pl.ds(uo, LANES)].astype(BF16)
        bu = jnp.dot(ub, w1_ref[j], preferred_element_type=F32)
        xr = bu[:, :SSM_LCH].reshape(nrt, SUBLANES, SSM_LCH)
        xi = bu[:, SSM_LCH:].reshape(nrt, SUBLANES, SSM_LCH)
        for lvl, d in enumerate((1, 2, 4)):
            ar = lr_ref[lvl, :, pl.ds(lo, SSM_LCH)]
            ai = li_ref[lvl, :, pl.ds(lo, SSM_LCH)]
            zr = pltpu.roll(xr, d, 1)
            zi = pltpu.roll(xi, d, 1)
            xr, xi = xr + ar * zr - ai * zi, xi + ar * zi + ai * zr
        br_ref[...] = xr.reshape(tt, SSM_LCH)
        bi_ref[...] = xi.reshape(tt, SSM_LCH)
        pr = pr_ref[:, pl.ds(lo, SSM_LCH)]
        pi = pi_ref[:, pl.ds(lo, SSM_LCH)]

        def tile(k, h):
            hr, hi = h
            r0 = pl.multiple_of(k * SUBLANES, SUBLANES)
            vr = br_ref[pl.ds(r0, SUBLANES), :] + pr * hr - pi * hi
            vi = bi_ref[pl.ds(r0, SUBLANES), :] + pr * hi + pi * hr
            br_ref[pl.ds(r0, SUBLANES), :] = vr
            bi_ref[pl.ds(r0, SUBLANES), :] = vi
            return vr[SUBLANES - 1:, :], vi[SUBLANES - 1:, :]

        hr, hi = lax.fori_loop(0, nrt, tile, (hr_ref[:, pl.ds(lo, SSM_LCH)], hi_ref[:, pl.ds(lo, SSM_LCH)]))
        hr_ref[:, pl.ds(lo, SSM_LCH)] = hr
        hi_ref[:, pl.ds(lo, SSM_LCH)] = hi
        hcat = jnp.concatenate([br_ref[...], bi_ref[...]], axis=1).astype(BF16)
        y_ref[:, pl.ds(uo, LANES)] = jnp.dot(hcat, w2_ref[j], preferred_element_type=F32)
        return carry

    lax.fori_loop(0, SSM_NCH, chunk, 0)
    o_ref[...] = _s5_post(y_ref[...], u_ref[...], gg_ref[...], d_ref[...], go_ref[...])
    sr_ref[...] = hr_ref[...]
    si_ref[...] = hi_ref[...]


def _s5_prompt(za, b, s, w1, w2, lvl_r, lvl_i, pw_r, pw_i, d_skip, g_out, tt):
    nt = s // tt
    gp = SSM_GROUPS * SSM_STATE
    const2 = lambda shape: pl.BlockSpec(shape, lambda bi, ti: (0,) * len(shape))
    st_spec = pl.BlockSpec((None, 1, gp), lambda bi, ti: (bi, 0, 0))
    return pl.pallas_call(
        functools.partial(_s5_scan_kernel, tt=tt),
        grid=(b, nt),
        in_specs=[pl.BlockSpec((tt, SSM_WIDTH), lambda bi, ti: (bi * nt + ti, 0)),
                  pl.BlockSpec((tt, SSM_WIDTH), lambda bi, ti: (bi * nt + ti, 1)),
                  const2(w1.shape), const2(w2.shape), const2(lvl_r.shape), const2(lvl_i.shape),
                  const2(pw_r.shape), const2(pw_i.shape), const2((1, SSM_WIDTH)), const2((1, SSM_WIDTH))],
        out_specs=[pl.BlockSpec((tt, SSM_WIDTH), lambda bi, ti: (bi * nt + ti, 0)), st_spec, st_spec],
        out_shape=[jax.ShapeDtypeStruct((b * s, SSM_WIDTH), BF16),
                   jax.ShapeDtypeStruct((b, 1, gp), F32), jax.ShapeDtypeStruct((b, 1, gp), F32)],
        scratch_shapes=[pltpu.VMEM((tt, SSM_LCH), F32), pltpu.VMEM((tt, SSM_LCH), F32),
                        pltpu.VMEM((tt, SSM_WIDTH), F32),
                        pltpu.VMEM((1, gp), F32), pltpu.VMEM((1, gp), F32)],
        compiler_params=_cparams(("arbitrary", "arbitrary")),
        name="s5_prompt",
    )(za, za, w1, w2, lvl_r, lvl_i, pw_r, pw_i, d_skip.reshape(1, -1), g_out.reshape(1, -1))


def _s5_step_kernel(u_ref, gg_ref, h0r_ref, h0i_ref, w1_ref, w2_ref, pr_ref, pi_ref, d_ref, go_ref,
                    o_ref, sr_ref, si_ref, y_ref):
    for j in range(SSM_NCH):
        lo, uo = j * SSM_LCH, j * LANES
        bu = jnp.dot(u_ref[:, uo:uo + LANES].astype(BF16), w1_ref[j], preferred_element_type=F32)
        ar = pr_ref[0:1, lo:lo + SSM_LCH]
        ai = pi_ref[0:1, lo:lo + SSM_LCH]
        h0r = h0r_ref[:, lo:lo + SSM_LCH]
        h0i = h0i_ref[:, lo:lo + SSM_LCH]
        hr = bu[:, :SSM_LCH] + (ar * h0r - ai * h0i)
        hi = bu[:, SSM_LCH:] + (ar * h0i + ai * h0r)
        sr_ref[:, lo:lo + SSM_LCH] = hr
        si_ref[:, lo:lo + SSM_LCH] = hi
        hcat = jnp.concatenate([hr, hi], axis=1).astype(BF16)
        y_ref[:, uo:uo + LANES] = jnp.dot(hcat, w2_ref[j], preferred_element_type=F32)
    o_ref[...] = _s5_post(y_ref[...], u_ref[...], gg_ref[...], d_ref[...], go_ref[...])


def _s5_sample(za, h0r, h0i, w1, w2, pw_r, pw_i, d_skip, g_out):
    n = za.shape[0]
    gp = SSM_GROUPS * SSM_STATE
    full = lambda shape: pl.BlockSpec(shape, lambda i: (0,) * len(shape))
    return pl.pallas_call(
        _s5_step_kernel,
        grid=(1,),
        in_specs=[pl.BlockSpec((n, SSM_WIDTH), lambda i: (0, 0)), pl.BlockSpec((n, SSM_WIDTH), lambda i: (0, 1)),
                  full((n, gp)), full((n, gp)), full(w1.shape), full(w2.shape),
                  full(pw_r.shape), full(pw_i.shape), full((1, SSM_WIDTH)), full((1, SSM_WIDTH))],
        out_specs=[full((n, SSM_WIDTH)), full((n, gp)), full((n, gp))],
        out_shape=[jax.ShapeDtypeStruct((n, SSM_WIDTH), BF16),
                   jax.ShapeDtypeStruct((n, gp), F32), jax.ShapeDtypeStruct((n, gp), F32)],
        scratch_shapes=[pltpu.VMEM((n, SSM_WIDTH), F32)],
        compiler_params=_cparams(("arbitrary",)),
        name="s5_sample",
    )(za, za, h0r, h0i, w1, w2, pw_r, pw_i, d_skip.reshape(1, -1), g_out.reshape(1, -1))


def _cmp_weights(w_cmp, pe_cmp):
    eye = jnp.eye(KV_GROUPS, dtype=F32)

    def bd(w):
        return jnp.einsum('sjde,gh->sjgdhe', w, eye).reshape(2, CMP_STRIDE, KV_WIDTH, KV_WIDTH)

    wbd = jnp.concatenate([bd(w_cmp[:, :CMP_STRIDE]), bd(w_cmp[:, CMP_STRIDE:])], axis=-1).astype(BF16)
    pe8 = jnp.zeros((2, SUBLANES, CMP_BLOCK * HEAD_DIM), F32).at[:, 0].set(pe_cmp.reshape(2, -1)).astype(BF16)
    wb = jnp.tile(w_cmp.reshape(2, CMP_BLOCK * HEAD_DIM, HEAD_DIM), (1, 1, KV_GROUPS)).astype(BF16)
    return wbd, pe8, wb


def _split3(x):
    a = x.astype(BF16)
    r = x - a.astype(F32)
    b = r.astype(BF16)
    c = (r - b.astype(F32)).astype(BF16)
    return a, b, c


def _dot_exact_left(e, x):
    a, b, c = _split3(x)
    return (jnp.dot(e, a, preferred_element_type=F32) + jnp.dot(e, b, preferred_element_type=F32)
            + jnp.dot(e, c, preferred_element_type=F32))


def _dot_exact_right(x, e):
    a, b, c = _split3(x)
    return (jnp.dot(a, e, preferred_element_type=F32) + jnp.dot(b, e, preferred_element_type=F32)
            + jnp.dot(c, e, preferred_element_type=F32))


def _rank(score, n_cand):
    n_tiles = score.shape[0] // SUBLANES
    tiles = [score[v * SUBLANES:(v + 1) * SUBLANES] for v in range(n_tiles)]
    ranks = [jnp.zeros(t.shape, F32) for t in tiles]
    row = lax.broadcasted_iota(jnp.int32, (SUBLANES, 1), 0)
    for i in range(n_cand):
        si = tiles[i // SUBLANES][i % SUBLANES:i % SUBLANES + 1]
        for v in range(n_tiles):
            ge = jnp.where(si >= tiles[v], 1.0, 0.0)
            gt = jnp.where(si > tiles[v], 1.0, 0.0)
            if i < v * SUBLANES:
                beats = ge
            elif i >= (v + 1) * SUBLANES:
                beats = gt
            else:
                beats = jnp.where(row > i % SUBLANES, ge, gt)
            ranks[v] = ranks[v] + beats
    return jnp.concatenate(ranks, axis=0)


def _col_reduce(x, op, final):
    slabs = [x[i * SUBLANES:(i + 1) * SUBLANES] for i in range(x.shape[0] // SUBLANES)]
    while len(slabs) > 1:
        pairs = [op(slabs[i], slabs[i + 1]) for i in range(0, len(slabs) - 1, 2)]
        slabs = pairs + ([slabs[-1]] if len(slabs) % 2 else [])
    return final(slabs[0], axis=0, keepdims=True)


def _softmax_stage(g, s, bias, pv_prev, m_ref, l_ref, acc_ref, p_ref, rows):
    n_tok = s.shape[1] // HEADS_PER_GROUP
    hp = HEADS_PER_GROUP // 2
    for h in range(2):
        cs = slice(h * hp * n_tok, (h + 1) * hp * n_tok)
        sh = s[:, cs]
        if bias is not None:
            sh = sh + jnp.concatenate([bias] * hp, axis=1)
        m_old = m_ref[g, :, cs]
        m_new = jnp.maximum(m_old, _col_reduce(sh, jnp.maximum, jnp.max))
        alpha = jnp.exp2(m_old - m_new)
        p = jnp.exp2(sh - m_new)
        l_ref[g, :, cs] = alpha * l_ref[g, :, cs] + _col_reduce(p, jnp.add, jnp.sum)
        pv_h = pv_prev if isinstance(pv_prev, float) else pv_prev[:, cs]
        acc_ref[g, :, cs] = alpha * (acc_ref[g, :, cs] + pv_h)
        m_ref[g, :, cs] = m_new
        p_ref[g, rows, cs] = p.astype(BF16)


def _cmpproj_kernel(x_ref, wbd_ref, pe_ref, wb_ref, kc_ref, vct_ref):
    nch = x_ref.shape[0]
    outs = []
    for s in range(2):
        acc = jnp.zeros((nch, 2 * KV_WIDTH), F32)
        for j in range(CMP_STRIDE):
            c0 = j * 2 * KV_WIDTH + s * KV_WIDTH
            acc = acc + jnp.dot(x_ref[:, c0:c0 + KV_WIDTH].astype(BF16), wbd_ref[s, j],
                                preferred_element_type=F32)
        bias = jnp.dot(pe_ref[s], wb_ref[s], preferred_element_type=F32)[0:1]
        outs.append(acc[:, :KV_WIDTH] + pltpu.roll(acc[:, KV_WIDTH:], nch - 1, 0) + bias)
    kc_ref[...] = outs[0].astype(BF16)
    vct_ref[...] = jnp.transpose(outs[1]).reshape(KV_GROUPS, HEAD_DIM, nch).astype(BF16)


def _cmpproj(kvc, b, s, wbd, pe8, wb):
    nch = s // CMP_STRIDE
    x = kvc.reshape(b, nch, CMP_STRIDE * 2 * KV_WIDTH)
    const = lambda shape: pl.BlockSpec(shape, lambda bi: (0,) * len(shape))
    return pl.pallas_call(
        _cmpproj_kernel,
        grid=(b,),
        in_specs=[pl.BlockSpec((None, nch, CMP_STRIDE * 2 * KV_WIDTH), lambda bi: (bi, 0, 0)),
                  const(wbd.shape), const(pe8.shape), const(wb.shape)],
        out_specs=[pl.BlockSpec((None, nch, KV_WIDTH), lambda bi: (bi, 0, 0)),
                   pl.BlockSpec((None, KV_GROUPS, HEAD_DIM, nch), lambda bi: (bi, 0, 0, 0))],
        out_shape=[jax.ShapeDtypeStruct((b, nch, KV_WIDTH), BF16),
                   jax.ShapeDtypeStruct((b, KV_GROUPS, HEAD_DIM, nch), BF16)],
        compiler_params=_cparams(("arbitrary",)),
        name="cmpproj",
    )(x, wbd, pe8, wb)


def _kvprep_kernel(kc_ref, ks_ref, kw_ref, kct_ref, kst_ref, ksa_ref, vsto_ref, kwo_ref, vwto_ref):
    tk = ks_ref.shape[0]
    kct_ref[...] = jnp.transpose(kc_ref[...]).reshape(2, KV_GROUPS, HEAD_DIM, tk)
    xs = ks_ref[...]
    xst = jnp.transpose(xs).reshape(2, KV_GROUPS, HEAD_DIM, tk)
    kst_ref[...] = xst
    vsto_ref[...] = xst[1].astype(BF16)
    row = lax.broadcasted_iota(jnp.int32, (tk, HEAD_DIM), 0)
    lane = lax.broadcasted_iota(jnp.int32, (tk, HEAD_DIM), 1)
    onehot = jnp.where(lane == (row // SLC_BLOCK) % (SEL_KT // SLC_BLOCK), 1.0, 0.0)
    ksa_ref[...] = jnp.concatenate(
        [piece for g in range(KV_GROUPS) for piece in (xs[:, g * HEAD_DIM:(g + 1) * HEAD_DIM], onehot)],
        axis=1).astype(BF16)
    xw = kw_ref[...]
    kwo_ref[...] = xw[:, :KV_WIDTH].astype(BF16)
    vwto_ref[...] = jnp.transpose(xw[:, KV_WIDTH:]).reshape(KV_GROUPS, HEAD_DIM, tk).astype(BF16)


def _kvprep(kvc, kvs, kvw, b, s, tk):
    nt = s // tk
    in_spec = pl.BlockSpec((tk, 2 * KV_WIDTH), lambda bi, ti: (bi * nt + ti, 0))
    t_spec = pl.BlockSpec((None, 2, KV_GROUPS, HEAD_DIM, tk), lambda bi, ti: (bi, 0, 0, 0, ti))
    t_shape = jax.ShapeDtypeStruct((b, 2, KV_GROUPS, HEAD_DIM, s), F32)
    vt_spec = pl.BlockSpec((None, KV_GROUPS, HEAD_DIM, tk), lambda bi, ti: (bi, 0, 0, ti))
    vt_shape = jax.ShapeDtypeStruct((b, KV_GROUPS, HEAD_DIM, s), BF16)
    return pl.pallas_call(
        _kvprep_kernel,
        grid=(b, nt),
        in_specs=[in_spec, in_spec, in_spec],
        out_specs=[t_spec, t_spec,
                   pl.BlockSpec((None, tk, KV_GROUPS * LANES), lambda bi, ti: (bi, ti, 0)), vt_spec,
                   pl.BlockSpec((None, tk, KV_WIDTH), lambda bi, ti: (bi, ti, 0)), vt_spec],
        out_shape=[t_shape, t_shape, jax.ShapeDtypeStruct((b, s, KV_GROUPS * LANES), BF16), vt_shape,
                   jax.ShapeDtypeStruct((b, s, KV_WIDTH), BF16), vt_shape],
        compiler_params=_cparams(("arbitrary", "arbitrary")),
        name="kvprep",
    )(kvc, kvs, kvw)


def _nsa_prompt_kernel(q_ref, gt_ref, ks_ref, vst_ref, kw_ref, vwt_ref, kc_ref, vct_ref, esum_ref, go_ref,
                       o_ref, yt_ref, qst_ref, selb_ref, oc_ref, os_ref, m_ref, l_ref, acc_ref,
                       sa_ref, sb_ref, pa_ref, pb_ref, rhs_ref, *, n_blk, n_sel):
    qi = pl.program_id(1)
    q0 = qi * Q_TILE
    ncb = kc_ref.shape[0]
    bpt = SEL_KT // SLC_BLOCK
    tcol = lax.broadcasted_iota(jnp.int32, (1, Q_TILE), 1)
    tpos = q0 + tcol
    tpos4 = jnp.concatenate([tpos] * HEADS_PER_GROUP, axis=1)
    cend = lax.broadcasted_iota(jnp.int32, (ncb, 1), 0) * CMP_STRIDE + (CMP_BLOCK - 1)
    blk = lax.broadcasted_iota(jnp.int32, (n_blk, 1), 0)
    cur = tpos // SLC_BLOCK
    valid = blk <= cur
    forced = (blk == 0) | (blk == cur) | (blk == cur - 1)

    def keys(ref, k0, n, g):
        lane0 = (g // 2) * LANES
        return ref[pl.ds(k0, n), lane0:lane0 + LANES][:, (g % 2) * HEAD_DIM:(g % 2 + 1) * HEAD_DIM]

    def reset():
        m_ref[...] = jnp.full(m_ref.shape, NEG_INF, F32)
        l_ref[...] = jnp.zeros(l_ref.shape, F32)
        acc_ref[...] = jnp.zeros(acc_ref.shape, F32)

    for g in range(KV_GROUPS):
        qt = jnp.transpose(q_ref[:, g * 256:(g + 1) * 256] * ATTN_SCALE)
        qf = jnp.concatenate([qt[r * HEAD_DIM:(r + 1) * HEAD_DIM] for r in range(HEADS_PER_GROUP)], axis=1)
        qst = qf.astype(BF16)
        qs2 = (qf * LOG2E).astype(BF16)
        qst_ref[g] = qs2
        rhs_ref[g, 0:HEAD_DIM, :] = qs2
        rhs_ref[g, HEAD_DIM:, :] = jnp.zeros((LANES - HEAD_DIM, HEADS_PER_GROUP * Q_TILE), BF16)
        sc = jnp.dot(keys(kc_ref, 0, ncb, g), qst, preferred_element_type=F32)
        sc = jnp.where(cend <= tpos4, sc, NEG_INF)
        mx = jnp.max(sc, axis=0, keepdims=True)
        e = jnp.exp(sc - mx)
        den = jnp.sum(e, axis=0, keepdims=True)
        p = e * jnp.where(mx > 0.5 * NEG_INF, 1.0 / den, 0.0)
        oc_ref[g] = jnp.dot(vct_ref[g], p.astype(BF16), preferred_element_type=F32)
        imp = p[:, 0:Q_TILE]
        for r in range(1, HEADS_PER_GROUP):
            imp = imp + p[:, r * Q_TILE:(r + 1) * Q_TILE]
        impb = _dot_exact_left(esum_ref[...], imp)
        score = jnp.where(valid, impb + jnp.where(forced, FORCE_BONUS, 0.0), NEG_INF)
        selb_ref[g] = jnp.where(_rank(score, n_blk) < n_sel, 0.0, NEG_INF)

    bufs = ((sa_ref, pa_ref), (sb_ref, pb_ref))
    srows = slice(0, WIN_KT)
    krow = lax.broadcasted_iota(jnp.int32, (WIN_KT, 1), 0)
    zeros_pv = [0.0] * KV_GROUPS

    def static_tiles(tiles, pv):
        def scores(t, s_ref):
            for g in range(KV_GROUPS):
                s_ref[g, srows, :] = jnp.dot(tiles[t][0](g), qst_ref[g], preferred_element_type=F32)

        def pvs(t, p_ref):
            return [jnp.dot(tiles[t][1](g), p_ref[g, srows, :], preferred_element_type=F32)
                    for g in range(KV_GROUPS)]

        scores(0, sa_ref)
        for t in range(len(tiles)):
            (s_cur, p_cur), (s_nxt, p_prv) = bufs[t % 2], bufs[(t + 1) % 2]
            if t + 1 < len(tiles):
                scores(t + 1, s_nxt)
            if t > 0:
                pv = pvs(t - 1, p_prv)
            for g in range(KV_GROUPS):
                _softmax_stage(g, s_cur[g, srows, :], tiles[t][2](g), pv[g], m_ref, l_ref, acc_ref, p_cur, srows)
        return pvs(len(tiles) - 1, bufs[(len(tiles) - 1) % 2][1])

    n_full = qi // (SEL_KT // Q_TILE)
    last_full = jnp.maximum(n_full - 1, 0)

    def sel_scores(kt, s_ref):
        kc = jnp.clip(kt, 0, last_full)
        k0 = pl.multiple_of(kc * SEL_KT, SEL_KT)
        live = kt < n_full
        pad = jnp.zeros((2 * SUBLANES - bpt, HEADS_PER_GROUP * Q_TILE), F32)
        for g in range(KV_GROUPS):
            rows = jnp.where(live, selb_ref[g, pl.ds(kc * bpt, bpt), :], NEG_INF)
            rows = jnp.concatenate([jnp.concatenate([rows] * HEADS_PER_GROUP, axis=1), pad], axis=0)
            rhs_ref[g, HEAD_DIM:HEAD_DIM + 2 * SUBLANES, :] = rows.astype(BF16)
            s_ref[g] = jnp.dot(ks_ref[pl.ds(k0, SEL_KT), g * LANES:(g + 1) * LANES], rhs_ref[g],
                               preferred_element_type=F32)

    def sel_pv(kt, p_ref):
        k0 = pl.multiple_of(jnp.clip(kt, 0, last_full) * SEL_KT, SEL_KT)
        return [jnp.dot(vst_ref[g, :, pl.ds(k0, SEL_KT)], p_ref[g], preferred_element_type=F32)
                for g in range(KV_GROUPS)]

    def sel_step(kt, s_cur, s_nxt, p_cur, p_prv):
        sel_scores(kt + 1, s_nxt)
        pv = sel_pv(kt - 1, p_prv)
        for g in range(KV_GROUPS):
            _softmax_stage(g, s_cur[g], None, pv[g], m_ref, l_ref, acc_ref, p_cur, slice(0, SEL_KT))

    def sel_pair(i, carry):
        sel_step(2 * i, sa_ref, sb_ref, pa_ref, pb_ref)
        sel_step(2 * i + 1, sb_ref, sa_ref, pb_ref, pa_ref)
        return carry

    reset()
    pb_ref[...] = jnp.zeros(pb_ref.shape, BF16)
    sel_scores(0, sa_ref)
    n_pairs = (n_full + 1) // 2
    lax.fori_loop(0, n_pairs, sel_pair, 0)
    pv = sel_pv(2 * n_pairs - 1, pb_ref)

    def block_bias(g, blk0):
        rows = selb_ref[g, pl.ds(blk0, WIN_KT // SLC_BLOCK), :]
        return jnp.concatenate([jnp.broadcast_to(rows[i:i + 1], (SLC_BLOCK, Q_TILE))
                                for i in range(WIN_KT // SLC_BLOCK)], axis=0)

    def sel_tail_tile(k0, blk0, extra):
        return (lambda g: ks_ref[pl.ds(k0, WIN_KT), g * LANES:(g + 1) * LANES][:, :HEAD_DIM],
                lambda g: vst_ref[g, :, pl.ds(k0, WIN_KT)],
                lambda g: block_bias(g, blk0) + extra)

    bpq = Q_TILE // SLC_BLOCK
    odd = jnp.where(qi % (SEL_KT // Q_TILE) == 1, 0.0, NEG_INF)
    prev0 = pl.multiple_of(jnp.maximum(qi - 1, 0) * Q_TILE, Q_TILE)
    pv = static_tiles([sel_tail_tile(pl.multiple_of(q0, Q_TILE), qi * bpq, jnp.where(krow <= tcol, 0.0, NEG_INF)),
                       sel_tail_tile(prev0, jnp.maximum(qi - 1, 0) * bpq, odd)], pv)
    for g in range(KV_GROUPS):
        os_ref[g] = (acc_ref[g] + pv[g]) / jnp.maximum(l_ref[g], 1e-30)

    reset()
    n_back = WINDOW // WIN_KT

    def win_tile(t):
        k0 = pl.multiple_of(jnp.maximum(qi - t, 0) * WIN_KT, WIN_KT)
        if t == 0:
            bias = jnp.where(krow <= tcol, 0.0, NEG_INF)
        else:
            off = jnp.where(qi - t >= 0, 0.0, NEG_INF)
            bias = jnp.where(krow > tcol, off, NEG_INF) if t == n_back else jnp.zeros((WIN_KT, Q_TILE), F32) + off
        return (lambda g: keys(kw_ref, k0, WIN_KT, g), lambda g: vwt_ref[g, :, pl.ds(k0, WIN_KT)], lambda g: bias)

    pv = static_tiles([win_tile(t) for t in range(n_back + 1)], zeros_pv)

    gates = jax.nn.sigmoid(jnp.transpose(gt_ref[...]))
    for g in range(KV_GROUPS):
        o_c, o_s = oc_ref[g], os_ref[g]
        o_w = (acc_ref[g] + pv[g]) / jnp.maximum(l_ref[g], 1e-30)
        for r in range(HEADS_PER_GROUP):
            cs = slice(r * Q_TILE, (r + 1) * Q_TILE)
            gi = (g * HEADS_PER_GROUP + r) * 3
            y = (gates[gi:gi + 1] * o_c[:, cs] + gates[gi + 1:gi + 2] * o_s[:, cs]
                 + gates[gi + 2:gi + 3] * o_w[:, cs])
            row0 = (g * HEADS_PER_GROUP + r) * HEAD_DIM
            yt_ref[row0:row0 + HEAD_DIM, :] = y

    o_ref[...] = _rms(jnp.transpose(yt_ref[...]), go_ref[...]).astype(BF16)


def _esum_matrix(n_blk, n_rows, row_of_block0):
    r = SLC_BLOCK // CMP_STRIDE
    j = np.arange(n_blk)[:, None]
    i = np.arange(n_rows)[None, :] - row_of_block0
    e = (i >= r * j - 1) & (i <= r * j + r - 1) & (i >= 0)
    return jnp.asarray(e, BF16)


def _nsa_prompt(za, gates, ks, vst, kw, vwt, kc, vct, g_out, b, s):
    assert WINDOW % WIN_KT == 0 and WIN_KT == Q_TILE and s % SEL_KT == 0
    nqt = s // Q_TILE
    nq = HEADS_PER_GROUP * Q_TILE
    n_blk = s // SLC_BLOCK
    n_sel = min(N_SEL, n_blk)
    ncb = s // CMP_STRIDE
    esum = _esum_matrix(n_blk, ncb, 0) * jnp.asarray(np.arange(ncb)[None, :] < ncb - 1, BF16)
    qcol = 2 * SSM_WIDTH // ATT_WIDTH
    per_b3 = lambda shape: pl.BlockSpec((None,) + shape, lambda bi, qi: (bi, 0, 0))
    per_b4 = lambda shape: pl.BlockSpec((None,) + shape, lambda bi, qi: (bi, 0, 0, 0))
    return pl.pallas_call(
        functools.partial(_nsa_prompt_kernel, n_blk=n_blk, n_sel=n_sel),
        grid=(b, nqt),
        in_specs=[pl.BlockSpec((Q_TILE, ATT_WIDTH), lambda bi, qi: (bi * nqt + qi, qcol)),
                  pl.BlockSpec((Q_TILE, GATE_PAD), lambda bi, qi: (bi * nqt + qi, 0)),
                  per_b3((s, KV_GROUPS * LANES)), per_b4((KV_GROUPS, HEAD_DIM, s)),
                  per_b3((s, KV_WIDTH)), per_b4((KV_GROUPS, HEAD_DIM, s)),
                  per_b3((ncb, KV_WIDTH)), per_b4((KV_GROUPS, HEAD_DIM, ncb)),
                  pl.BlockSpec((n_blk, ncb), lambda bi, qi: (0, 0)),
                  pl.BlockSpec((1, ATT_WIDTH), lambda bi, qi: (0, 0))],
        out_specs=pl.BlockSpec((Q_TILE, ATT_WIDTH), lambda bi, qi: (bi * nqt + qi, 0)),
        out_shape=jax.ShapeDtypeStruct((b * s, ATT_WIDTH), BF16),
        scratch_shapes=[pltpu.VMEM((ATT_WIDTH, Q_TILE), F32),
                        pltpu.VMEM((KV_GROUPS, HEAD_DIM, nq), BF16),
                        pltpu.VMEM((KV_GROUPS, n_blk, Q_TILE), F32),
                        pltpu.VMEM((KV_GROUPS, HEAD_DIM, nq), F32), pltpu.VMEM((KV_GROUPS, HEAD_DIM, nq), F32),
                        pltpu.VMEM((KV_GROUPS, 1, nq), F32), pltpu.VMEM((KV_GROUPS, 1, nq), F32),
                        pltpu.VMEM((KV_GROUPS, HEAD_DIM, nq), F32),
                        pltpu.VMEM((KV_GROUPS, SEL_KT, nq), F32), pltpu.VMEM((KV_GROUPS, SEL_KT, nq), F32),
                        pltpu.VMEM((KV_GROUPS, SEL_KT, nq), BF16), pltpu.VMEM((KV_GROUPS, SEL_KT, nq), BF16),
                        pltpu.VMEM((KV_GROUPS, LANES, nq), BF16)],
        compiler_params=_cparams(("arbitrary", "arbitrary")),
        name="nsa_prompt",
    )(za, gates, ks, vst, kw, vwt, kc, vct, esum, g_out.reshape(1, -1))


def _cmp_pages_kernel(pt_ref, *refs, pg):
    x_refs = refs[:pg]
    wbd_ref, pe_ref, wb_ref, kvn_ref, kc_ref, vc_ref, tail_ref, xk_ref, xv_ref, carry_ref = refs[pg:]
    rows = pg * (PAGE_SIZE // CMP_STRIDE)
    row_t = lax.broadcasted_iota(jnp.int32, (SUBLANES, 1), 0)
    h = pl.program_id(1)

    @pl.when(h == 0)
    def _():
        carry_ref[...] = jnp.zeros_like(carry_ref)

    cpp = PAGE_SIZE // CMP_STRIDE
    pitch = rows + SUBLANES
    def regroup(s, xs_ref, pages):
        for k in pages:
            for gp in range(KV_GROUPS // 2):
                t = jnp.transpose(x_refs[k][s, 2 * gp:2 * gp + 2].reshape(2 * HEAD_DIM, PAGE_SIZE))
                for n in range(cpp):
                    xs_ref[gp, pl.ds(k * cpp + n, CMP_STRIDE, stride=pitch), :] = (
                        t[n * CMP_STRIDE:(n + 1) * CMP_STRIDE])

    regroup(0, xk_ref, range(pg))
    ppj = -(-pg // CMP_STRIDE)
    row = lax.broadcasted_iota(jnp.int32, (rows, 1), 0)
    for s, out_ref, xs_ref in ((0, kc_ref, xk_ref), (1, vc_ref, xv_ref)):
        acc = jnp.zeros((rows, 2 * KV_WIDTH), F32)
        for j in range(CMP_STRIDE):
            xs = jnp.concatenate([xs_ref[gp, j * pitch:j * pitch + rows, :]
                                  for gp in range(KV_GROUPS // 2)], axis=1).astype(BF16)
            acc = acc + jnp.dot(xs, wbd_ref[s, j], preferred_element_type=F32)
            if s == 0:
                regroup(1, xv_ref, range(min(j * ppj, pg), min((j + 1) * ppj, pg)))
        bias = jnp.dot(pe_ref[s], wb_ref[s], preferred_element_type=F32)[0:1]
        lo = acc[:, :KV_WIDTH]
        prev = jnp.where(row == 0, carry_ref[s, SUBLANES - 1:SUBLANES, :], pltpu.roll(lo, 1, 0))
        out_ref[...] = (prev + acc[:, KV_WIDTH:] + bias).astype(BF16)
        carry_ref[s] = lo[rows - SUBLANES:, :]
        xn = jnp.broadcast_to(kvn_ref[:, s * KV_WIDTH:(s + 1) * KV_WIDTH], (SUBLANES, KV_WIDTH)).astype(BF16)
        pn = jnp.dot(xn, wbd_ref[s, 0], preferred_element_type=F32)
        tail = jnp.where(row_t == 0, lo[rows - 1:rows, :] + pn[:, KV_WIDTH:], jnp.where(row_t == 1, pn[:, :KV_WIDTH], 0.0))
        tail_ref[s] = (tail + jnp.where(row_t < CMP_TAIL, bias, 0.0)).astype(BF16)


def _cmp_pages(cache_cmp, page_table, kvc_new, wbd, pe8, wb):
    db, n_pages = page_table.shape
    cpp = PAGE_SIZE // CMP_STRIDE
    pg = min(CMP_PG, n_pages)
    nh = n_pages // pg
    page_shape = cache_cmp.shape[1:]
    page_specs = [pl.BlockSpec((None,) + page_shape, functools.partial(
        lambda bi, hi, pt, k: (pt[bi, hi * pg + k], 0, 0, 0, 0), k=k)) for k in range(pg)]
    const = lambda shape: pl.BlockSpec(shape, lambda bi, hi, pt: (0,) * len(shape))
    out_spec = pl.BlockSpec((None, pg * cpp, KV_WIDTH), lambda bi, hi, pt: (bi, hi, 0))
    out_shape = jax.ShapeDtypeStruct((db, n_pages * cpp, KV_WIDTH), BF16)
    return pl.pallas_call(
        functools.partial(_cmp_pages_kernel, pg=pg),
        grid_spec=pltpu.PrefetchScalarGridSpec(
            num_scalar_prefetch=1,
            grid=(db, nh),
            in_specs=page_specs + [const(wbd.shape), const(pe8.shape), const(wb.shape),
                                   pl.BlockSpec((None, 1, 2 * KV_WIDTH), lambda bi, hi, pt: (bi, 0, 0))],
            out_specs=[out_spec, out_spec,
                       pl.BlockSpec((None, 2, SUBLANES, KV_WIDTH), lambda bi, hi, pt: (bi, 0, 0, 0))],
            scratch_shapes=[pltpu.VMEM((KV_GROUPS // 2, CMP_STRIDE * (pg * cpp + SUBLANES), LANES), F32),
                            pltpu.VMEM((KV_GROUPS // 2, CMP_STRIDE * (pg * cpp + SUBLANES), LANES), F32),
                            pltpu.VMEM((2, SUBLANES, KV_WIDTH), F32)]),
        out_shape=[out_shape, out_shape, jax.ShapeDtypeStruct((db, 2, SUBLANES, KV_WIDTH), BF16)],
        compiler_params=_cparams(("arbitrary", "arbitrary")),
        name="cmp_pages",
    )(page_table, *([cache_cmp] * pg), wbd, pe8, wb, kvc_new.reshape(db, 1, 2 * KV_WIDTH))


def _query_blockdiag(q_ref):
    qt = jnp.transpose(q_ref[...] * ATTN_SCALE)
    tiled = jnp.concatenate([qt] * KV_GROUPS, axis=0)
    rowg = lax.broadcasted_iota(jnp.int32, (KV_WIDTH, 1), 0) // HEAD_DIM
    colg = lax.broadcasted_iota(jnp.int32, (1, LANES), 1) // HEADS_PER_GROUP
    return jnp.where(rowg == colg, tiled, 0.0).astype(BF16)


def _diag_heads(o):
    rowg = lax.broadcasted_iota(jnp.int32, (N_HEADS, 1), 0) // HEADS_PER_GROUP
    out = jnp.zeros((N_HEADS, HEAD_DIM), F32)
    for g in range(KV_GROUPS):
        out = out + jnp.where(rowg == g, o[:N_HEADS, g * HEAD_DIM:(g + 1) * HEAD_DIM], 0.0)
    return out


def _smp_cmp_kernel(q_ref, kc_ref, vc_ref, tail_ref, esum_ref, esum_t_ref, gsum_ref, oc_ref, imp_ref, *, qpos):
    qbd = _query_blockdiag(q_ref)
    nr = kc_ref.shape[0]
    s = jnp.dot(kc_ref[...], qbd, preferred_element_type=F32)
    row = lax.broadcasted_iota(jnp.int32, (nr, 1), 0)
    mf = jnp.where(row >= 1, jnp.where((row - 1) * CMP_STRIDE + CMP_BLOCK - 1 <= qpos, 1.0, 0.0), 0.0)
    s = jnp.where(mf > 0.5, s, NEG_INF)
    row_t = lax.broadcasted_iota(jnp.int32, (SUBLANES, 1), 0)
    mf_t = jnp.where(row_t < CMP_TAIL,
                     jnp.where((nr - 1 + row_t) * CMP_STRIDE + CMP_BLOCK - 1 <= qpos, 1.0, 0.0), 0.0)
    s_t = jnp.where(mf_t > 0.5, jnp.dot(tail_ref[0], qbd, preferred_element_type=F32), NEG_INF)
    mx = jnp.maximum(jnp.max(s, axis=0, keepdims=True), jnp.max(s_t, axis=0, keepdims=True))
    p = jnp.exp(s - mx) * mf
    p_t = jnp.exp(s_t - mx) * mf_t
    den = jnp.maximum(jnp.sum(p, axis=0, keepdims=True) + jnp.sum(p_t, axis=0, keepdims=True), 1e-30)
    p = p / den
    p_t = jnp.concatenate([p_t / den, jnp.zeros((LANES - SUBLANES, LANES), F32)], axis=0)
    v_t = jnp.concatenate([tail_ref[1], jnp.zeros((LANES - SUBLANES, KV_WIDTH), BF16)], axis=0)
    o = (jnp.dot(jnp.transpose(p).astype(BF16), vc_ref[...], preferred_element_type=F32)
         + jnp.dot(jnp.transpose(p_t).astype(BF16), v_t, preferred_element_type=F32))
    oc_ref[...] = _diag_heads(o)
    impg = _dot_exact_right(p, gsum_ref[...])
    impg_t = _dot_exact_right(p_t, gsum_ref[...])
    imp_ref[...] = (_dot_exact_left(esum_ref[...], impg)
                    + _dot_exact_left(esum_t_ref[...], impg_t))


def _smp_cmp(q_pad, kc, vc, tail, n_blk_pad, qpos):
    db, nr, _ = kc.shape
    n_blk = (qpos + SLC_BLOCK) // SLC_BLOCK
    live = jnp.asarray(np.arange(n_blk_pad)[:, None] < n_blk, BF16)
    esum = _esum_matrix(n_blk_pad, nr, 1) * live
    esum_t = _esum_matrix(n_blk_pad, LANES, 1 - nr) * live * jnp.asarray(np.arange(LANES)[None, :] < CMP_TAIL, BF16)
    gs = (np.arange(LANES)[:, None] // HEADS_PER_GROUP == np.arange(LANES)[None, :]) & (
        np.arange(LANES)[:, None] < N_HEADS)
    gsum = jnp.asarray(gs, BF16)
    per_b = lambda shape: pl.BlockSpec((None,) + shape, lambda bi: (bi, 0, 0))
    const = lambda shape: pl.BlockSpec(shape, lambda bi: (0,) * len(shape))
    return pl.pallas_call(
        functools.partial(_smp_cmp_kernel, qpos=qpos),
        grid=(db,),
        in_specs=[per_b((LANES, HEAD_DIM)), per_b((nr, KV_WIDTH)), per_b((nr, KV_WIDTH)),
                  pl.BlockSpec((None, 2, SUBLANES, KV_WIDTH), lambda bi: (bi, 0, 0, 0)),
                  const((n_blk_pad, nr)), const((n_blk_pad, LANES)), const((LANES, LANES))],
        out_specs=[per_b((N_HEADS, HEAD_DIM)), per_b((n_blk_pad, LANES))],
        out_shape=[jax.ShapeDtypeStruct((db, N_HEADS, HEAD_DIM), F32),
                   jax.ShapeDtypeStruct((db, n_blk_pad, LANES), F32)],
        compiler_params=_cparams(("arbitrary",)),
        name="smp_cmp",
    )(q_pad, kc, vc, tail, esum, esum_t, gsum)


def _smp_topk_kernel(imp_ref, idx_ref, *, n_blk, n_sel):
    nbp = imp_ref.shape[0]
    blk = lax.broadcasted_iota(jnp.int32, (nbp, 1), 0)
    cur = n_blk - 1
    valid = blk <= cur
    forced = (blk == 0) | (blk == cur) | (blk == cur - 1)
    score = jnp.where(valid, imp_ref[...] + jnp.where(forced, FORCE_BONUS, 0.0), NEG_INF)
    rank = _rank(score, nbp)
    blkf = blk.astype(F32)
    for k in range(n_sel):
        pick = jnp.sum(jnp.where(rank == float(k), blkf, 0.0), axis=0, keepdims=True)
        idx_ref[k:k + 1, :] = pick.astype(jnp.int32)


def _smp_topk(score_t, n_blk, n_sel):
    nbp, w = score_t.shape
    full = lambda shape: pl.BlockSpec(shape, lambda: (0,) * len(shape))
    return pl.pallas_call(
        functools.partial(_smp_topk_kernel, n_blk=n_blk, n_sel=n_sel),
        in_specs=[full((nbp, w))],
        out_specs=full((n_sel, w)),
        out_shape=jax.ShapeDtypeStruct((n_sel, w), jnp.int32),
        name="smp_topk",
    )(score_t)


def _smp_attn_kernel(page_ref, half_ref, isnew_ref, *refs, n_slots, n_sel, past, wbuf):
    blk_refs = refs[:n_slots]
    (q_ref, ksn_ref, win_ref, kwn_ref, oc_ref, gt_ref, go_ref, o_ref, wout_ref) = refs[n_slots:]
    b = pl.program_id(0)
    lane_half = lax.broadcasted_iota(jnp.int32, (1, PAGE_SIZE), 1) // SLC_BLOCK
    row8 = lax.broadcasted_iota(jnp.int32, (SUBLANES, 1), 0)
    wlane = lax.broadcasted_iota(jnp.int32, (1, wbuf), 1)
    kposw = past - wbuf + wlane
    wbias = jnp.where((kposw > past - WINDOW) & (kposw >= 0), 0.0, NEG_INF)
    nt = (((1,), (1,)), ((), ()))

    def bf(x):
        return x.astype(BF16).astype(F32)

    ys = []
    for g in range(KV_GROUPS):
        q = q_ref[g] * ATTN_SCALE
        qb = q.astype(BF16)
        kn = bf(ksn_ref[:, g * HEAD_DIM:(g + 1) * HEAD_DIM])
        vn = bf(ksn_ref[:, KV_WIDTH + g * HEAD_DIM:KV_WIDTH + (g + 1) * HEAD_DIM])
        kwn = bf(kwn_ref[:, g * HEAD_DIM:(g + 1) * HEAD_DIM])
        vwn = bf(kwn_ref[:, KV_WIDTH + g * HEAD_DIM:KV_WIDTH + (g + 1) * HEAD_DIM])

        slots = range(g * n_sel, (g + 1) * n_sel)
        kt = jnp.concatenate([blk_refs[k][0] for k in slots], axis=1).astype(BF16)
        vt = jnp.concatenate([blk_refs[k][1] for k in slots], axis=1).astype(BF16)
        keeps = [jnp.where(isnew_ref[b, k] == 0, 1.0, 0.0) for k in slots]
        mf = jnp.concatenate([jnp.where(lane_half == half_ref[b, k], keep, 0.0)
                              for k, keep in zip(slots, keeps)], axis=1)
        has_new = 1.0 - functools.reduce(jnp.minimum, keeps)
        s = jnp.where(mf > 0.5, jnp.dot(qb, kt, preferred_element_type=F32), NEG_INF)
        s_n = jnp.where(has_new > 0.5, jnp.sum(bf(q) * kn, axis=1, keepdims=True), NEG_INF)
        m = jnp.maximum(jnp.max(s, axis=1, keepdims=True), s_n)
        p = jnp.exp(s - m) * mf
        p_n = jnp.exp(s_n - m) * has_new
        l = jnp.sum(p, axis=1, keepdims=True) + p_n
        o_s = (lax.dot_general(p.astype(BF16), vt, nt, preferred_element_type=F32) + bf(p_n) * vn)
        o_s = o_s / jnp.maximum(l, 1e-30)

        s = jnp.dot(qb, win_ref[0, g].astype(BF16), preferred_element_type=F32) + wbias
        s_n = jnp.sum(bf(q) * kwn, axis=1, keepdims=True)
        m = jnp.maximum(jnp.max(s, axis=1, keepdims=True), s_n)
        p = jnp.exp(s - m)
        p_n = jnp.exp(s_n - m)
        l = jnp.sum(p, axis=1, keepdims=True) + p_n
        o_w = (lax.dot_general(p.astype(BF16), win_ref[1, g].astype(BF16), nt, preferred_element_type=F32)
               + bf(p_n) * vwn)
        o_w = o_w / l

        gates = jax.nn.sigmoid(gt_ref[g])
        ys.append(gates[:, 0:1] * oc_ref[g] + gates[:, 1:2] * o_s + gates[:, 2:3] * o_w)

    real = row8 < HEADS_PER_GROUP
    ssq = functools.reduce(lambda a, c: a + c, [jnp.sum(jnp.where(real, y * y, 0.0), axis=1, keepdims=True)
                                                 for y in ys])
    scale = lax.rsqrt(jnp.sum(ssq, axis=0, keepdims=True) / ATT_WIDTH + EPS)
    for g in range(KV_GROUPS):
        o_ref[g] = (ys[g] * scale * go_ref[g]).astype(BF16)

    new_col = jnp.transpose(jnp.broadcast_to(kwn_ref[...], (SUBLANES, 2 * KV_WIDTH)))[:, 0:1]
    for s2 in range(2):
        for g in range(KV_GROUPS):
            c0 = (s2 * KV_GROUPS + g) * HEAD_DIM
            wout_ref[s2, g] = jnp.where(wlane == wbuf - 1, new_col[c0:c0 + HEAD_DIM],
                                        pltpu.roll(win_ref[s2, g], wbuf - 1, 1))


def _smp_attn(page, half, isnew, cache_slc, q_g, kvs_new, cache_win, kvw_new, o_c, gates, g_out, past):
    db, n_slots = page.shape
    n_sel = n_slots // KV_GROUPS
    wbuf = cache_win.shape[-1]
    width = 2 * KV_WIDTH
    blk_specs = [pl.BlockSpec((None, 2, None, HEAD_DIM, PAGE_SIZE), functools.partial(
        lambda bi, pg, hf, nw, k: (pg[bi, k], 0, k // n_sel, 0, 0), k=k)) for k in range(n_slots)]
    per_b3 = lambda shape: pl.BlockSpec((None,) + shape, lambda bi, pg, hf, nw: (bi, 0, 0))
    per_b4 = lambda shape: pl.BlockSpec((None,) + shape, lambda bi, pg, hf, nw: (bi, 0, 0, 0))
    per_b5 = lambda shape: pl.BlockSpec((None,) + shape, lambda bi, pg, hf, nw: (bi, 0, 0, 0, 0))
    win_shape = (2, KV_GROUPS, HEAD_DIM, wbuf)
    head_shape = (KV_GROUPS, SUBLANES, HEAD_DIM)
    return pl.pallas_call(
        functools.partial(_smp_attn_kernel, n_slots=n_slots, n_sel=n_sel, past=past, wbuf=wbuf),
        grid_spec=pltpu.PrefetchScalarGridSpec(
            num_scalar_prefetch=3,
            grid=(db,),
            in_specs=blk_specs + [per_b4(head_shape), per_b3((1, width)), per_b5(win_shape), per_b3((1, width)),
                                  per_b4(head_shape), per_b4((KV_GROUPS, SUBLANES, 3)),
                                  pl.BlockSpec(head_shape, lambda bi, pg, hf, nw: (0, 0, 0))],
            out_specs=[per_b4(head_shape), per_b5(win_shape)]),
        out_shape=[jax.ShapeDtypeStruct((db,) + head_shape, BF16),
                   jax.ShapeDtypeStruct((db,) + win_shape, F32)],
        compiler_params=_cparams(("arbitrary",)),
        name="smp_attn",
    )(page, half, isnew, *([cache_slc] * n_slots), q_g, kvs_new.reshape(db, 1, width), cache_win,
      kvw_new.reshape(db, 1, width), o_c, gates, g_out)


def _nsa_sample(za, gates_raw, kvc_new, kvs_new, kvw_new, cache_cmp, cache_slc, cache_win, page_table, wbd, pe8,
                wb, g_out):
    db, n_pages = page_table.shape
    past = n_pages * PAGE_SIZE
    n_blk = (past + SLC_BLOCK) // SLC_BLOCK
    n_sel = min(N_SEL, n_blk)
    n_blk_pad = -(-n_blk // SUBLANES) * SUBLANES
    q = za[:, 2 * SSM_WIDTH:].reshape(db, N_HEADS, HEAD_DIM)
    q_pad = jnp.pad(q, ((0, 0), (0, LANES - N_HEADS), (0, 0)))
    kc, vc, tail = _cmp_pages(cache_cmp, page_table, kvc_new, wbd, pe8, wb)
    o_c, imp = _smp_cmp(q_pad, kc, vc, tail, n_blk_pad, past)
    score_t = jnp.transpose(imp[:, :, :KV_GROUPS], (1, 0, 2)).reshape(n_blk_pad, db * KV_GROUPS)
    lane_pad = -(-db * KV_GROUPS // LANES) * LANES
    score_t = jnp.pad(score_t, ((0, 0), (0, lane_pad - db * KV_GROUPS)))
    idx = _smp_topk(score_t, n_blk, n_sel)[:, :db * KV_GROUPS]
    idx = jnp.transpose(idx.reshape(n_sel, db, KV_GROUPS), (1, 2, 0))
    n_past_blk = past // SLC_BLOCK
    per_page = PAGE_SIZE // SLC_BLOCK
    jp = jnp.minimum(idx, n_past_blk - 1).reshape(db, KV_GROUPS * n_sel)
    page = jnp.take_along_axis(page_table, jp // per_page, axis=1).astype(jnp.int32)
    half = (jp % per_page).astype(jnp.int32)
    isnew = (idx >= n_past_blk).reshape(db, KV_GROUPS * n_sel).astype(jnp.int32)
    pad_heads = lambda a: jnp.pad(a.reshape(a.shape[0], KV_GROUPS, HEADS_PER_GROUP, a.shape[-1]),
                                  ((0, 0), (0, 0), (0, SUBLANES - HEADS_PER_GROUP), (0, 0)))
    gates = pad_heads(gates_raw[:, :3 * N_HEADS].reshape(db, N_HEADS, 3))
    g_out_g = pad_heads(g_out.reshape(1, N_HEADS, HEAD_DIM))[0]
    y, win_new = _smp_attn(page, half, isnew, cache_slc, pad_heads(q), kvs_new, cache_win, kvw_new,
                           pad_heads(o_c), gates, g_out_g, past)
    return y[:, :, :HEADS_PER_GROUP].reshape(db, ATT_WIDTH), win_new


def _outproj_kernel(x_ref, ms_ref, ma_ref, w_ref, gt_ref, sc_ref, sh_ref, g_ref, x1_ref, h2_ref):
    mixed = jnp.concatenate([ms_ref[...], ma_ref[...]], axis=1)
    x1 = x_ref[...] + gt_ref[...] * jnp.dot(mixed, w_ref[...], preferred_element_type=F32)
    x1_ref[...] = x1
    h2_ref[...] = (_rms(x1, g_ref[...]) * (1.0 + sc_ref[...]) + sh_ref[...]).astype(BF16)


def _outproj(x, m_ssm, m_att, w_out, gt1, sc2, sh2, g2, tm, rows_per_mod):
    n, d = x.shape
    r = gt1.shape[1]
    tpm = rows_per_mod // tm
    mod_spec = pl.BlockSpec((None, r, d), lambda i: (i // tpm, 0, 0))
    row = lambda w: pl.BlockSpec((tm, w), lambda i: (i, 0))
    return pl.pallas_call(
        _outproj_kernel,
        grid=(n // tm,),
        in_specs=[row(d), row(SSM_WIDTH), row(ATT_WIDTH), pl.BlockSpec((d, d), lambda i: (0, 0)),
                  mod_spec, mod_spec, mod_spec, pl.BlockSpec((1, d), lambda i: (0, 0))],
        out_specs=[row(d), row(d)],
        out_shape=[jax.ShapeDtypeStruct((n, d), F32), jax.ShapeDtypeStruct((n, d), BF16)],
        compiler_params=_cparams(("arbitrary",)),
        name="outproj",
    )(x, m_ssm, m_att, w_out, gt1, sc2, sh2, g2.reshape(1, d))


def _ffn_act(a_v, a_g, a1_v, a1_g, a2_v, a2_g, cwv_ref, cwg_ref, cbv_ref, cbg_ref):
    val = cbv_ref[...] + cwv_ref[2:3, :] * a_v + cwv_ref[0:1, :] * a2_v + cwv_ref[1:2, :] * a1_v
    gate = cbg_ref[...] + cwg_ref[2:3, :] * a_g + cwg_ref[0:1, :] * a2_g + cwg_ref[1:2, :] * a1_g
    return (gate * jax.nn.sigmoid(gate) * val).astype(BF16)


def _ffn_finish(j, contrib, x1_ref, gt_ref, gf_ref, y_ref):
    @pl.when(j == 0)
    def _():
        y_ref[...] = contrib

    @pl.when(j > 0)
    def _():
        y_ref[...] += contrib

    @pl.when(j == FFN_NF - 1)
    def _():
        y_ref[...] = _rms(x1_ref[...] + gt_ref[...] * y_ref[...], gf_ref[...])


def _ffn_seq_kernel(h_ref, x1_ref, gt_ref, wv_ref, wg_ref, cwv_ref, cwg_ref, cbv_ref, cbg_ref, wd_ref, gf_ref,
                    y_ref, tv_ref, tg_ref, sv_ref, sg_ref, hv_ref, hg_ref, *, tm, tpb, rs):
    i = pl.program_id(0)
    j = pl.program_id(1)

    @pl.when(i % tpb == 0)
    def _():
        hv_ref[j] = jnp.zeros((SUBLANES, FFN_TF), F32)
        hg_ref[j] = jnp.zeros((SUBLANES, FFN_TF), F32)

    @pl.when(j == 0)
    def _():
        y_ref[...] = jnp.zeros(y_ref.shape, F32)

    sv_ref[0:SUBLANES, :] = hv_ref[j]
    sg_ref[0:SUBLANES, :] = hg_ref[j]

    def up(k):
        hk = h_ref[k * rs:(k + 1) * rs, :]
        o = SUBLANES + k * rs
        sv_ref[o:o + rs, :] = jnp.dot(hk, wv_ref[...], preferred_element_type=F32)
        sg_ref[o:o + rs, :] = jnp.dot(hk, wg_ref[...], preferred_element_type=F32)

    def down(k):
        o = SUBLANES + k * rs
        act = _ffn_act(sv_ref[o:o + rs, :], sg_ref[o:o + rs, :], sv_ref[o - 1:o - 1 + rs, :],
                       sg_ref[o - 1:o - 1 + rs, :], sv_ref[o - 2:o - 2 + rs, :], sg_ref[o - 2:o - 2 + rs, :],
                       cwv_ref, cwg_ref, cbv_ref, cbg_ref)
        y_ref[k * rs:(k + 1) * rs, :] += jnp.dot(act, wd_ref[...], preferred_element_type=F32)

    up(0)
    for k in range(tm // rs):
        if k + 1 < tm // rs:
            up(k + 1)
        down(k)

    for s_ref, halo_ref, t_ref in ((sv_ref, hv_ref, tv_ref), (sg_ref, hg_ref, tg_ref)):
        halo_ref[j] = s_ref[tm:tm + SUBLANES, :]
        t_ref[...] = s_ref[tm + SUBLANES - 2:tm + SUBLANES, :]

    @pl.when(j == FFN_NF - 1)
    def _():
        y_ref[...] = _rms(x1_ref[...] + gt_ref[...] * y_ref[...], gf_ref[...])


def _ffn_step_kernel(h_ref, x1_ref, gt_ref, wv_ref, wg_ref, cwv_ref, cwg_ref, cbv_ref, cbg_ref, wd_ref, gf_ref,
                     p0v_ref, p0g_ref, p1v_ref, p1g_ref, y_ref, av_ref, ag_ref):
    j = pl.program_id(1)
    h = h_ref[...]
    a_v = jnp.dot(h, wv_ref[...], preferred_element_type=F32)
    a_g = jnp.dot(h, wg_ref[...], preferred_element_type=F32)
    av_ref[...] = a_v
    ag_ref[...] = a_g
    act = _ffn_act(a_v, a_g, p1v_ref[...], p1g_ref[...], p0v_ref[...], p0g_ref[...],
                   cwv_ref, cwg_ref, cbv_ref, cbg_ref)
    _ffn_finish(j, jnp.dot(act, wd_ref[...], preferred_element_type=F32), x1_ref, gt_ref, gf_ref, y_ref)


def _ffn_specs(d, tm, r, tpm):
    row_once = pl.BlockSpec((tm, d), lambda i, j: (i, 0), pipeline_mode=pl.Buffered(1))
    return [row_once, row_once, pl.BlockSpec((None, r, d), lambda i, j: (i // tpm, 0, 0)),
            pl.BlockSpec((d, FFN_TF), lambda i, j: (0, j)), pl.BlockSpec((d, FFN_TF), lambda i, j: (0, j + FFN_NF)),
            pl.BlockSpec((CONV_W, FFN_TF), lambda i, j: (0, j)),
            pl.BlockSpec((CONV_W, FFN_TF), lambda i, j: (0, j + FFN_NF)),
            pl.BlockSpec((1, FFN_TF), lambda i, j: (0, j)), pl.BlockSpec((1, FFN_TF), lambda i, j: (0, j + FFN_NF)),
            pl.BlockSpec((FFN_TF, d), lambda i, j: (j, 0)), pl.BlockSpec((1, d), lambda i, j: (0, 0))]


def _ffn_seq(h2, x1, gt2, w_up, conv_w, conv_b, w_down, g_final, tm, rows_per_mod):
    n, d = x1.shape
    tpb = rows_per_mod // tm
    nt = n // tm
    cb = conv_b.reshape(1, -1)
    y, tv, tg = pl.pallas_call(
        functools.partial(_ffn_seq_kernel, tm=tm, tpb=tpb, rs=min(FFN_RS, tm)),
        grid=(nt, FFN_NF),
        in_specs=_ffn_specs(d, tm, gt2.shape[1], tpb),
        out_specs=[pl.BlockSpec((tm, d), lambda i, j: (i, 0), pipeline_mode=pl.Buffered(1)),
                   pl.BlockSpec((None, CONV_W - 1, FFN_TF), lambda i, j: (i, 0, j)),
                   pl.BlockSpec((None, CONV_W - 1, FFN_TF), lambda i, j: (i, 0, j))],
        out_shape=[jax.ShapeDtypeStruct((n, d), F32),
                   jax.ShapeDtypeStruct((nt, CONV_W - 1, D_FF), F32),
                   jax.ShapeDtypeStruct((nt, CONV_W - 1, D_FF), F32)],
        scratch_shapes=[pltpu.VMEM((tm + SUBLANES, FFN_TF), F32), pltpu.VMEM((tm + SUBLANES, FFN_TF), F32),
                        pltpu.VMEM((FFN_NF, SUBLANES, FFN_TF), F32), pltpu.VMEM((FFN_NF, SUBLANES, FFN_TF), F32)],
        compiler_params=_cparams(("arbitrary", "arbitrary")),
        name="ffn_seq",
    )(h2, x1, gt2, w_up, w_up, conv_w, conv_w, cb, cb, w_down, g_final.reshape(1, d))
    tails = jnp.concatenate([tv, tg], axis=-1)
    return y, tails[tpb - 1::tpb]


def _ffn_step(h2, x1, gt2, w_up, conv_w, conv_b, w_down, g_final, conv_prev):
    n, d = x1.shape
    cb = conv_b.reshape(1, -1)
    prev_v = pl.BlockSpec((n, FFN_TF), lambda i, j: (0, j))
    prev_g = pl.BlockSpec((n, FFN_TF), lambda i, j: (0, j + FFN_NF))
    p0, p1 = conv_prev[:, 0], conv_prev[:, 1]
    y, a_v, a_g = pl.pallas_call(
        _ffn_step_kernel,
        grid=(1, FFN_NF),
        in_specs=_ffn_specs(d, n, gt2.shape[1], 1) + [prev_v, prev_g, prev_v, prev_g],
        out_specs=[pl.BlockSpec((n, d), lambda i, j: (0, 0)),
                   pl.BlockSpec((n, FFN_TF), lambda i, j: (0, j)), pl.BlockSpec((n, FFN_TF), lambda i, j: (0, j))],
        out_shape=[jax.ShapeDtypeStruct((n, d), F32),
                   jax.ShapeDtypeStruct((n, D_FF), F32), jax.ShapeDtypeStruct((n, D_FF), F32)],
        compiler_params=_cparams(("arbitrary", "arbitrary")),
        name="ffn_step",
    )(h2, x1, gt2, w_up, w_up, conv_w, conv_w, cb, cb, w_down, g_final.reshape(1, d), p0, p0, p1, p1)
    return y, jnp.stack([p1, jnp.concatenate([a_v, a_g], axis=-1)], axis=1)


def kernel(x_prompt, x_sample, cache_cmp_kv, cache_slc_kv, cache_win_kv, state_ssm_re, state_ssm_im, state_conv,
           page_table, c_prompt, c_sample, w_ada, b_ada, g_norm1, w_in, ssm_lam_re, ssm_lam_im, ssm_log_dt,
           ssm_b_re, ssm_b_im, ssm_c_re, ssm_c_im, ssm_d, w_cmp, pe_cmp, g_out_ssm, g_out_att, w_out, g_norm2,
           w_up, conv_w, conv_b, w_down, g_final):
    depth = w_in.shape[0]
    b, s, d = x_prompt.shape
    db, ds, _ = x_sample.shape
    assert depth == 1 and ds == 1 and d == D_MODEL, "kernel is written for one layer and one new token per sequence"
    assert s % 512 == 0
    tm = 512
    tm_in = 256
    tm_ffn = min(1024, s)
    gp = SSM_GROUPS * SSM_STATE
    kv_shape = (2, KV_GROUPS, HEAD_DIM)
    l = 0

    w_in_p = jnp.pad(w_in[l], ((0, 0), (0, IN_PAD - IN_WIDTH))).astype(BF16)
    w_out_b = w_out[l].astype(BF16)
    w_up_b = w_up[l].astype(BF16)
    w_down_b = w_down[l].astype(BF16)
    pwr, pwi, bbr, bbi = _s5_prep(ssm_lam_re[l], ssm_lam_im[l], ssm_log_dt[l], ssm_b_re[l], ssm_b_im[l])
    w1, w2, lvl_r, lvl_i, pw_r, pw_i = _s5_weights(pwr, pwi, bbr, bbi, ssm_c_re[l], ssm_c_im[l])
    wbd, pe8, wb = _cmp_weights(w_cmp[l], pe_cmp[l])

    n_c = b + db
    n_c_pad = -(-n_c // SUBLANES) * SUBLANES
    c_all = jnp.pad(jnp.concatenate([c_prompt, c_sample], axis=0), ((0, n_c_pad - n_c), (0, 0)))
    mod = _ada(c_all, w_ada[l], b_ada[l]).reshape(n_c_pad, 6, d)
    mod_p = [mod[:b, k].reshape(b, 1, d) for k in range(6)]
    mod_s = [mod[b:n_c, k].reshape(1, db, d) for k in range(6)]

    xp = x_prompt.reshape(b * s, d)
    za, kvc, kvs, kvw, graw = _inproj(xp, mod_p[1], mod_p[0], g_norm1[l], w_in_p, tm_in, s)
    m_ssm, st_re, st_im = _s5_prompt(za, b, s, w1, w2, lvl_r, lvl_i, pw_r, pw_i, ssm_d[l], g_out_ssm[l], 256)
    kc, vct = _cmpproj(kvc, b, s, wbd, pe8, wb)
    kvc_t, kvs_t, ks_b, vst, kw_b, vwt = _kvprep(kvc, kvs, kvw, b, s, 512)
    rows_major = lambda a: jnp.transpose(a, (0, 4, 1, 2, 3))[None]
    m_att = _nsa_prompt(za, graw, ks_b, vst, kw_b, vwt, kc, vct, g_out_att[l], b, s)
    x1, h2 = _outproj(xp, m_ssm, m_att, w_out_b, mod_p[2], mod_p[4], mod_p[3], g_norm2[l], tm, s)
    y_p, conv_p = _ffn_seq(h2, x1, mod_p[5], w_up_b, conv_w[l], conv_b[l], w_down_b, g_final, tm_ffn, s)
    wlen = min(WINDOW, s)
    win_p = kvw.reshape(b, s, 2 * KV_WIDTH)[:, s - wlen:].reshape(b, wlen, *kv_shape)

    xs = x_sample.reshape(db, d)
    za_s, kvc_s, kvs_s, kvw_s, graw_s = _inproj(xs, mod_s[1], mod_s[0], g_norm1[l], w_in_p, db, db)
    m_ssm_s, st_re_s, st_im_s = _s5_sample(za_s, state_ssm_re[l].reshape(db, gp), state_ssm_im[l].reshape(db, gp),
                                           w1, w2, pw_r, pw_i, ssm_d[l], g_out_ssm[l])
    rows_minor = lambda c: jnp.transpose(c, (0, 2, 3, 4, 1))
    m_att_s, win_s = _nsa_sample(za_s, graw_s, kvc_s, kvs_s, kvw_s, rows_minor(cache_cmp_kv[l]),
                                 rows_minor(cache_slc_kv[l]), rows_minor(cache_win_kv[l]),
                                 page_table, wbd, pe8, wb, g_out_att[l])
    win_s = jnp.transpose(win_s, (0, 4, 1, 2, 3))
    x1_s, h2_s = _outproj(xs, m_ssm_s, m_att_s, w_out_b, mod_s[2], mod_s[4], mod_s[3], g_norm2[l], db, db)
    y_s, conv_s = _ffn_step(h2_s, x1_s, mod_s[5], w_up_b, conv_w[l], conv_b[l], w_down_b, g_final, state_conv[l])

    wbuf = cache_win_kv.shape[2]
    return (y_p.reshape(b, s, d), y_s.reshape(db, 1, d),
            rows_major(kvc_t), kvc_s.reshape(1, db, 1, *kv_shape),
            rows_major(kvs_t), kvs_s.reshape(1, db, 1, *kv_shape),
            win_p[None], win_s.reshape(1, db, wbuf, *kv_shape),
            st_re.reshape(1, b, SSM_GROUPS, SSM_STATE), st_im.reshape(1, b, SSM_GROUPS, SSM_STATE),
            st_re_s.reshape(1, db, SSM_GROUPS, SSM_STATE), st_im_s.reshape(1, db, SSM_GROUPS, SSM_STATE),
            conv_p[None], conv_s[None])
```
